```python
import jax, jax.numpy as jnp
from jax import lax
import numpy as np

D_MODEL = 2048
BATCH = 8
SEQ = 2048
DEPTH = 1

GRID_W = 64
CTX_LEN = 256
D_MIX = D_MODEL
RET_HEADS = 8
RET_HEAD_DIM = 128
D_RET = RET_HEADS * RET_HEAD_DIM
D_CONV = D_MIX - D_RET
D_IN = 4 * D_RET + 2 * D_CONV
CONV_WIDTH = 31
CONV_PAD = CONV_WIDTH // 2
CHUNK = 128
ROPE_BASE = 10000.0
N_EXPERTS = 32
TOP_K = 4
D_FF = D_MODEL
SWIGLU_ALPHA = 1.702
SWIGLU_LIMIT = 7.0
EPS = 1e-6
GN_EPS = 1e-5

kernel_name = 'hybrid_retention_conformer_moe_dit_layer'


def rms_norm(x, w):
    xf = x.astype(jnp.float32)
    y = xf * lax.rsqrt(jnp.mean(xf * xf, axis=-1, keepdims=True) + EPS)
    return (y * w.astype(jnp.float32)).astype(x.dtype)


def modulate(h, shift, scale):
    return h * (1 + scale) + shift


def to_heads(t):
    b, n, _ = t.shape
    return t.reshape(b, n, RET_HEADS, RET_HEAD_DIM).transpose(0, 2, 1, 3).astype(jnp.float32)


def rotary(t, pos):
    half = RET_HEAD_DIM // 2
    inv = ROPE_BASE ** (-jnp.arange(half, dtype=jnp.float32) / half)
    ang = pos[:, None] * inv[None, :]
    cos, sin = jnp.cos(ang), jnp.sin(ang)
    t1, t2 = t[..., :half], t[..., half:]
    return jnp.concatenate([t1 * cos - t2 * sin, t1 * sin + t2 * cos], axis=-1)


def retention_chunked(q, k, v, log_g, state0):
    b, h, n, d = q.shape
    nc = n // CHUNK

    def chunks(t):
        return t.reshape(b, h, nc, CHUNK, d).transpose(2, 0, 1, 3, 4)

    idx = jnp.arange(CHUNK, dtype=jnp.float32)
    diff = idx[:, None] - idx[None, :]
    decay_in = jnp.where(diff >= 0, jnp.exp(log_g[:, None, None] * jnp.maximum(diff, 0.0)), 0.0)
    decay_q = jnp.exp(log_g[:, None] * (idx + 1.0))[:, :, None]
    decay_k = jnp.exp(log_g[:, None] * (CHUNK - 1.0 - idx))[:, :, None]
    decay_c = jnp.exp(log_g * CHUNK)[:, None, None]

    def step(state, qkv):
        qi, ki, vi = qkv
        scores = jnp.einsum('bhid,bhjd->bhij', qi, ki) * decay_in
        out = (jnp.einsum('bhij,bhje->bhie', scores, vi)
               + jnp.einsum('bhid,bhde->bhie', qi, state) * decay_q)
        state = state * decay_c + jnp.einsum('bhjd,bhje->bhde', ki * decay_k, vi)
        return state, out

    _, out = lax.scan(step, state0, (chunks(q), chunks(k), chunks(v)))
    return out.transpose(1, 2, 0, 3, 4).reshape(b, h, n, d)


def retention_state(k, v, log_g):
    n = k.shape[2]
    w = jnp.exp(log_g[:, None] * (n - 1.0 - jnp.arange(n, dtype=jnp.float32)))
    return jnp.einsum('bhtd,bhte,ht->bhde', k, v, w)


def bidirectional_retention(q, k, v, lg_f, lg_b, state_f, state_b):
    fwd = retention_chunked(q, k, v, lg_f, state_f)
    bwd = retention_chunked(q[:, :, ::-1], k[:, :, ::-1], v[:, :, ::-1], lg_b, state_b)
    return fwd + bwd[:, :, ::-1]


def retention_group_out(o, g, gn_w):
    mu = jnp.mean(o, axis=-1, keepdims=True)
    var = jnp.mean(jnp.square(o - mu), axis=-1, keepdims=True)
    on = (o - mu) * lax.rsqrt(var + GN_EPS)
    b, h, n, d = o.shape
    on = on.transpose(0, 2, 1, 3).reshape(b, n, h * d) * gn_w.astype(jnp.float32)
    return on * jax.nn.silu(g.astype(jnp.float32))


def conformer_conv(a, gate, conv_w, conv_b, ln_w, ln_b, n_seq, seq_len):
    u = a * jax.nn.sigmoid(gate)
    b, n, ch = u.shape
    u = u.reshape(n_seq, seq_len, ch)
    u = lax.conv_general_dilated(u, conv_w[:, None, :].astype(u.dtype), window_strides=(1,),
                                 padding=[(CONV_PAD, CONV_PAD)],
                                 dimension_numbers=('NWC', 'WIO', 'NWC'),
                                 feature_group_count=ch)
    u = (u + conv_b).reshape(b, n, ch).astype(jnp.float32)
    mu = jnp.mean(u, axis=-1, keepdims=True)
    var = jnp.mean(jnp.square(u - mu), axis=-1, keepdims=True)
    u = (u - mu) * lax.rsqrt(var + EPS) * ln_w.astype(jnp.float32) + ln_b.astype(jnp.float32)
    return jax.nn.silu(u)


def parallel_mixer(h_lat, h_ctx, w_in, dec_f, dec_b, gn_w, conv_w, conv_b, ln_w, ln_b, w_out,
                   need_ctx_out):
    b, s, _ = h_lat.shape
    n_ctx = h_ctx.shape[1]
    rows = s // GRID_W
    pos_ctx = jnp.arange(n_ctx, dtype=jnp.float32)
    pos_lat = n_ctx + jnp.arange(s, dtype=jnp.float32)
    lg_f = jax.nn.log_sigmoid(dec_f.astype(jnp.float32))
    lg_b = jax.nn.log_sigmoid(dec_b.astype(jnp.float32))
    splits = [D_RET, 2 * D_RET, 3 * D_RET, 4 * D_RET, 4 * D_RET + D_CONV]
    q, k, v, g, ca, cb = jnp.split(h_lat @ w_in, splits, axis=-1)
    if need_ctx_out:
        qc, kc, vc, gc, cac, cbc = jnp.split(h_ctx @ w_in, splits, axis=-1)
    else:
        kc, vc = jnp.split(h_ctx @ w_in[:, D_RET:3 * D_RET], 2, axis=-1)
    q_scale = RET_HEAD_DIM ** -0.5
    kc = rotary(to_heads(kc), pos_ctx)
    vc = to_heads(vc)
    state_f = retention_state(kc, vc, lg_f)
    state_b = retention_state(kc[:, :, ::-1], vc[:, :, ::-1], lg_b)
    o_lat = bidirectional_retention(rotary(to_heads(q), pos_lat) * q_scale, rotary(to_heads(k), pos_lat),
                                    to_heads(v), lg_f, lg_b, state_f, state_b)
    y_lat = jnp.concatenate([retention_group_out(o_lat, g, gn_w),
                             conformer_conv(ca, cb, conv_w, conv_b, ln_w, ln_b, b * rows, GRID_W)], axis=-1)
    mix_lat = y_lat.astype(h_lat.dtype) @ w_out
    if not need_ctx_out:
        return mix_lat, None
    zero = jnp.zeros((b, RET_HEADS, RET_HEAD_DIM, RET_HEAD_DIM), jnp.float32)
    o_ctx = bidirectional_retention(rotary(to_heads(qc), pos_ctx) * q_scale, kc, vc, lg_f, lg_b, zero, zero)
    y_ctx = jnp.concatenate([retention_group_out(o_ctx, gc, gn_w),
                             conformer_conv(cac, cbc, conv_w, conv_b, ln_w, ln_b, b, n_ctx)], axis=-1)
    return mix_lat, y_ctx.astype(h_ctx.dtype) @ w_out


def moe_ffn(h, router_w, router_b, w1, b1, w2, b2):
    shp = h.shape
    hf = h.reshape(-1, shp[-1])
    logits = (hf @ router_w + router_b).astype(jnp.float32)
    top_val, top_idx = lax.top_k(logits, TOP_K)
    top_w = jax.nn.softmax(top_val, axis=-1)
    gates = jnp.einsum('nk,nke->ne', top_w, jax.nn.one_hot(top_idx, N_EXPERTS, dtype=jnp.float32))
    out = jnp.zeros(hf.shape, jnp.float32)
    for e in range(N_EXPERTS):
        u = hf @ w1[e] + b1[e]
        glu = jnp.minimum(u[:, :D_FF], SWIGLU_LIMIT)
        lin = jnp.clip(u[:, D_FF:], -SWIGLU_LIMIT, SWIGLU_LIMIT)
        act = glu * jax.nn.sigmoid(SWIGLU_ALPHA * glu) * (lin + 1)
        out = out + gates[:, e:e + 1] * (act @ w2[e] + b2[e]).astype(jnp.float32)
    return out.reshape(shp).astype(h.dtype)


def setup_inputs(seed: int = 0) -> dict:
    key = jax.random.key(seed)
    ks = jax.random.split(key, 26)

    def nrm(k, shape, scale):
        return scale * jax.random.normal(k, shape, jnp.float32)

    base = 1.0 - 2.0 ** (-5.0 - np.arange(RET_HEADS, dtype=np.float32))
    decay_logit = jnp.asarray(np.log(base / (1.0 - base)), jnp.float32)
    return {
        'x': nrm(ks[0], (BATCH, SEQ, D_MODEL), 1.0),
        'c': nrm(ks[1], (BATCH, D_MODEL), 1.0),
        'ctx': nrm(ks[2], (BATCH, CTX_LEN, D_MODEL), 1.0),
        'c_ctx': nrm(ks[3], (D_MODEL,), 1.0),
        'ada_w': nrm(ks[4], (DEPTH, D_MODEL, 6 * D_MODEL), 0.5 * D_MODEL ** -0.5),
        'ada_b': nrm(ks[5], (DEPTH, 6 * D_MODEL), 0.02),
        'pre_mix_norm': 1.0 + nrm(ks[6], (DEPTH, D_MODEL), 0.05),
        'post_mix_norm': 1.0 + nrm(ks[7], (DEPTH, D_MODEL), 0.05),
        'pre_ffn_norm': 1.0 + nrm(ks[8], (DEPTH, D_MODEL), 0.05),
        'post_ffn_norm': 1.0 + nrm(ks[9], (DEPTH, D_MODEL), 0.05),
        'w_in': nrm(ks[10], (DEPTH, D_MODEL, D_IN), D_MODEL ** -0.5),
        'ret_decay_fwd': decay_logit[None, :] + nrm(ks[11], (DEPTH, RET_HEADS), 0.1),
        'ret_decay_bwd': decay_logit[None, :] + nrm(ks[12], (DEPTH, RET_HEADS), 0.1),
        'ret_gn_w': 1.0 + nrm(ks[13], (DEPTH, D_RET), 0.05),
        'conv_w': nrm(ks[14], (DEPTH, CONV_WIDTH, D_CONV), CONV_WIDTH ** -0.5),
        'conv_b': nrm(ks[15], (DEPTH, D_CONV), 0.02),
        'conv_ln_w': 1.0 + nrm(ks[16], (DEPTH, D_CONV), 0.05),
        'conv_ln_b': nrm(ks[17], (DEPTH, D_CONV), 0.02),
        'w_out': nrm(ks[18], (DEPTH, D_MIX, D_MODEL), D_MIX ** -0.5),
        'router_w': nrm(ks[19], (DEPTH, D_MODEL, N_EXPERTS), D_MODEL ** -0.5),
        'router_b': nrm(ks[20], (DEPTH, N_EXPERTS), 0.01),
        'w1': nrm(ks[21], (DEPTH, N_EXPERTS, D_MODEL, 2 * D_FF), D_MODEL ** -0.5),
        'b1': nrm(ks[22], (DEPTH, N_EXPERTS, 2 * D_FF), 0.02),
        'w2': nrm(ks[23], (DEPTH, N_EXPERTS, D_FF, D_MODEL), D_FF ** -0.5),
        'b2': nrm(ks[24], (DEPTH, N_EXPERTS, D_MODEL), 0.02),
    }


def reference(x, c, ctx, c_ctx, ada_w, ada_b, pre_mix_norm, post_mix_norm, pre_ffn_norm, post_ffn_norm,
              w_in, ret_decay_fwd, ret_decay_bwd, ret_gn_w, conv_w, conv_b, conv_ln_w, conv_ln_b, w_out,
              router_w, router_b, w1, b1, w2, b2):
    ctx_h = ctx
    for l in range(DEPTH):
        last = l == DEPTH - 1
        mod_lat = jax.nn.silu(c) @ ada_w[l] + ada_b[l]
        sh1, sc1, g1, sh2, sc2, g2 = [m[:, None, :] for m in jnp.split(mod_lat, 6, axis=-1)]
        mod_ctx = jax.nn.silu(c_ctx) @ ada_w[l] + ada_b[l]
        csh1, csc1, cg1, csh2, csc2, cg2 = jnp.split(mod_ctx, 6, axis=-1)

        h_lat = modulate(rms_norm(x, pre_mix_norm[l]), sh1, sc1)
        h_ctx = modulate(rms_norm(ctx_h, pre_mix_norm[l]), csh1, csc1)
        mix_lat, mix_ctx = parallel_mixer(h_lat, h_ctx, w_in[l], ret_decay_fwd[l], ret_decay_bwd[l],
                                          ret_gn_w[l], conv_w[l], conv_b[l], conv_ln_w[l], conv_ln_b[l],
                                          w_out[l], not last)
        x = x + g1 * rms_norm(mix_lat, post_mix_norm[l])

        h = modulate(rms_norm(x, pre_ffn_norm[l]), sh2, sc2)
        x = x + g2 * rms_norm(moe_ffn(h, router_w[l], router_b[l], w1[l], b1[l], w2[l], b2[l]),
                              post_ffn_norm[l])

        if not last:
            ctx_h = ctx_h + cg1 * rms_norm(mix_ctx, post_mix_norm[l])
            hc = modulate(rms_norm(ctx_h, pre_ffn_norm[l]), csh2, csc2)
            ctx_h = ctx_h + cg2 * rms_norm(moe_ffn(hc, router_w[l], router_b[l], w1[l], b1[l], w2[l], b2[l]),
                                            post_ffn_norm[l])
    return x
```

```python
import functools
import math

import jax
import jax.numpy as jnp
from jax import lax
from jax.experimental import pallas as pl
from jax.experimental.pallas import tpu as pltpu

F32 = jnp.float32
BF16 = jnp.bfloat16
HIGHEST = lax.Precision.HIGHEST

GRID_W = 64
HEAD_DIM = 128
CHUNK = 128
ROPE_BASE = 10000.0
TOP_K = 4
SWIGLU_ALPHA = 1.702
SWIGLU_LIMIT = 7.0
EPS = 1e-6
GN_EPS = 1e-5
LANES = 128
MOD_ROWS = 16
VMEM_LIMIT = 56 * 1024 * 1024


def _tile(n, pref):
    t = min(n, pref)
    while n % t:
        t -= 1
    return t


def _params(sem, vmem=VMEM_LIMIT):
    return pltpu.CompilerParams(dimension_semantics=sem, vmem_limit_bytes=vmem)


def _rms(x, w):
    return x * lax.rsqrt(jnp.mean(x * x, axis=-1, keepdims=True) + EPS) * w


def _silu(x):
    return x * jax.nn.sigmoid(x)


def _ada_kernel(c_ref, w_ref, b_ref, o_ref):
    s = _silu(c_ref[...])
    o_ref[...] = jnp.dot(s, w_ref[...], preferred_element_type=F32, precision=HIGHEST) + b_ref[...]


def _ada(cc, w, b):
    d, n = w.shape
    tn = _tile(n, 1024)
    return pl.pallas_call(
        _ada_kernel,
        out_shape=jax.ShapeDtypeStruct((MOD_ROWS, n), F32),
        grid=(n // tn,),
        in_specs=[pl.BlockSpec((MOD_ROWS, d), lambda j: (0, 0)),
                  pl.BlockSpec((d, tn), lambda j: (0, j)),
                  pl.BlockSpec((1, tn), lambda j: (0, j))],
        out_specs=pl.BlockSpec((MOD_ROWS, tn), lambda j: (0, j)),
        compiler_params=_params(("parallel",)),
        name="ada",
    )(cc, w, b)


def _rope_kernel(cos_ref, sin_ref):
    p, _ = cos_ref.shape
    half = HEAD_DIM // 2
    lane = lax.broadcasted_iota(jnp.int32, (p, HEAD_DIM), 1)
    pos = lax.broadcasted_iota(jnp.int32, (p, HEAD_DIM), 0).astype(F32)
    j = jnp.where(lane < half, lane, lane - half).astype(F32)
    inv = jnp.exp(j * (-jnp.log(ROPE_BASE) / half))
    ang = pos * inv
    cos_ref[...] = jnp.cos(ang)
    sin_ref[...] = jnp.where(lane < half, -1.0, 1.0) * jnp.sin(ang)


def _rope_tables(p):
    return pl.pallas_call(
        _rope_kernel,
        out_shape=(jax.ShapeDtypeStruct((p, HEAD_DIM), F32), jax.ShapeDtypeStruct((p, HEAD_DIM), F32)),
        name="rope",
    )()


def _rot(t, cos, sin_signed):
    return t * cos + pltpu.roll(t, HEAD_DIM // 2, axis=1) * sin_signed


def _inproj_kernel(x_ref, nw_ref, sh_ref, sc_ref, w_ref, o_ref, h_scr):
    @pl.when(pl.program_id(2) == 0)
    def _():
        h = _rms(x_ref[0], nw_ref[...]) * (1.0 + sc_ref[0]) + sh_ref[0]
        h_scr[...] = h.astype(BF16)

    o_ref[0] = jnp.dot(h_scr[...], w_ref[...], preferred_element_type=F32)


def _in_proj(x, norm_w, mod3, mod_row, w_bf, col0, ncols):
    b, t, d = x.shape
    tm = _tile(t, 1024)
    tn = _tile(math.gcd(ncols, col0), 512)
    joff = col0 // tn
    return pl.pallas_call(
        _inproj_kernel,
        out_shape=jax.ShapeDtypeStruct((b, t, ncols), F32),
        grid=(b, t // tm, ncols // tn),
        in_specs=[pl.BlockSpec((1, tm, d), lambda bi, i, j: (bi, i, 0)),
                  pl.BlockSpec((1, d), lambda bi, i, j: (0, 0)),
                  pl.BlockSpec((1, 1, d), lambda bi, i, j: (mod_row(bi), 0, 0)),
                  pl.BlockSpec((1, 1, d), lambda bi, i, j: (mod_row(bi), 0, 1)),
                  pl.BlockSpec((d, tn), lambda bi, i, j: (0, j + joff))],
        out_specs=pl.BlockSpec((1, tm, tn), lambda bi, i, j: (bi, i, j)),
        scratch_shapes=[pltpu.VMEM((tm, d), BF16)],
        compiler_params=_params(("parallel", "parallel", "arbitrary")),
        name="in_proj",
    )(x, norm_w, mod3, mod3, w_bf)


def _log_sigmoid(x):
    return jnp.minimum(x, 0.0) - jnp.log(1.0 + jnp.exp(-jnp.abs(x)))


def _dot_nt(a, b):
    return lax.dot_general(a, b, (((1,), (1,)), ((), ())), preferred_element_type=F32)


def _dot_tn(a, b):
    return lax.dot_general(a, b, (((0,), (0,)), ((), ())), preferred_element_type=F32)


def _ret_kernel(q_ref, k_ref, v_ref, g_ref, kc_ref, vc_ref, cos_ref, sin_ref, decf_ref, decb_ref,
                gnw_ref, o_ref, q_scr, k_scr, v_scr, o_scr, *, n_ctx):
    t = q_ref.shape[1]
    c = CHUNK
    nc = t // c
    lgf = _log_sigmoid(decf_ref[0])[:, 0:1]
    lgb = _log_sigmoid(decb_ref[0])[:, 0:1]

    kc = _rot(kc_ref[0], cos_ref[0:n_ctx, :], sin_ref[0:n_ctx, :])
    vc = vc_ref[0].astype(BF16)
    tc = lax.broadcasted_iota(jnp.int32, (n_ctx, 1), 0).astype(F32)
    rf0 = _dot_tn((kc * jnp.exp(lgf * (n_ctx - 1.0 - tc))).astype(BF16), vc)
    rb0 = _dot_tn((kc * jnp.exp(lgb * tc)).astype(BF16), vc)

    scale = HEAD_DIM ** -0.5
    q_scr[...] = (_rot(q_ref[0], cos_ref[n_ctx:n_ctx + t, :], sin_ref[n_ctx:n_ctx + t, :]) * scale).astype(BF16)
    k_scr[...] = _rot(k_ref[0], cos_ref[n_ctx:n_ctx + t, :], sin_ref[n_ctx:n_ctx + t, :])
    v_scr[...] = v_ref[0].astype(BF16)

    ri = lax.broadcasted_iota(jnp.int32, (c, 1), 0).astype(F32)
    dq_f = jnp.exp(lgf * (ri + 1.0))
    dk_f = jnp.exp(lgf * (c - 1.0 - ri))
    gc_f = jnp.exp(lgf * float(c))
    dq_b = jnp.exp(lgb * (c - ri))
    dk_b = jnp.exp(lgb * ri)
    gc_b = jnp.exp(lgb * float(c))
    diff = (lax.broadcasted_iota(jnp.int32, (c, c), 0) - lax.broadcasted_iota(jnp.int32, (c, c), 1)).astype(F32)
    d_in = jnp.where(diff > 0, jnp.exp(lgf * jnp.maximum(diff, 0.0)),
                     jnp.where(diff < 0, jnp.exp(lgb * jnp.maximum(-diff, 0.0)), 2.0))

    def fwd(i, rf):
        sl = pl.ds(pl.multiple_of(i * c, c), c)
        qi = q_scr[sl, :]
        ki = k_scr[sl, :]
        vi = v_scr[sl, :]
        scores = _dot_nt(qi, ki.astype(BF16)) * d_in
        o_scr[sl, :] = (jnp.dot(scores.astype(BF16), vi, preferred_element_type=F32)
                        + jnp.dot(qi, rf.astype(BF16), preferred_element_type=F32) * dq_f)
        return rf * gc_f + _dot_tn((ki * dk_f).astype(BF16), vi)

    lax.fori_loop(0, nc, fwd, rf0)

    def bwd(n, rb):
        i = nc - 1 - n
        sl = pl.ds(pl.multiple_of(i * c, c), c)
        qi = q_scr[sl, :]
        o = o_scr[sl, :] + jnp.dot(qi, rb.astype(BF16), preferred_element_type=F32) * dq_b
        mu = jnp.mean(o, axis=-1, keepdims=True)
        var = jnp.mean(jnp.square(o - mu), axis=-1, keepdims=True)
        on = (o - mu) * lax.rsqrt(var + GN_EPS) * gnw_ref[...]
        o_ref[0, sl, :] = (on * _silu(g_ref[0, sl, :])).astype(o_ref.dtype)
        return rb * gc_b + _dot_tn((k_scr[sl, :] * dk_b).astype(BF16), v_scr[sl, :])

    lax.fori_loop(0, nc, bwd, rb0)


def _retention(proj, ctx_kv, cos, sin, decf3, decb3, gn_w, n_heads):
    b, t, _ = proj.shape
    n_ctx = ctx_kv.shape[1]
    hd = HEAD_DIM
    lat = lambda g: pl.BlockSpec((1, t, hd), lambda bi, h: (bi, 0, g * n_heads + h))
    return pl.pallas_call(
        functools.partial(_ret_kernel, n_ctx=n_ctx),
        out_shape=jax.ShapeDtypeStruct((b, t, n_heads * hd), BF16),
        grid=(b, n_heads),
        in_specs=[lat(0), lat(1), lat(2), lat(3),
                  pl.BlockSpec((1, n_ctx, hd), lambda bi, h: (bi, 0, h)),
                  pl.BlockSpec((1, n_ctx, hd), lambda bi, h: (bi, 0, n_heads + h)),
                  pl.BlockSpec(cos.shape, lambda bi, h: (0, 0)),
                  pl.BlockSpec(sin.shape, lambda bi, h: (0, 0)),
                  pl.BlockSpec((1, 1, LANES), lambda bi, h: (h, 0, 0)),
                  pl.BlockSpec((1, 1, LANES), lambda bi, h: (h, 0, 0)),
                  pl.BlockSpec((1, hd), lambda bi, h: (0, h))],
        out_specs=pl.BlockSpec((1, t, hd), lambda bi, h: (bi, 0, h)),
        scratch_shapes=[pltpu.VMEM((t, hd), BF16), pltpu.VMEM((t, hd), F32),
                        pltpu.VMEM((t, hd), BF16), pltpu.VMEM((t, hd), F32)],
        compiler_params=_params(("parallel", "parallel")),
        name="ret",
    )(proj, proj, proj, proj, ctx_kv, ctx_kv, cos, sin, decf3, decb3, gn_w)


CONV_PAD_ROWS = 16
CONV_LANE_CHUNK = 256


def _conv_kernel(a_ref, b_ref, w_ref, cb_ref, lnw_ref, lnb_ref, o_ref, up_scr, y_scr):
    tt, ch = a_ref.shape[1], a_ref.shape[2]
    kw = w_ref.shape[0]
    n_seq = tt // GRID_W
    lead = CONV_PAD_ROWS - kw // 2
    u = a_ref[0] * jax.nn.sigmoid(b_ref[0])
    zeros = jnp.zeros((CONV_PAD_ROWS, ch), F32)
    for s in range(n_seq):
        up_scr[s, 0:CONV_PAD_ROWS, :] = zeros
        up_scr[s, CONV_PAD_ROWS:CONV_PAD_ROWS + GRID_W, :] = u[s * GRID_W:(s + 1) * GRID_W, :]
        up_scr[s, CONV_PAD_ROWS + GRID_W:, :] = zeros

    cw = min(ch, CONV_LANE_CHUNK)

    def seq(s, carry):
        row0 = pl.multiple_of(s * GRID_W, GRID_W)
        for c0 in range(0, ch, cw):
            acc = jnp.broadcast_to(cb_ref[:, c0:c0 + cw], (GRID_W, cw))
            for k in range(kw):
                acc = acc + up_scr[s, lead + k:lead + k + GRID_W, c0:c0 + cw] * w_ref[k:k + 1, c0:c0 + cw]
            y_scr[pl.ds(row0, GRID_W), c0:c0 + cw] = acc
        return carry

    lax.fori_loop(0, n_seq, seq, 0)
    y = y_scr[...]
    mu = jnp.mean(y, axis=-1, keepdims=True)
    var = jnp.mean(jnp.square(y - mu), axis=-1, keepdims=True)
    yn = (y - mu) * lax.rsqrt(var + EPS) * lnw_ref[...] + lnb_ref[...]
    o_ref[0] = _silu(yn).astype(o_ref.dtype)


def _conv(proj, col0, conv_w, conv_b, ln_w, ln_b):
    b, t, _ = proj.shape
    kw, ch = conv_w.shape
    assert col0 % ch == 0 and kw // 2 <= CONV_PAD_ROWS and t % GRID_W == 0
    tt = _tile(t, 4 * GRID_W)
    ca, cb = col0 // ch, col0 // ch + 1
    vec = pl.BlockSpec((1, ch), lambda bi, i: (0, 0))
    return pl.pallas_call(
        _conv_kernel,
        out_shape=jax.ShapeDtypeStruct((b, t, ch), BF16),
        grid=(b, t // tt),
        in_specs=[pl.BlockSpec((1, tt, ch), lambda bi, i: (bi, i, ca)),
                  pl.BlockSpec((1, tt, ch), lambda bi, i: (bi, i, cb)),
                  pl.BlockSpec((kw, ch), lambda bi, i: (0, 0)),
                  vec, vec, vec],
        out_specs=pl.BlockSpec((1, tt, ch), lambda bi, i: (bi, i, 0)),
        scratch_shapes=[pltpu.VMEM((tt // GRID_W, GRID_W + 2 * CONV_PAD_ROWS, ch), F32),
                        pltpu.VMEM((tt, ch), F32)],
        compiler_params=_params(("parallel", "parallel")),
        name="conv",
    )(proj, proj, conv_w, conv_b, ln_w, ln_b)


def _outproj_kernel(yr_ref, yc_ref, wr_ref, wc_ref, x_ref, g1_ref, sh2_ref, sc2_ref, pmn_ref, pfn_ref,
                    rw_ref, rb_ref, x1_ref, h2_ref, lg_ref):
    mix = (jnp.dot(yr_ref[0], wr_ref[...], preferred_element_type=F32)
           + jnp.dot(yc_ref[0], wc_ref[...], preferred_element_type=F32))
    x1 = x_ref[0] + g1_ref[0] * _rms(mix, pmn_ref[...])
    x1_ref[0] = x1
    h2 = _rms(x1, pfn_ref[...]) * (1.0 + sc2_ref[0]) + sh2_ref[0]
    h2_ref[0] = h2
    lg_ref[0] = jnp.dot(h2, rw_ref[...], preferred_element_type=F32, precision=HIGHEST) + rb_ref[...]


def _out_proj(y_ret, y_conv, w_out_bf, x, mod3, post_mix_norm, pre_ffn_norm, rw_pad, rb_pad):
    b, t, d = x.shape
    d_ret, d_conv = y_ret.shape[2], y_conv.shape[2]
    tm = _tile(t, 512)
    mod = lambda col: pl.BlockSpec((1, 1, d), lambda bi, i: (bi, 0, col))
    vec = pl.BlockSpec((1, d), lambda bi, i: (0, 0))
    row = lambda width: pl.BlockSpec((1, tm, width), lambda bi, i: (bi, i, 0))
    w_ret, w_conv = w_out_bf[:d_ret], w_out_bf[d_ret:]
    return pl.pallas_call(
        _outproj_kernel,
        out_shape=(jax.ShapeDtypeStruct((b, t, d), F32), jax.ShapeDtypeStruct((b, t, d), F32),
                   jax.ShapeDtypeStruct((b, t, LANES), F32)),
        grid=(b, t // tm),
        in_specs=[row(d_ret), row(d_conv),
                  pl.BlockSpec((d_ret, d), lambda bi, i: (0, 0)),
                  pl.BlockSpec((d_conv, d), lambda bi, i: (0, 0)),
                  row(d), mod(2), mod(3), mod(4), vec, vec,
                  pl.BlockSpec((d, LANES), lambda bi, i: (0, 0)),
                  pl.BlockSpec((1, LANES), lambda bi, i: (0, 0))],
        out_specs=(row(d), row(d), row(LANES)),
        compiler_params=_params(("parallel", "parallel")),
        name="out_proj",
    )(y_ret, y_conv, w_ret, w_conv, x, mod3, mod3, mod3, post_mix_norm, pre_ffn_norm, rw_pad, rb_pad)


def _route_kernel(lg_ref, idx_ref, gate_ref, rank_ref, cnt_ref, tri_scr, run_scr, *, n_experts):
    tr = lg_ref.shape[0]

    @pl.when(pl.program_id(0) == 0)
    def _():
        r = lax.broadcasted_iota(jnp.int32, (tr, tr), 0)
        c = lax.broadcasted_iota(jnp.int32, (tr, tr), 1)
        tri_scr[...] = jnp.where(r <= c, 1.0, 0.0).astype(BF16)
        run_scr[...] = jnp.zeros_like(run_scr)

    logits = lg_ref[...].T
    e_iota = lax.broadcasted_iota(jnp.int32, (LANES, tr), 0)
    neg = jnp.float32(-jnp.inf)
    logits = jnp.where(e_iota < n_experts, logits, neg)
    vals, hots = [], []
    for k in range(TOP_K):
        m = jnp.max(logits, axis=0, keepdims=True)
        ik = jnp.min(jnp.where(logits == m, e_iota, LANES), axis=0, keepdims=True)
        hot = e_iota == ik
        logits = jnp.where(hot, neg, logits)
        vals.append(m)
        hots.append(hot)
        idx_ref[k:k + 1, :] = ik
    exps = [jnp.exp(v - vals[0]) for v in vals]
    den = exps[0]
    for e in exps[1:]:
        den = den + e
    for k in range(TOP_K):
        gate_ref[k:k + 1, :] = exps[k] / den

    sel = jnp.zeros((LANES, tr), F32)
    for hot in hots:
        sel = sel + jnp.where(hot, 1.0, 0.0)
    csum = jnp.dot(sel.astype(BF16), tri_scr[...], preferred_element_type=F32)
    before = run_scr[:, 0:1] + csum - sel
    for k in range(TOP_K):
        rank_ref[k:k + 1, :] = jnp.sum(jnp.where(hots[k], before, 0.0), axis=0, keepdims=True).astype(jnp.int32)
    run_scr[...] = run_scr[...] + jnp.sum(sel, axis=1, keepdims=True)
    cnt_ref[...] = run_scr[...]


def _route(logits, n_experts):
    n = logits.shape[0]
    tr = _tile(n, 512)
    kt = lambda dt: jax.ShapeDtypeStruct((TOP_K, n), dt)
    blk = pl.BlockSpec((TOP_K, tr), lambda i: (0, i))
    return pl.pallas_call(
        functools.partial(_route_kernel, n_experts=n_experts),
        out_shape=(kt(jnp.int32), kt(F32), kt(jnp.int32), jax.ShapeDtypeStruct((LANES, LANES), F32)),
        grid=(n // tr,),
        in_specs=[pl.BlockSpec((tr, LANES), lambda i: (i, 0))],
        out_specs=(blk, blk, blk, pl.BlockSpec((LANES, LANES), lambda i: (0, 0))),
        scratch_shapes=[pltpu.VMEM((tr, tr), BF16), pltpu.VMEM((LANES, LANES), F32)],
        compiler_params=_params(("arbitrary",)),
        name="route",
    )(logits)


def _dispatch_kernel(pos_ref, h_ref, xs_in_ref, xs_ref, sem, *, n_tok):
    del xs_in_ref
    td = h_ref.shape[0]
    base = pl.program_id(0) * td

    def row_copy(t, k):
        p = pos_ref[k * n_tok + base + t]
        return pltpu.make_async_copy(h_ref.at[pl.ds(t, 1)], xs_ref.at[pl.ds(p, 1)], sem)

    def start(t, carry):
        for k in range(TOP_K):
            row_copy(t, k).start()
        return carry

    def wait(t, carry):
        for k in range(TOP_K):
            row_copy(t, k).wait()
        return carry

    lax.fori_loop(0, td, start, 0)
    lax.fori_loop(0, td, wait, 0)


def _dispatch(pos_flat, h2, xs_init):
    n, d = h2.shape
    td = _tile(n, 256)
    return pl.pallas_call(
        functools.partial(_dispatch_kernel, n_tok=n),
        out_shape=jax.ShapeDtypeStruct(xs_init.shape, xs_init.dtype),
        grid_spec=pltpu.PrefetchScalarGridSpec(
            num_scalar_prefetch=1,
            grid=(n // td,),
            in_specs=[pl.BlockSpec((td, d), lambda i, pos: (i, 0)),
                      pl.BlockSpec(memory_space=pl.ANY)],
            out_specs=pl.BlockSpec(memory_space=pl.ANY),
            scratch_shapes=[pltpu.SemaphoreType.DMA(())]),
        input_output_aliases={2: 0},
        compiler_params=_params(("arbitrary",)),
        name="dispatch",
    )(pos_flat, h2, xs_init)


def _expert_changed(te_ref, i):
    return jnp.logical_or(i == 0, te_ref[i] != te_ref[jnp.maximum(i - 1, 0)])


def _ffn1_kernel(te_ref, nu_ref, xs_ref, wg_ref, wl_ref, bg_ref, bl_ref, act_ref, wg_scr, wl_scr):
    i = pl.program_id(1)

    @pl.when(_expert_changed(te_ref, i))
    def _():
        wg_scr[...] = wg_ref[0].astype(BF16)
        wl_scr[...] = wl_ref[0].astype(BF16)

    @pl.when(i < nu_ref[0])
    def _():
        x = xs_ref[...].astype(BF16)
        glu = jnp.minimum(jnp.dot(x, wg_scr[...], preferred_element_type=F32) + bg_ref[0], SWIGLU_LIMIT)
        lin = jnp.clip(jnp.dot(x, wl_scr[...], preferred_element_type=F32) + bl_ref[0],
                       -SWIGLU_LIMIT, SWIGLU_LIMIT)
        act_ref[...] = (glu * jax.nn.sigmoid(SWIGLU_ALPHA * glu) * (lin + 1.0)).astype(act_ref.dtype)

    @pl.when(i >= nu_ref[0])
    def _():
        act_ref[...] = jnp.zeros_like(act_ref)


def _ffn1(tile_expert, n_used, xs, w1, b1_3, tm):
    slots, d = xs.shape
    n_exp, _, two_ff = w1.shape
    d_ff = two_ff // 2
    tn = _tile(d_ff, 512)
    nj = d_ff // tn
    n_tiles = slots // tm
    row = lambda i, nu: jnp.minimum(i, nu[0] - 1)
    return pl.pallas_call(
        _ffn1_kernel,
        out_shape=jax.ShapeDtypeStruct((slots, d_ff), BF16),
        grid_spec=pltpu.PrefetchScalarGridSpec(
            num_scalar_prefetch=2,
            grid=(nj, n_tiles),
            in_specs=[pl.BlockSpec((tm, d), lambda j, i, te, nu: (row(i, nu), 0)),
                      pl.BlockSpec((1, d, tn), lambda j, i, te, nu: (te[i], 0, j)),
                      pl.BlockSpec((1, d, tn), lambda j, i, te, nu: (te[i], 0, nj + j)),
                      pl.BlockSpec((1, 1, tn), lambda j, i, te, nu: (te[i], 0, j)),
                      pl.BlockSpec((1, 1, tn), lambda j, i, te, nu: (te[i], 0, nj + j))],
            out_specs=pl.BlockSpec((tm, tn), lambda j, i, te, nu: (i, j)),
            scratch_shapes=[pltpu.VMEM((d, tn), BF16), pltpu.VMEM((d, tn), BF16)]),
        compiler_params=_params(("arbitrary", "arbitrary")),
        name="ffn1",
    )(tile_expert, n_used, xs, w1, w1, b1_3, b1_3)


def _ffn2_kernel(te_ref, nu_ref, act_ref, w_ref, b_ref, ys_ref, w_scr):
    i = pl.program_id(1)

    @pl.when(_expert_changed(te_ref, i))
    def _():
        w_scr[...] = w_ref[0].astype(BF16)

    @pl.when(i < nu_ref[0])
    def _():
        ys_ref[...] = jnp.dot(act_ref[...], w_scr[...], preferred_element_type=F32) + b_ref[0]

    @pl.when(i >= nu_ref[0])
    def _():
        ys_ref[...] = jnp.zeros_like(ys_ref)


def _ffn2(tile_expert, n_used, act, w2, b2_3, tm):
    slots, d_ff = act.shape
    d = w2.shape[2]
    tn = _tile(d, 1024)
    n_tiles = slots // tm
    row = lambda i, nu: jnp.minimum(i, nu[0] - 1)
    return pl.pallas_call(
        _ffn2_kernel,
        out_shape=jax.ShapeDtypeStruct((slots, d), F32),
        grid_spec=pltpu.PrefetchScalarGridSpec(
            num_scalar_prefetch=2,
            grid=(d // tn, n_tiles),
            in_specs=[pl.BlockSpec((tm, d_ff), lambda j, i, te, nu: (row(i, nu), 0)),
                      pl.BlockSpec((1, d_ff, tn), lambda j, i, te, nu: (te[i], 0, j)),
                      pl.BlockSpec((1, 1, tn), lambda j, i, te, nu: (te[i], 0, j))],
            out_specs=pl.BlockSpec((tm, tn), lambda j, i, te, nu: (i, j)),
            scratch_shapes=[pltpu.VMEM((d_ff, tn), BF16)]),
        compiler_params=_params(("arbitrary", "arbitrary")),
        name="ffn2",
    )(tile_expert, n_used, act, w2, b2_3)


def _combine_kernel(pos_ref, ys_ref, gate_ref, x1_ref, g2_ref, pfn_ref, o_ref, buf, sem, *, n_tok):
    tc = x1_ref.shape[0]
    base = pl.program_id(0) * tc

    def row_copy(t, k):
        p = pos_ref[k * n_tok + base + t]
        return pltpu.make_async_copy(ys_ref.at[pl.ds(p, 1)], buf.at[k, pl.ds(t, 1)], sem.at[k])

    def start(t, carry):
        for k in range(TOP_K):
            row_copy(t, k).start()
        return carry

    def wait(t, carry):
        for k in range(TOP_K):
            row_copy(t, k).wait()
        return carry

    lax.fori_loop(0, tc, start, 0)
    lax.fori_loop(0, tc, wait, 0)
    moe = buf[0] * gate_ref[:, 0:1]
    for k in range(1, TOP_K):
        moe = moe + buf[k] * gate_ref[:, k:k + 1]
    o_ref[...] = x1_ref[...] + g2_ref[0] * _rms(moe, pfn_ref[...])


def _combine(pos_flat, ys, gates_t, x1, mod3, post_ffn_norm, seq):
    n, d = x1.shape
    tc = _tile(seq, 128)
    per_batch = seq // tc
    return pl.pallas_call(
        functools.partial(_combine_kernel, n_tok=n),
        out_shape=jax.ShapeDtypeStruct((n, d), F32),
        grid_spec=pltpu.PrefetchScalarGridSpec(
            num_scalar_prefetch=1,
            grid=(n // tc,),
            in_specs=[pl.BlockSpec(memory_space=pl.ANY),
                      pl.BlockSpec((tc, TOP_K), lambda i, pos: (i, 0)),
                      pl.BlockSpec((tc, d), lambda i, pos: (i, 0)),
                      pl.BlockSpec((1, 1, d), lambda i, pos: (i // per_batch, 0, 5)),
                      pl.BlockSpec((1, d), lambda i, pos: (0, 0))],
            out_specs=pl.BlockSpec((tc, d), lambda i, pos: (i, 0)),
            scratch_shapes=[pltpu.VMEM((TOP_K, tc, d), F32), pltpu.SemaphoreType.DMA((TOP_K,))]),
        compiler_params=_params(("arbitrary",)),
        name="combine",
    )(pos_flat, ys, gates_t, x1, mod3, post_ffn_norm)


def _moe(h2, logits, x1, mod3, post_ffn_norm, w1, b1, w2, b2, seq):
    n, d = h2.shape
    n_exp = w1.shape[0]
    tm = _tile(n * TOP_K, 512)
    idx, gates, rank, cnt = _route(logits, n_exp)

    counts = cnt[:n_exp, 0].astype(jnp.int32)
    padded = (counts + tm - 1) // tm * tm
    ends = jnp.cumsum(padded)
    starts = ends - padded
    pos_flat = (starts[idx] + rank).reshape(-1)
    n_tiles = n * TOP_K // tm + n_exp
    n_used = (ends[-1] // tm).astype(jnp.int32)
    tile_start = jnp.minimum(jnp.arange(n_tiles, dtype=jnp.int32), n_used - 1) * tm
    tile_expert = jnp.sum(tile_start[:, None] >= ends[None, :], axis=1).astype(jnp.int32)
    n_used = n_used.reshape(1)

    xs = _dispatch(pos_flat, h2, jnp.zeros((n_tiles * tm, d), F32))
    act = _ffn1(tile_expert, n_used, xs, w1, b1[:, None, :], tm)
    ys = _ffn2(tile_expert, n_used, act, w2, b2[:, None, :], tm)
    return _combine(pos_flat, ys, gates.T, x1, mod3, post_ffn_norm, seq)


def kernel(x, c, ctx, c_ctx, ada_w, ada_b, pre_mix_norm, post_mix_norm, pre_ffn_norm, post_ffn_norm,
           w_in, ret_decay_fwd, ret_decay_bwd, ret_gn_w, conv_w, conv_b, conv_ln_w, conv_ln_b, w_out,
           router_w, router_b, w1, b1, w2, b2):
    assert ada_w.shape[0] == 1, "single-layer stack only"
    b, t, d = x.shape
    n_ctx = ctx.shape[1]
    n_heads = ret_decay_fwd.shape[1]
    d_ret = n_heads * HEAD_DIM
    n_exp = router_w.shape[2]
    assert b < MOD_ROWS and n_exp <= LANES and ret_gn_w.shape[1] == d_ret

    cc = jnp.zeros((MOD_ROWS, d), F32).at[:b].set(c).at[b].set(c_ctx)
    mod3 = _ada(cc, ada_w[0], ada_b)[:, None, :]
    cos, sin = _rope_tables(n_ctx + t)

    w_in_bf = w_in[0].astype(BF16)
    proj = _in_proj(x, pre_mix_norm, mod3, lambda bi: bi, w_in_bf, 0, w_in.shape[2])
    ctx_kv = _in_proj(ctx, pre_mix_norm, mod3, lambda bi: b, w_in_bf, d_ret, 2 * d_ret)

    lane_bcast = lambda v: jnp.broadcast_to(v.reshape(n_heads, 1, 1), (n_heads, 1, LANES))
    y_ret = _retention(proj, ctx_kv, cos, sin, lane_bcast(ret_decay_fwd[0]), lane_bcast(ret_decay_bwd[0]),
                       ret_gn_w, n_heads)
    y_conv = _conv(proj, 4 * d_ret, conv_w[0], conv_b, conv_ln_w, conv_ln_b)

    rw_pad = jnp.zeros((d, LANES), F32).at[:, :n_exp].set(router_w[0])
    rb_pad = jnp.zeros((1, LANES), F32).at[:, :n_exp].set(router_b)
    x1, h2, logits = _out_proj(y_ret, y_conv, w_out[0].astype(BF16), x, mod3, post_mix_norm, pre_ffn_norm,
                               rw_pad, rb_pad)

    out = _moe(h2.reshape(b * t, d), logits.reshape(b * t, LANES), x1.reshape(b * t, d), mod3,
               post_ffn_norm, w1[0], b1[0], w2[0], b2[0], t)
    return out.reshape(b, t, d)
```

```python
import functools
import math

import jax
import jax.numpy as jnp
from jax import lax
from jax.experimental import pallas as pl
from jax.experimental.pallas import tpu as pltpu

F32 = jnp.float32
BF16 = jnp.bfloat16
U32 = jnp.uint32
HIGHEST = lax.Precision.HIGHEST

GRID_W = 64
HEAD_DIM = 128
CHUNK = 128
ROPE_BASE = 10000.0
TOP_K = 4
SWIGLU_ALPHA = 1.702
SWIGLU_LIMIT = 7.0
EPS = 1e-6
GN_EPS = 1e-5
LANES = 128
SUBLANES = 8
MOD_ROWS = 16
VMEM_LIMIT = 56 * 1024 * 1024
HI16 = 0xFFFF0000


def _tile(n, pref):
    t = min(n, pref)
    while n % t:
        t -= 1
    return t


def _params(sem, vmem=VMEM_LIMIT):
    return pltpu.CompilerParams(dimension_semantics=sem, vmem_limit_bytes=vmem)


def _rms(x, w):
    return x * lax.rsqrt(jnp.mean(x * x, axis=-1, keepdims=True) + EPS) * w


def _silu(x):
    return x * jax.nn.sigmoid(x)


def _dot_nt(a, b):
    return lax.dot_general(a, b, (((1,), (1,)), ((), ())), preferred_element_type=F32)


def _dot_tn(a, b):
    return lax.dot_general(a, b, (((0,), (0,)), ((), ())), preferred_element_type=F32)


def _ada_kernel(c_ref, w_ref, b_ref, o_ref):
    s = _silu(c_ref[...])
    o_ref[...] = jnp.dot(s, w_ref[...], preferred_element_type=F32, precision=HIGHEST) + b_ref[...]


def _ada(cc, w, b):
    d, n = w.shape
    tn = _tile(n, 1024)
    return pl.pallas_call(
        _ada_kernel,
        out_shape=jax.ShapeDtypeStruct((MOD_ROWS, n), F32),
        grid=(n // tn,),
        in_specs=[pl.BlockSpec((MOD_ROWS, d), lambda j: (0, 0)),
                  pl.BlockSpec((d, tn), lambda j: (0, j)),
                  pl.BlockSpec((1, tn), lambda j: (0, j))],
        out_specs=pl.BlockSpec((MOD_ROWS, tn), lambda j: (0, j)),
        compiler_params=_params(("parallel",)),
        name="ada",
    )(cc, w, b)


def _rope_kernel(cos_ref, sin_ref):
    p, _ = cos_ref.shape
    half = HEAD_DIM // 2
    lane = lax.broadcasted_iota(jnp.int32, (p, HEAD_DIM), 1)
    pos = lax.broadcasted_iota(jnp.int32, (p, HEAD_DIM), 0).astype(F32)
    j = jnp.where(lane < half, lane, lane - half).astype(F32)
    inv = jnp.exp(j * (-jnp.log(ROPE_BASE) / half))
    ang = pos * inv
    cos_ref[...] = jnp.cos(ang)
    sin_ref[...] = jnp.where(lane < half, -1.0, 1.0) * jnp.sin(ang)


def _rope_tables(p):
    return pl.pallas_call(
        _rope_kernel,
        out_shape=(jax.ShapeDtypeStruct((p, HEAD_DIM), F32), jax.ShapeDtypeStruct((p, HEAD_DIM), F32)),
        name="rope",
    )()


def _rot(t, cos, sin_signed):
    return t * cos + pltpu.roll(t, HEAD_DIM // 2, axis=1) * sin_signed


def _inproj_kernel(x_ref, nw_ref, sh_ref, sc_ref, w_ref, o_ref, h_scr):
    @pl.when(pl.program_id(2) == 0)
    def _():
        h = _rms(x_ref[0], nw_ref[...]) * (1.0 + sc_ref[0]) + sh_ref[0]
        h_scr[...] = h.astype(BF16)

    o_ref[0] = jnp.dot(h_scr[...], w_ref[...], preferred_element_type=F32)


def _in_proj(x, norm_w, mod3, mod_row, w_bf, col0, ncols):
    b, t, d = x.shape
    tm = _tile(t, 1024)
    tn = _tile(math.gcd(ncols, col0), 512)
    joff = col0 // tn
    return pl.pallas_call(
        _inproj_kernel,
        out_shape=jax.ShapeDtypeStruct((b, t, ncols), F32),
        grid=(b, t // tm, ncols // tn),
        in_specs=[pl.BlockSpec((1, tm, d), lambda bi, i, j: (bi, i, 0)),
                  pl.BlockSpec((1, d), lambda bi, i, j: (0, 0)),
                  pl.BlockSpec((1, 1, d), lambda bi, i, j: (mod_row(bi), 0, 0)),
                  pl.BlockSpec((1, 1, d), lambda bi, i, j: (mod_row(bi), 0, 1)),
                  pl.BlockSpec((d, tn), lambda bi, i, j: (0, j + joff))],
        out_specs=pl.BlockSpec((1, tm, tn), lambda bi, i, j: (bi, i, j)),
        scratch_shapes=[pltpu.VMEM((tm, d), BF16)],
        compiler_params=_params(("parallel", "parallel", "arbitrary")),
        name="in_proj",
    )(x, norm_w, mod3, mod3, w_bf)


def _log_sigmoid(x):
    return jnp.minimum(x, 0.0) - jnp.log(1.0 + jnp.exp(-jnp.abs(x)))


def _ret_kernel(q_ref, k_ref, v_ref, g_ref, kc_ref, vc_ref, cos_ref, sin_ref, decf_ref, decb_ref,
                gnw_ref, o_ref, q_scr, k_scr, v_scr, o_scr, *, n_ctx):
    t = q_ref.shape[1]
    c = CHUNK
    nc = t // c
    lgf = _log_sigmoid(decf_ref[0])[:, 0:1]
    lgb = _log_sigmoid(decb_ref[0])[:, 0:1]

    kc = _rot(kc_ref[0], cos_ref[0:n_ctx, :], sin_ref[0:n_ctx, :])
    vc = vc_ref[0].astype(BF16)
    tc = lax.broadcasted_iota(jnp.int32, (n_ctx, 1), 0).astype(F32)
    rf0 = _dot_tn((kc * jnp.exp(lgf * (n_ctx - 1.0 - tc))).astype(BF16), vc)
    rb0 = _dot_tn((kc * jnp.exp(lgb * tc)).astype(BF16), vc)

    scale = HEAD_DIM ** -0.5
    q_scr[...] = (_rot(q_ref[0], cos_ref[n_ctx:n_ctx + t, :], sin_ref[n_ctx:n_ctx + t, :]) * scale).astype(BF16)
    k_scr[...] = _rot(k_ref[0], cos_ref[n_ctx:n_ctx + t, :], sin_ref[n_ctx:n_ctx + t, :])
    v_scr[...] = v_ref[0].astype(BF16)

    ri = lax.broadcasted_iota(jnp.int32, (c, 1), 0).astype(F32)
    dq_f = jnp.exp(lgf * (ri + 1.0))
    dk_f = jnp.exp(lgf * (c - 1.0 - ri))
    gc_f = jnp.exp(lgf * float(c))
    dq_b = jnp.exp(lgb * (c - ri))
    dk_b = jnp.exp(lgb * ri)
    gc_b = jnp.exp(lgb * float(c))
    diff = (lax.broadcasted_iota(jnp.int32, (c, c), 0) - lax.broadcasted_iota(jnp.int32, (c, c), 1)).astype(F32)
    d_in = jnp.where(diff > 0, jnp.exp(lgf * jnp.maximum(diff, 0.0)),
                     jnp.where(diff < 0, jnp.exp(lgb * jnp.maximum(-diff, 0.0)), 2.0))

    def fwd(i, rf):
        sl = pl.ds(pl.multiple_of(i * c, c), c)
        qi = q_scr[sl, :]
        ki = k_scr[sl, :]
        vi = v_scr[sl, :]
        scores = _dot_nt(qi, ki.astype(BF16)) * d_in
        o_scr[sl, :] = (jnp.dot(scores.astype(BF16), vi, preferred_element_type=F32)
                        + jnp.dot(qi, rf.astype(BF16), preferred_element_type=F32) * dq_f)
        return rf * gc_f + _dot_tn((ki * dk_f).astype(BF16), vi)

    lax.fori_loop(0, nc, fwd, rf0)

    def bwd(n, rb):
        i = nc - 1 - n
        sl = pl.ds(pl.multiple_of(i * c, c), c)
        qi = q_scr[sl, :]
        o = o_scr[sl, :] + jnp.dot(qi, rb.astype(BF16), preferred_element_type=F32) * dq_b
        mu = jnp.mean(o, axis=-1, keepdims=True)
        var = jnp.mean(jnp.square(o - mu), axis=-1, keepdims=True)
        on = (o - mu) * lax.rsqrt(var + GN_EPS) * gnw_ref[...]
        o_ref[0, sl, :] = (on * _silu(g_ref[0, sl, :])).astype(o_ref.dtype)
        return rb * gc_b + _dot_tn((k_scr[sl, :] * dk_b).astype(BF16), v_scr[sl, :])

    lax.fori_loop(0, nc, bwd, rb0)


def _retention(proj, ctx_kv, cos, sin, decf3, decb3, gn_w, n_heads):
    b, t, _ = proj.shape
    n_ctx = ctx_kv.shape[1]
    hd = HEAD_DIM
    lat = lambda g: pl.BlockSpec((1, t, hd), lambda bi, h: (bi, 0, g * n_heads + h))
    return pl.pallas_call(
        functools.partial(_ret_kernel, n_ctx=n_ctx),
        out_shape=jax.ShapeDtypeStruct((b, t, n_heads * hd), BF16),
        grid=(b, n_heads),
        in_specs=[lat(0), lat(1), lat(2), lat(3),
                  pl.BlockSpec((1, n_ctx, hd), lambda bi, h: (bi, 0, h)),
                  pl.BlockSpec((1, n_ctx, hd), lambda bi, h: (bi, 0, n_heads + h)),
                  pl.BlockSpec(cos.shape, lambda bi, h: (0, 0)),
                  pl.BlockSpec(sin.shape, lambda bi, h: (0, 0)),
                  pl.BlockSpec((1, 1, LANES), lambda bi, h: (h, 0, 0)),
                  pl.BlockSpec((1, 1, LANES), lambda bi, h: (h, 0, 0)),
                  pl.BlockSpec((1, hd), lambda bi, h: (0, h))],
        out_specs=pl.BlockSpec((1, t, hd), lambda bi, h: (bi, 0, h)),
        scratch_shapes=[pltpu.VMEM((t, hd), BF16), pltpu.VMEM((t, hd), F32),
                        pltpu.VMEM((t, hd), BF16), pltpu.VMEM((t, hd), F32)],
        compiler_params=_params(("parallel", "parallel")),
        name="ret",
    )(proj, proj, proj, proj, ctx_kv, ctx_kv, cos, sin, decf3, decb3, gn_w)


CONV_PAD_ROWS = 16
CONV_LANE_CHUNK = 256


def _conv_kernel(a_ref, b_ref, w_ref, cb_ref, lnw_ref, lnb_ref, o_ref, up_scr, sh_scr, y_scr):
    tt, ch = a_ref.shape[1], a_ref.shape[2]
    kw = w_ref.shape[0]
    n_seq = tt // GRID_W
    lead = CONV_PAD_ROWS - kw // 2
    rows = GRID_W + 2 * CONV_PAD_ROWS
    u = a_ref[0] * jax.nn.sigmoid(b_ref[0])
    zeros = jnp.zeros((CONV_PAD_ROWS, ch), F32)
    for s in range(n_seq):
        up_scr[s, 0:CONV_PAD_ROWS, :] = zeros
        up_scr[s, CONV_PAD_ROWS:CONV_PAD_ROWS + GRID_W, :] = u[s * GRID_W:(s + 1) * GRID_W, :]
        up_scr[s, CONV_PAD_ROWS + GRID_W:, :] = zeros

    cw = sh_scr.shape[2]

    def seq(s, carry):
        row0 = pl.multiple_of(s * GRID_W, GRID_W)
        for c0 in range(0, ch, cw):
            for r in range(SUBLANES):
                sh_scr[r] = up_scr[s, r:r + rows - SUBLANES, c0:c0 + cw]
            acc = jnp.broadcast_to(cb_ref[:, c0:c0 + cw], (GRID_W, cw))
            for k in range(kw):
                a8, r = divmod(lead + k, SUBLANES)
                acc = acc + sh_scr[r, a8 * SUBLANES:a8 * SUBLANES + GRID_W, :] * w_ref[k:k + 1, c0:c0 + cw]
            y_scr[pl.ds(row0, GRID_W), c0:c0 + cw] = acc
        return carry

    lax.fori_loop(0, n_seq, seq, 0)
    y = y_scr[...]
    mu = jnp.mean(y, axis=-1, keepdims=True)
    var = jnp.mean(jnp.square(y - mu), axis=-1, keepdims=True)
    yn = (y - mu) * lax.rsqrt(var + EPS) * lnw_ref[...] + lnb_ref[...]
    o_ref[0] = _silu(yn).astype(o_ref.dtype)


def _conv(proj, col0, conv_w, conv_b, ln_w, ln_b):
    b, t, _ = proj.shape
    kw, ch = conv_w.shape
    assert col0 % ch == 0 and kw // 2 <= CONV_PAD_ROWS and t % GRID_W == 0
    tt = _tile(t, 4 * GRID_W)
    cw = min(ch, CONV_LANE_CHUNK)
    rows = GRID_W + 2 * CONV_PAD_ROWS
    ca, cb = col0 // ch, col0 // ch + 1
    vec = pl.BlockSpec((1, ch), lambda bi, i: (0, 0))
    return pl.pallas_call(
        _conv_kernel,
        out_shape=jax.ShapeDtypeStruct((b, t, ch), BF16),
        grid=(b, t // tt),
        in_specs=[pl.BlockSpec((1, tt, ch), lambda bi, i: (bi, i, ca)),
                  pl.BlockSpec((1, tt, ch), lambda bi, i: (bi, i, cb)),
                  pl.BlockSpec((kw, ch), lambda bi, i: (0, 0)),
                  vec, vec, vec],
        out_specs=pl.BlockSpec((1, tt, ch), lambda bi, i: (bi, i, 0)),
        scratch_shapes=[pltpu.VMEM((tt // GRID_W, rows, ch), F32),
                        pltpu.VMEM((SUBLANES, rows - SUBLANES, cw), F32),
                        pltpu.VMEM((tt, ch), F32)],
        compiler_params=_params(("parallel", "parallel")),
        name="conv",
    )(proj, proj, conv_w, conv_b, ln_w, ln_b)


def _outproj_kernel(yr_ref, yc_ref, wr_ref, wc_ref, x_ref, g1_ref, sh2_ref, sc2_ref, pmn_ref, pfn_ref,
                    rwh_ref, rwl_ref, rb_ref, x1_ref, h2p_ref, lg_ref):
    tm, d = x_ref.shape[1], x_ref.shape[2]
    half = d // 2
    seg = half // LANES
    mix = (jnp.dot(yr_ref[0], wr_ref[...], preferred_element_type=F32)
           + jnp.dot(yc_ref[0], wc_ref[...], preferred_element_type=F32))
    x1 = x_ref[0] + g1_ref[0] * _rms(mix, pmn_ref[...])
    x1_ref[0] = x1
    h2 = _rms(x1, pfn_ref[...]) * (1.0 + sc2_ref[0]) + sh2_ref[0]
    h2_hi = h2.astype(BF16)
    h2_lo = (h2 - h2_hi.astype(F32)).astype(BF16)
    lg_ref[...] = (_dot_nt(rwh_ref[...], h2_hi) + _dot_nt(rwh_ref[...], h2_lo)
                   + _dot_nt(rwl_ref[...], h2_hi) + rb_ref[...])
    hf = h2_hi.astype(F32)
    packed = ((lax.bitcast_convert_type(hf[:, :half], U32) >> 16)
              | (lax.bitcast_convert_type(hf[:, half:], U32) & jnp.uint32(HI16)))
    for s in range(seg):
        h2p_ref[pl.ds(s, tm, stride=seg), :] = packed[:, s * LANES:(s + 1) * LANES]


def _out_proj(y_ret, y_conv, w_out_bf, x, mod3, post_mix_norm, pre_ffn_norm, rw_hi, rw_lo, rb_col):
    b, t, d = x.shape
    d_ret, d_conv = y_ret.shape[2], y_conv.shape[2]
    e_rows = rw_hi.shape[0]
    seg = d // 2 // LANES
    tm = _tile(t, 512)
    per_b = t // tm
    mod = lambda col: pl.BlockSpec((1, 1, d), lambda bi, i: (bi, 0, col))
    vec = pl.BlockSpec((1, d), lambda bi, i: (0, 0))
    row = lambda width: pl.BlockSpec((1, tm, width), lambda bi, i: (bi, i, 0))
    rw = pl.BlockSpec((e_rows, d), lambda bi, i: (0, 0))
    w_ret, w_conv = w_out_bf[:d_ret], w_out_bf[d_ret:]
    return pl.pallas_call(
        _outproj_kernel,
        out_shape=(jax.ShapeDtypeStruct((b, t, d), F32),
                   jax.ShapeDtypeStruct((b * t * seg, LANES), U32),
                   jax.ShapeDtypeStruct((e_rows, b * t), F32)),
        grid=(b, per_b),
        in_specs=[row(d_ret), row(d_conv),
                  pl.BlockSpec((d_ret, d), lambda bi, i: (0, 0)),
                  pl.BlockSpec((d_conv, d), lambda bi, i: (0, 0)),
                  row(d), mod(2), mod(3), mod(4), vec, vec, rw, rw,
                  pl.BlockSpec((e_rows, 1), lambda bi, i: (0, 0))],
        out_specs=(row(d),
                   pl.BlockSpec((tm * seg, LANES), lambda bi, i: (bi * per_b + i, 0)),
                   pl.BlockSpec((e_rows, tm), lambda bi, i: (0, bi * per_b + i))),
        compiler_params=_params(("parallel", "parallel")),
        name="out_proj",
    )(y_ret, y_conv, w_ret, w_conv, x, mod3, mod3, mod3, post_mix_norm, pre_ffn_norm, rw_hi, rw_lo, rb_col)


def _route_kernel(lg_ref, idx_ref, gate_ref, rank_ref, cnt_ref, tri_scr, run_scr, *, n_experts):
    e_rows, tr = lg_ref.shape

    @pl.when(pl.program_id(0) == 0)
    def _():
        r = lax.broadcasted_iota(jnp.int32, (tr, tr), 0)
        c = lax.broadcasted_iota(jnp.int32, (tr, tr), 1)
        tri_scr[...] = jnp.where(r <= c, 1.0, 0.0).astype(BF16)
        run_scr[...] = jnp.zeros_like(run_scr)

    e_iota = lax.broadcasted_iota(jnp.int32, (e_rows, tr), 0)
    neg = jnp.float32(-jnp.inf)
    logits = jnp.where(e_iota < n_experts, lg_ref[...], neg)
    vals, hots = [], []
    for k in range(TOP_K):
        m = jnp.max(logits, axis=0, keepdims=True)
        ik = jnp.min(jnp.where(logits == m, e_iota, e_rows), axis=0, keepdims=True)
        hot = e_iota == ik
        logits = jnp.where(hot, neg, logits)
        vals.append(m)
        hots.append(hot)
        idx_ref[k:k + 1, :] = ik
    exps = [jnp.exp(v - vals[0]) for v in vals]
    den = exps[0]
    for e in exps[1:]:
        den = den + e
    for k in range(TOP_K):
        gate_ref[k:k + 1, :] = exps[k] / den

    sel = jnp.zeros((e_rows, tr), F32)
    for hot in hots:
        sel = sel + jnp.where(hot, 1.0, 0.0)
    csum = jnp.dot(sel.astype(BF16), tri_scr[...], preferred_element_type=F32)
    before = run_scr[:, 0:1] + csum - sel
    for k in range(TOP_K):
        rank_ref[k:k + 1, :] = jnp.sum(jnp.where(hots[k], before, 0.0), axis=0, keepdims=True).astype(jnp.int32)
    run_scr[...] = run_scr[...] + jnp.sum(sel, axis=1, keepdims=True)
    cnt_ref[...] = run_scr[...]


def _route(logits_t, n_experts):
    e_rows, n = logits_t.shape
    tr = _tile(n, 512)
    kt = lambda dt: jax.ShapeDtypeStruct((TOP_K, n), dt)
    blk = pl.BlockSpec((TOP_K, tr), lambda i: (0, i))
    return pl.pallas_call(
        functools.partial(_route_kernel, n_experts=n_experts),
        out_shape=(kt(jnp.int32), kt(F32), kt(jnp.int32), jax.ShapeDtypeStruct((e_rows, LANES), F32)),
        grid=(n // tr,),
        in_specs=[pl.BlockSpec((e_rows, tr), lambda i: (0, i))],
        out_specs=(blk, blk, blk, pl.BlockSpec((e_rows, LANES), lambda i: (0, 0))),
        scratch_shapes=[pltpu.VMEM((tr, tr), BF16), pltpu.VMEM((e_rows, LANES), F32)],
        compiler_params=_params(("arbitrary",)),
        name="route",
    )(logits_t)


def _dispatch_kernel(pos_ref, poff_ref, plen_ref, h_ref, xs_ref, zero_scr, sem, zsem, *, n_tok, n_exp, seg, tm):
    td = h_ref.shape[0] // seg
    step = pl.program_id(0)
    base = step * td

    def pad_copies(e, act):
        off, ln = poff_ref[e], plen_ref[e]
        for bit in range(tm.bit_length() - 1):
            size = 1 << bit

            @pl.when(((ln >> bit) & 1) == 1)
            def _():
                row = pl.multiple_of((off + (ln & (size - 1))) * seg, seg)
                act(pltpu.make_async_copy(zero_scr.at[pl.ds(0, size * seg)],
                                          xs_ref.at[pl.ds(row, size * seg)], zsem))

    def for_pads(act):
        def body(e, carry):
            pad_copies(e, act)
            return carry
        lax.fori_loop(0, n_exp, body, 0)

        half = zero_scr.shape[0]
        end = (poff_ref[n_exp - 1] + plen_ref[n_exp - 1]) * seg

        def tail(n, carry):
            row = pl.multiple_of(end + n * half, seg)
            act(pltpu.make_async_copy(zero_scr, xs_ref.at[pl.ds(row, half)], zsem))
            return carry
        lax.fori_loop(0, (xs_ref.shape[0] - end) // half, tail, 0)

    @pl.when(step == 0)
    def _():
        zero_scr[...] = jnp.zeros_like(zero_scr)
        for_pads(lambda cp: cp.start())

    def row_copy(t, k):
        p = pos_ref[k * n_tok + base + t]
        return pltpu.make_async_copy(h_ref.at[pl.ds(pl.multiple_of(t * seg, seg), seg)],
                                     xs_ref.at[pl.ds(pl.multiple_of(p * seg, seg), seg)], sem)

    def start(t, carry):
        for k in range(TOP_K):
            row_copy(t, k).start()
        return carry

    lax.fori_loop(0, td, start, 0)
    for k in range(TOP_K):
        pltpu.make_async_copy(h_ref, h_ref, sem).wait()

    @pl.when(step == 0)
    def _():
        for_pads(lambda cp: cp.wait())


def _dispatch(pos_flat, pad_off, pad_len, h2p, n_slots, seg, tm):
    n = h2p.shape[0] // seg
    n_exp = pad_off.shape[0]
    assert tm & (tm - 1) == 0
    td = _tile(n, 256)
    return pl.pallas_call(
        functools.partial(_dispatch_kernel, n_tok=n, n_exp=n_exp, seg=seg, tm=tm),
        out_shape=jax.ShapeDtypeStruct((n_slots * seg, LANES), U32),
        grid_spec=pltpu.PrefetchScalarGridSpec(
            num_scalar_prefetch=3,
            grid=(n // td,),
            in_specs=[pl.BlockSpec((td * seg, LANES), lambda i, *_: (i, 0))],
            out_specs=pl.BlockSpec(memory_space=pl.ANY),
            scratch_shapes=[pltpu.VMEM((max(tm // 2, 1) * seg, LANES), U32),
                            pltpu.SemaphoreType.DMA(()), pltpu.SemaphoreType.DMA(())]),
        compiler_params=_params(("arbitrary",)),
        name="dispatch",
    )(pos_flat, pad_off, pad_len, h2p)


def _expert_changed(te_ref, i):
    return jnp.logical_or(i == 0, te_ref[i] != te_ref[jnp.maximum(i - 1, 0)])


def _ffn1_kernel(te_ref, nu_ref, xs_ref, wg_ref, wl_ref, bg_ref, bl_ref, act_ref, wg_scr, wl_scr, x_scr, *, seg):
    i = pl.program_id(1)
    tm = x_scr.shape[0]
    half = seg * LANES

    @pl.when(_expert_changed(te_ref, i))
    def _():
        wg_scr[...] = wg_ref[0].astype(BF16)
        wl_scr[...] = wl_ref[0].astype(BF16)

    @pl.when(i < nu_ref[0])
    def _():
        for s in range(seg):
            w = xs_ref[pl.ds(s, tm, stride=seg), :]
            x_scr[:, s * LANES:(s + 1) * LANES] = lax.bitcast_convert_type(w << 16, F32).astype(BF16)
            x_scr[:, half + s * LANES:half + (s + 1) * LANES] = (
                lax.bitcast_convert_type(w & jnp.uint32(HI16), F32).astype(BF16))
        x = x_scr[...]
        glu = jnp.minimum(jnp.dot(x, wg_scr[...], preferred_element_type=F32) + bg_ref[0], SWIGLU_LIMIT)
        lin = jnp.clip(jnp.dot(x, wl_scr[...], preferred_element_type=F32) + bl_ref[0],
                       -SWIGLU_LIMIT, SWIGLU_LIMIT)
        act_ref[...] = (glu * jax.nn.sigmoid(SWIGLU_ALPHA * glu) * (lin + 1.0)).astype(act_ref.dtype)

    @pl.when(i >= nu_ref[0])
    def _():
        act_ref[...] = jnp.zeros_like(act_ref)


def _ffn1(tile_expert, n_used, xs, w1, b1_3, tm, seg):
    d = seg * LANES * 2
    slots = xs.shape[0] // seg
    n_exp, _, two_ff = w1.shape
    d_ff = two_ff // 2
    tn = _tile(d_ff, 1024)
    nj = d_ff // tn
    n_tiles = slots // tm
    row = lambda i, nu: jnp.minimum(i, nu[0] - 1)
    return pl.pallas_call(
        functools.partial(_ffn1_kernel, seg=seg),
        out_shape=jax.ShapeDtypeStruct((slots, d_ff), BF16),
        grid_spec=pltpu.PrefetchScalarGridSpec(
            num_scalar_prefetch=2,
            grid=(nj, n_tiles),
            in_specs=[pl.BlockSpec((tm * seg, LANES), lambda j, i, te, nu: (row(i, nu), 0)),
                      pl.BlockSpec((1, d, tn), lambda j, i, te, nu: (te[i], 0, j)),
                      pl.BlockSpec((1, d, tn), lambda j, i, te, nu: (te[i], 0, nj + j)),
                      pl.BlockSpec((1, 1, tn), lambda j, i, te, nu: (te[i], 0, j)),
                      pl.BlockSpec((1, 1, tn), lambda j, i, te, nu: (te[i], 0, nj + j))],
            out_specs=pl.BlockSpec((tm, tn), lambda j, i, te, nu: (i, j)),
            scratch_shapes=[pltpu.VMEM((d, tn), BF16), pltpu.VMEM((d, tn), BF16), pltpu.VMEM((tm, d), BF16)]),
        compiler_params=_params(("arbitrary", "arbitrary"), 60000 * 1024),
        name="ffn1",
    )(tile_expert, n_used, xs, w1, w1, b1_3, b1_3)


def _ffn2_kernel(te_ref, nu_ref, act_ref, w_ref, b_ref, ys_ref, w_scr, *, nseg):
    i = pl.program_id(1)
    tm = act_ref.shape[0]

    @pl.when(_expert_changed(te_ref, i))
    def _():
        w_scr[...] = w_ref[0].astype(BF16)

    @pl.when(i < nu_ref[0])
    def _():
        y = jnp.dot(act_ref[...], w_scr[...], preferred_element_type=F32) + b_ref[0]
        for s in range(nseg):
            ys_ref[0, pl.ds(s, tm, stride=nseg), :] = y[:, s * LANES:(s + 1) * LANES]

    @pl.when(i >= nu_ref[0])
    def _():
        ys_ref[...] = jnp.zeros_like(ys_ref)


def _ffn2(tile_expert, n_used, act, w2, b2_3, tm):
    slots, d_ff = act.shape
    d = w2.shape[2]
    tn = _tile(d, SUBLANES * LANES)
    nseg = tn // LANES
    n_tiles = slots // tm
    row = lambda i, nu: jnp.minimum(i, nu[0] - 1)
    return pl.pallas_call(
        functools.partial(_ffn2_kernel, nseg=nseg),
        out_shape=jax.ShapeDtypeStruct((d // tn, slots * nseg, LANES), F32),
        grid_spec=pltpu.PrefetchScalarGridSpec(
            num_scalar_prefetch=2,
            grid=(d // tn, n_tiles),
            in_specs=[pl.BlockSpec((tm, d_ff), lambda j, i, te, nu: (row(i, nu), 0)),
                      pl.BlockSpec((1, d_ff, tn), lambda j, i, te, nu: (te[i], 0, j)),
                      pl.BlockSpec((1, 1, tn), lambda j, i, te, nu: (te[i], 0, j))],
            out_specs=pl.BlockSpec((1, tm * nseg, LANES), lambda j, i, te, nu: (j, i, 0)),
            scratch_shapes=[pltpu.VMEM((d_ff, tn), BF16)]),
        compiler_params=_params(("arbitrary", "arbitrary")),
        name="ffn2",
    )(tile_expert, n_used, act, w2, b2_3)


def _combine_kernel(pos_ref, ys_ref, gate_ref, x1_ref, g2_ref, pfn_ref, o_ref, buf, moe_scr, sem, *, n_tok):
    tc = x1_ref.shape[0]
    planes = ys_ref.shape[0]
    seg = buf.shape[2] // tc
    base = pl.program_id(0) * tc

    def row_copy(t, k):
        p = pos_ref[k * n_tok + base + t]
        return pltpu.make_async_copy(ys_ref.at[:, pl.ds(pl.multiple_of(p * seg, seg), seg)],
                                     buf.at[k, :, pl.ds(pl.multiple_of(t * seg, seg), seg)], sem.at[k])

    def start(t, carry):
        for k in range(TOP_K):
            row_copy(t, k).start()
        return carry

    lax.fori_loop(0, tc, start, 0)
    for k in range(TOP_K):
        pltpu.make_async_copy(buf.at[k], buf.at[k], sem.at[k]).wait()
    gates = [jnp.broadcast_to(gate_ref[:, k:k + 1], (tc, LANES)) for k in range(TOP_K)]
    for pl_i in range(planes):
        for s in range(seg):
            m = buf[0, pl_i, pl.ds(s, tc, stride=seg), :] * gates[0]
            for k in range(1, TOP_K):
                m = m + buf[k, pl_i, pl.ds(s, tc, stride=seg), :] * gates[k]
            col = (pl_i * seg + s) * LANES
            moe_scr[:, col:col + LANES] = m
    o_ref[...] = x1_ref[...] + g2_ref[0] * _rms(moe_scr[...], pfn_ref[...])


def _combine(pos_flat, ys, gates_t, x1, mod3, post_ffn_norm, seq):
    n, d = x1.shape
    planes = ys.shape[0]
    seg = d // planes // LANES
    tc = _tile(seq, 128)
    per_batch = seq // tc
    return pl.pallas_call(
        functools.partial(_combine_kernel, n_tok=n),
        out_shape=jax.ShapeDtypeStruct((n, d), F32),
        grid_spec=pltpu.PrefetchScalarGridSpec(
            num_scalar_prefetch=1,
            grid=(n // tc,),
            in_specs=[pl.BlockSpec(memory_space=pl.ANY),
                      pl.BlockSpec((tc, TOP_K), lambda i, pos: (i, 0)),
                      pl.BlockSpec((tc, d), lambda i, pos: (i, 0)),
                      pl.BlockSpec((1, 1, d), lambda i, pos: (i // per_batch, 0, 5)),
                      pl.BlockSpec((1, d), lambda i, pos: (0, 0))],
            out_specs=pl.BlockSpec((tc, d), lambda i, pos: (i, 0)),
            scratch_shapes=[pltpu.VMEM((TOP_K, planes, tc * seg, LANES), F32), pltpu.VMEM((tc, d), F32),
                            pltpu.SemaphoreType.DMA((TOP_K,))]),
        compiler_params=_params(("arbitrary",)),
        name="combine",
    )(pos_flat, ys, gates_t, x1, mod3, post_ffn_norm)


def _moe(h2p, logits_t, x1, mod3, post_ffn_norm, w1, b1, w2, b2, seq, n_exp):
    n, d = x1.shape
    seg = d // 2 // LANES
    tm = 1 << (min(n * TOP_K, 512).bit_length() - 1)
    idx, gates, rank, cnt = _route(logits_t, n_exp)

    counts = cnt[:n_exp, 0].astype(jnp.int32)
    padded = (counts + tm - 1) // tm * tm
    ends = jnp.cumsum(padded)
    starts = ends - padded
    hot = idx[:, :, None] == jnp.arange(n_exp, dtype=jnp.int32)
    pos_flat = (jnp.sum(jnp.where(hot, starts, 0), axis=-1) + rank).reshape(-1)
    n_tiles = -(-n * TOP_K // tm) + n_exp
    n_used = (ends[-1] // tm).astype(jnp.int32)
    tile_start = jnp.minimum(jnp.arange(n_tiles, dtype=jnp.int32), n_used - 1) * tm
    tile_expert = jnp.sum(tile_start[:, None] >= ends[None, :], axis=1).astype(jnp.int32)
    n_used = n_used.reshape(1)

    xs = _dispatch(pos_flat, starts + counts, padded - counts, h2p, n_tiles * tm, seg, tm)
    act = _ffn1(tile_expert, n_used, xs, w1, b1[:, None, :], tm, seg)
    ys = _ffn2(tile_expert, n_used, act, w2, b2[:, None, :], tm)
    return _combine(pos_flat, ys, gates.T, x1, mod3, post_ffn_norm, seq)


def kernel(x, c, ctx, c_ctx, ada_w, ada_b, pre_mix_norm, post_mix_norm, pre_ffn_norm, post_ffn_norm,
           w_in, ret_decay_fwd, ret_decay_bwd, ret_gn_w, conv_w, conv_b, conv_ln_w, conv_ln_b, w_out,
           router_w, router_b, w1, b1, w2, b2):
    assert ada_w.shape[0] == 1, "single-layer stack only"
    b, t, d = x.shape
    n_ctx = ctx.shape[1]
    n_heads = ret_decay_fwd.shape[1]
    d_ret = n_heads * HEAD_DIM
    n_exp = router_w.shape[2]
    assert b < MOD_ROWS and ret_gn_w.shape[1] == d_ret and d % (2 * LANES) == 0

    cc = jnp.zeros((MOD_ROWS, d), F32).at[:b].set(c).at[b].set(c_ctx)
    mod3 = _ada(cc, ada_w[0], ada_b)[:, None, :]
    cos, sin = _rope_tables(n_ctx + t)

    w_in_bf = w_in[0].astype(BF16)
    proj = _in_proj(x, pre_mix_norm, mod3, lambda bi: bi, w_in_bf, 0, w_in.shape[2])
    ctx_kv = _in_proj(ctx, pre_mix_norm, mod3, lambda bi: b, w_in_bf, d_ret, 2 * d_ret)

    lane_bcast = lambda v: jnp.broadcast_to(v.reshape(n_heads, 1, 1), (n_heads, 1, LANES))
    y_ret = _retention(proj, ctx_kv, cos, sin, lane_bcast(ret_decay_fwd[0]), lane_bcast(ret_decay_bwd[0]),
                       ret_gn_w, n_heads)
    y_conv = _conv(proj, 4 * d_ret, conv_w[0], conv_b, conv_ln_w, conv_ln_b)

    e_rows = -(-n_exp // SUBLANES) * SUBLANES
    rw_t = jnp.zeros((e_rows, d), F32).at[:n_exp].set(router_w[0].T)
    rw_hi = rw_t.astype(BF16)
    rw_lo = (rw_t - rw_hi.astype(F32)).astype(BF16)
    rb_col = jnp.zeros((e_rows, 1), F32).at[:n_exp, 0].set(router_b[0])
    x1, h2p, logits_t = _out_proj(y_ret, y_conv, w_out[0].astype(BF16), x, mod3, post_mix_norm, pre_ffn_norm,
                                  rw_hi, rw_lo, rb_col)

    out = _moe(h2p, logits_t, x1.reshape(b * t, d), mod3, post_ffn_norm, w1[0], b1[0], w2[0], b2[0], t, n_exp)
    return out.reshape(b, t, d)
```

```python
import functools
import math

import jax
import jax.numpy as jnp
from jax import lax
from jax.experimental import pallas as pl
from jax.experimental.pallas import tpu as pltpu

F32 = jnp.float32
BF16 = jnp.bfloat16
U32 = jnp.uint32
HIGHEST = lax.Precision.HIGHEST

GRID_W = 64
HEAD_DIM = 128
RET_CHUNK = 256
ROPE_BASE = 10000.0
TOP_K = 4
SWIGLU_ALPHA = 1.702
SWIGLU_LIMIT = 7.0
EPS = 1e-6
GN_EPS = 1e-5
LANES = 128
SUBLANES = 8
MOD_ROWS = 16
VMEM_LIMIT = 56 * 1024 * 1024
HI16 = 0xFFFF0000


def _tile(n, pref):
    t = min(n, pref)
    while n % t:
        t -= 1
    return t


def _params(sem, vmem=VMEM_LIMIT):
    return pltpu.CompilerParams(dimension_semantics=sem, vmem_limit_bytes=vmem)


def _rms(x, w):
    return x * lax.rsqrt(jnp.mean(x * x, axis=-1, keepdims=True) + EPS) * w


def _silu(x):
    return x * jax.nn.sigmoid(x)


def _dot_nt(a, b):
    return lax.dot_general(a, b, (((1,), (1,)), ((), ())), preferred_element_type=F32)


def _dot_tn(a, b):
    return lax.dot_general(a, b, (((0,), (0,)), ((), ())), preferred_element_type=F32)


def _ada_kernel(c_ref, w_ref, b_ref, o_ref):
    s = _silu(c_ref[...])
    o_ref[...] = jnp.dot(s, w_ref[...], preferred_element_type=F32, precision=HIGHEST) + b_ref[...]


def _ada(cc, w, b):
    d, n = w.shape
    tn = _tile(n, 1024)
    return pl.pallas_call(
        _ada_kernel,
        out_shape=jax.ShapeDtypeStruct((MOD_ROWS, n), F32),
        grid=(n // tn,),
        in_specs=[pl.BlockSpec((MOD_ROWS, d), lambda j: (0, 0)),
                  pl.BlockSpec((d, tn), lambda j: (0, j)),
                  pl.BlockSpec((1, tn), lambda j: (0, j))],
        out_specs=pl.BlockSpec((MOD_ROWS, tn), lambda j: (0, j)),
        compiler_params=_params(("parallel",)),
        name="ada",
    )(cc, w, b)


def _rope_kernel(cos_ref, sin_ref):
    p, _ = cos_ref.shape
    half = HEAD_DIM // 2
    lane = lax.broadcasted_iota(jnp.int32, (p, HEAD_DIM), 1)
    pos = lax.broadcasted_iota(jnp.int32, (p, HEAD_DIM), 0).astype(F32)
    j = jnp.where(lane < half, lane, lane - half).astype(F32)
    inv = jnp.exp(j * (-jnp.log(ROPE_BASE) / half))
    ang = pos * inv
    cos_ref[...] = jnp.cos(ang)
    sin_ref[...] = jnp.where(lane < half, -1.0, 1.0) * jnp.sin(ang)


def _rope_tables(p):
    return pl.pallas_call(
        _rope_kernel,
        out_shape=(jax.ShapeDtypeStruct((p, HEAD_DIM), F32), jax.ShapeDtypeStruct((p, HEAD_DIM), F32)),
        name="rope",
    )()


def _rot(t, cos, sin_signed):
    return t * cos + pltpu.roll(t, HEAD_DIM // 2, axis=1) * sin_signed


def _inproj_kernel(x_ref, nw_ref, sh_ref, sc_ref, w_ref, o_ref, h_scr):
    @pl.when(pl.program_id(2) == 0)
    def _():
        h = _rms(x_ref[0], nw_ref[...]) * (1.0 + sc_ref[0]) + sh_ref[0]
        h_scr[...] = h.astype(BF16)

    o_ref[0] = jnp.dot(h_scr[...], w_ref[...], preferred_element_type=F32)


def _in_proj(x, norm_w, mod3, mod_row, w_bf, col0, ncols):
    b, t, d = x.shape
    tm = _tile(t, 1024)
    tn = _tile(math.gcd(ncols, col0), 1536)
    joff = col0 // tn
    return pl.pallas_call(
        _inproj_kernel,
        out_shape=jax.ShapeDtypeStruct((b, t, ncols), F32),
        grid=(b, t // tm, ncols // tn),
        in_specs=[pl.BlockSpec((1, tm, d), lambda bi, i, j: (bi, i, 0)),
                  pl.BlockSpec((1, d), lambda bi, i, j: (0, 0)),
                  pl.BlockSpec((1, 1, d), lambda bi, i, j: (mod_row(bi), 0, 0)),
                  pl.BlockSpec((1, 1, d), lambda bi, i, j: (mod_row(bi), 0, 1)),
                  pl.BlockSpec((d, tn), lambda bi, i, j: (0, j + joff))],
        out_specs=pl.BlockSpec((1, tm, tn), lambda bi, i, j: (bi, i, j)),
        scratch_shapes=[pltpu.VMEM((tm, d), BF16)],
        compiler_params=_params(("parallel", "parallel", "arbitrary")),
        name="in_proj",
    )(x, norm_w, mod3, mod3, w_bf)


def _log_sigmoid(x):
    return jnp.minimum(x, 0.0) - jnp.log(1.0 + jnp.exp(-jnp.abs(x)))


def _ret_kernel(q_ref, k_ref, v_ref, g_ref, kc_ref, vc_ref, cos_ref, sin_ref, decf_ref, decb_ref,
                gnw_ref, o_ref, q_scr, k_scr, kf_scr, kb_scr, v_scr, o_scr, din_scr, kvf_scr, kvb_scr,
                rf_scr, rb_scr, *, n_ctx):
    t = q_ref.shape[1]
    c = din_scr.shape[0]
    nc = t // c
    lgf = _log_sigmoid(decf_ref[0])[:, 0:1]
    lgb = _log_sigmoid(decb_ref[0])[:, 0:1]

    kc = _rot(kc_ref[0], cos_ref[0:n_ctx, :], sin_ref[0:n_ctx, :])
    vc = vc_ref[0].astype(BF16)
    tc = lax.broadcasted_iota(jnp.int32, (n_ctx, 1), 0).astype(F32)
    rf = _dot_tn((kc * jnp.exp(lgf * (n_ctx - 1.0 - tc))).astype(BF16), vc)
    rb = _dot_tn((kc * jnp.exp(lgb * tc)).astype(BF16), vc)

    scale = HEAD_DIM ** -0.5
    cos, sin = cos_ref[n_ctx:n_ctx + t, :], sin_ref[n_ctx:n_ctx + t, :]
    q_scr[...] = (_rot(q_ref[0], cos, sin) * scale).astype(BF16)
    kr = _rot(k_ref[0], cos, sin)
    ri_all = (lax.broadcasted_iota(jnp.int32, (t, 1), 0) % c).astype(F32)
    k_scr[...] = kr.astype(BF16)
    kf_scr[...] = (kr * jnp.exp(lgf * (c - 1.0 - ri_all))).astype(BF16)
    kb_scr[...] = (kr * jnp.exp(lgb * ri_all)).astype(BF16)
    v_scr[...] = v_ref[0].astype(BF16)

    diff = (lax.broadcasted_iota(jnp.int32, (c, c), 0) - lax.broadcasted_iota(jnp.int32, (c, c), 1)).astype(F32)
    din_scr[...] = jnp.where(diff > 0, jnp.exp(lgf * jnp.maximum(diff, 0.0)),
                             jnp.where(diff < 0, jnp.exp(lgb * jnp.maximum(-diff, 0.0)), 2.0))

    for i in range(nc):
        sl = slice(i * c, (i + 1) * c)
        vi = v_scr[sl, :]
        scores = _dot_nt(q_scr[sl, :], k_scr[sl, :]) * din_scr[...]
        o_scr[sl, :] = jnp.dot(scores.astype(BF16), vi, preferred_element_type=F32)
        kvf_scr[i] = _dot_tn(kf_scr[sl, :], vi)
        kvb_scr[i] = _dot_tn(kb_scr[sl, :], vi)

    gc_f = jnp.exp(lgf * float(c))
    gc_b = jnp.exp(lgb * float(c))
    for i in range(nc):
        rf_scr[i] = rf.astype(BF16)
        rf = rf * gc_f + kvf_scr[i]
    for i in reversed(range(nc)):
        rb_scr[i] = rb.astype(BF16)
        rb = rb * gc_b + kvb_scr[i]

    ri = lax.broadcasted_iota(jnp.int32, (c, 1), 0).astype(F32)
    dq_f = jnp.exp(lgf * (ri + 1.0))
    dq_b = jnp.exp(lgb * (c - ri))
    for i in range(nc):
        sl = slice(i * c, (i + 1) * c)
        qi = q_scr[sl, :]
        o = (o_scr[sl, :] + jnp.dot(qi, rf_scr[i], preferred_element_type=F32) * dq_f
             + jnp.dot(qi, rb_scr[i], preferred_element_type=F32) * dq_b)
        mu = jnp.mean(o, axis=-1, keepdims=True)
        var = jnp.mean(jnp.square(o - mu), axis=-1, keepdims=True)
        on = (o - mu) * lax.rsqrt(var + GN_EPS) * gnw_ref[...]
        o_ref[0, sl, :] = (on * _silu(g_ref[0, sl, :])).astype(o_ref.dtype)


def _retention(proj, ctx_kv, cos, sin, decf3, decb3, gn_w, n_heads):
    b, t, _ = proj.shape
    n_ctx = ctx_kv.shape[1]
    hd = HEAD_DIM
    c = _tile(t, RET_CHUNK)
    nc = t // c
    lat = lambda g: pl.BlockSpec((1, t, hd), lambda bi, h: (bi, 0, g * n_heads + h))
    seq = lambda dt: pltpu.VMEM((t, hd), dt)
    return pl.pallas_call(
        functools.partial(_ret_kernel, n_ctx=n_ctx),
        out_shape=jax.ShapeDtypeStruct((b, t, n_heads * hd), BF16),
        grid=(b, n_heads),
        in_specs=[lat(0), lat(1), lat(2), lat(3),
                  pl.BlockSpec((1, n_ctx, hd), lambda bi, h: (bi, 0, h)),
                  pl.BlockSpec((1, n_ctx, hd), lambda bi, h: (bi, 0, n_heads + h)),
                  pl.BlockSpec(cos.shape, lambda bi, h: (0, 0)),
                  pl.BlockSpec(sin.shape, lambda bi, h: (0, 0)),
                  pl.BlockSpec((1, 1, LANES), lambda bi, h: (h, 0, 0)),
                  pl.BlockSpec((1, 1, LANES), lambda bi, h: (h, 0, 0)),
                  pl.BlockSpec((1, hd), lambda bi, h: (0, h))],
        out_specs=pl.BlockSpec((1, t, hd), lambda bi, h: (bi, 0, h)),
        scratch_shapes=[seq(BF16), seq(BF16), seq(BF16), seq(BF16), seq(BF16), seq(F32),
                        pltpu.VMEM((c, c), F32),
                        pltpu.VMEM((nc, hd, hd), F32), pltpu.VMEM((nc, hd, hd), F32),
                        pltpu.VMEM((nc, hd, hd), BF16), pltpu.VMEM((nc, hd, hd), BF16)],
        compiler_params=_params(("parallel", "parallel")),
        name="ret",
    )(proj, proj, proj, proj, ctx_kv, ctx_kv, cos, sin, decf3, decb3, gn_w)


CONV_PAD_ROWS = 16
CONV_LANE_CHUNK = 256


def _conv_kernel(a_ref, b_ref, w_ref, cb_ref, lnw_ref, lnb_ref, o_ref, up_scr, sh_scr, y_scr):
    tt, ch = a_ref.shape[1], a_ref.shape[2]
    kw = w_ref.shape[0]
    n_seq = tt // GRID_W
    lead = CONV_PAD_ROWS - kw // 2
    rows = GRID_W + 2 * CONV_PAD_ROWS
    u = a_ref[0] * jax.nn.sigmoid(b_ref[0])
    zeros = jnp.zeros((CONV_PAD_ROWS, ch), F32)
    for s in range(n_seq):
        up_scr[s, 0:CONV_PAD_ROWS, :] = zeros
        up_scr[s, CONV_PAD_ROWS:CONV_PAD_ROWS + GRID_W, :] = u[s * GRID_W:(s + 1) * GRID_W, :]
        up_scr[s, CONV_PAD_ROWS + GRID_W:, :] = zeros

    cw = sh_scr.shape[2]

    def seq(s, carry):
        row0 = pl.multiple_of(s * GRID_W, GRID_W)
        for c0 in range(0, ch, cw):
            for r in range(SUBLANES):
                sh_scr[r] = up_scr[s, r:r + rows - SUBLANES, c0:c0 + cw]
            acc = jnp.broadcast_to(cb_ref[:, c0:c0 + cw], (GRID_W, cw))
            for k in range(kw):
                a8, r = divmod(lead + k, SUBLANES)
                acc = acc + sh_scr[r, a8 * SUBLANES:a8 * SUBLANES + GRID_W, :] * w_ref[k:k + 1, c0:c0 + cw]
            y_scr[pl.ds(row0, GRID_W), c0:c0 + cw] = acc
        return carry

    lax.fori_loop(0, n_seq, seq, 0)
    y = y_scr[...]
    mu = jnp.mean(y, axis=-1, keepdims=True)
    var = jnp.mean(jnp.square(y - mu), axis=-1, keepdims=True)
    yn = (y - mu) * lax.rsqrt(var + EPS) * lnw_ref[...] + lnb_ref[...]
    o_ref[0] = _silu(yn).astype(o_ref.dtype)


def _conv(proj, col0, conv_w, conv_b, ln_w, ln_b):
    b, t, _ = proj.shape
    kw, ch = conv_w.shape
    assert col0 % ch == 0 and kw // 2 <= CONV_PAD_ROWS and t % GRID_W == 0
    tt = _tile(t, 4 * GRID_W)
    cw = min(ch, CONV_LANE_CHUNK)
    rows = GRID_W + 2 * CONV_PAD_ROWS
    ca, cb = col0 // ch, col0 // ch + 1
    vec = pl.BlockSpec((1, ch), lambda bi, i: (0, 0))
    return pl.pallas_call(
        _conv_kernel,
        out_shape=jax.ShapeDtypeStruct((b, t, ch), BF16),
        grid=(b, t // tt),
        in_specs=[pl.BlockSpec((1, tt, ch), lambda bi, i: (bi, i, ca)),
                  pl.BlockSpec((1, tt, ch), lambda bi, i: (bi, i, cb)),
                  pl.BlockSpec((kw, ch), lambda bi, i: (0, 0)),
                  vec, vec, vec],
        out_specs=pl.BlockSpec((1, tt, ch), lambda bi, i: (bi, i, 0)),
        scratch_shapes=[pltpu.VMEM((tt // GRID_W, rows, ch), F32),
                        pltpu.VMEM((SUBLANES, rows - SUBLANES, cw), F32),
                        pltpu.VMEM((tt, ch), F32)],
        compiler_params=_params(("parallel", "parallel")),
        name="conv",
    )(proj, proj, conv_w, conv_b, ln_w, ln_b)


def _outproj_kernel(yr_ref, yc_ref, wr_ref, wc_ref, x_ref, g1_ref, sh2_ref, sc2_ref, pmn_ref, pfn_ref,
                    rwh_ref, rwl_ref, rb_ref, x1_ref, h2p_ref, lg_ref):
    tm, d = x_ref.shape[1], x_ref.shape[2]
    half = d // 2
    seg = half // LANES
    mix = (jnp.dot(yr_ref[0], wr_ref[...], preferred_element_type=F32)
           + jnp.dot(yc_ref[0], wc_ref[...], preferred_element_type=F32))
    x1 = x_ref[0] + g1_ref[0] * _rms(mix, pmn_ref[...])
    x1_ref[0] = x1
    h2 = _rms(x1, pfn_ref[...]) * (1.0 + sc2_ref[0]) + sh2_ref[0]
    h2_hi = h2.astype(BF16)
    h2_lo = (h2 - h2_hi.astype(F32)).astype(BF16)
    lg_ref[...] = (_dot_nt(rwh_ref[...], h2_hi) + _dot_nt(rwh_ref[...], h2_lo)
                   + _dot_nt(rwl_ref[...], h2_hi) + rb_ref[...])
    hf = h2_hi.astype(F32)
    packed = ((lax.bitcast_convert_type(hf[:, :half], U32) >> 16)
              | (lax.bitcast_convert_type(hf[:, half:], U32) & jnp.uint32(HI16)))
    for s in range(seg):
        h2p_ref[pl.ds(s, tm, stride=seg), :] = packed[:, s * LANES:(s + 1) * LANES]


def _out_proj(y_ret, y_conv, w_out_bf, x, mod3, post_mix_norm, pre_ffn_norm, rw_hi, rw_lo, rb_col):
    b, t, d = x.shape
    d_ret, d_conv = y_ret.shape[2], y_conv.shape[2]
    e_rows = rw_hi.shape[0]
    seg = d // 2 // LANES
    tm = _tile(t, 512)
    per_b = t // tm
    mod = lambda col: pl.BlockSpec((1, 1, d), lambda bi, i: (bi, 0, col))
    vec = pl.BlockSpec((1, d), lambda bi, i: (0, 0))
    row = lambda width: pl.BlockSpec((1, tm, width), lambda bi, i: (bi, i, 0))
    rw = pl.BlockSpec((e_rows, d), lambda bi, i: (0, 0))
    w_ret, w_conv = w_out_bf[:d_ret], w_out_bf[d_ret:]
    return pl.pallas_call(
        _outproj_kernel,
        out_shape=(jax.ShapeDtypeStruct((b, t, d), F32),
                   jax.ShapeDtypeStruct((b * t * seg, LANES), U32),
                   jax.ShapeDtypeStruct((e_rows, b * t), F32)),
        grid=(b, per_b),
        in_specs=[row(d_ret), row(d_conv),
                  pl.BlockSpec((d_ret, d), lambda bi, i: (0, 0)),
                  pl.BlockSpec((d_conv, d), lambda bi, i: (0, 0)),
                  row(d), mod(2), mod(3), mod(4), vec, vec, rw, rw,
                  pl.BlockSpec((e_rows, 1), lambda bi, i: (0, 0))],
        out_specs=(row(d),
                   pl.BlockSpec((tm * seg, LANES), lambda bi, i: (bi * per_b + i, 0)),
                   pl.BlockSpec((e_rows, tm), lambda bi, i: (0, bi * per_b + i))),
        compiler_params=_params(("parallel", "parallel")),
        name="out_proj",
    )(y_ret, y_conv, w_ret, w_conv, x, mod3, mod3, mod3, post_mix_norm, pre_ffn_norm, rw_hi, rw_lo, rb_col)


def _route_kernel(lg_ref, idx_ref, gate_ref, rank_ref, cnt_ref, tri_scr, run_scr, *, n_experts):
    e_rows, tr = lg_ref.shape

    @pl.when(pl.program_id(0) == 0)
    def _():
        r = lax.broadcasted_iota(jnp.int32, (tr, tr), 0)
        c = lax.broadcasted_iota(jnp.int32, (tr, tr), 1)
        tri_scr[...] = jnp.where(r <= c, 1.0, 0.0).astype(BF16)
        run_scr[...] = jnp.zeros_like(run_scr)

    e_iota = lax.broadcasted_iota(jnp.int32, (e_rows, tr), 0)
    neg = jnp.float32(-jnp.inf)
    logits = jnp.where(e_iota < n_experts, lg_ref[...], neg)
    vals, hots = [], []
    for k in range(TOP_K):
        m = jnp.max(logits, axis=0, keepdims=True)
        ik = jnp.min(jnp.where(logits == m, e_iota, e_rows), axis=0, keepdims=True)
        hot = e_iota == ik
        logits = jnp.where(hot, neg, logits)
        vals.append(m)
        hots.append(hot)
        idx_ref[k:k + 1, :] = ik
    exps = [jnp.exp(v - vals[0]) for v in vals]
    den = exps[0]
    for e in exps[1:]:
        den = den + e
    for k in range(TOP_K):
        gate_ref[k:k + 1, :] = exps[k] / den

    sel = jnp.zeros((e_rows, tr), F32)
    for hot in hots:
        sel = sel + jnp.where(hot, 1.0, 0.0)
    csum = jnp.dot(sel.astype(BF16), tri_scr[...], preferred_element_type=F32)
    before = run_scr[:, 0:1] + csum - sel
    for k in range(TOP_K):
        rank_ref[k:k + 1, :] = jnp.sum(jnp.where(hots[k], before, 0.0), axis=0, keepdims=True).astype(jnp.int32)
    run_scr[...] = run_scr[...] + jnp.sum(sel, axis=1, keepdims=True)
    cnt_ref[...] = run_scr[...]


def _route(logits_t, n_experts):
    e_rows, n = logits_t.shape
    tr = _tile(n, 512)
    kt = lambda dt: jax.ShapeDtypeStruct((TOP_K, n), dt)
    blk = pl.BlockSpec((TOP_K, tr), lambda i: (0, i))
    return pl.pallas_call(
        functools.partial(_route_kernel, n_experts=n_experts),
        out_shape=(kt(jnp.int32), kt(F32), kt(jnp.int32), jax.ShapeDtypeStruct((e_rows, LANES), F32)),
        grid=(n // tr,),
        in_specs=[pl.BlockSpec((e_rows, tr), lambda i: (0, i))],
        out_specs=(blk, blk, blk, pl.BlockSpec((e_rows, LANES), lambda i: (0, 0))),
        scratch_shapes=[pltpu.VMEM((tr, tr), BF16), pltpu.VMEM((e_rows, LANES), F32)],
        compiler_params=_params(("arbitrary",)),
        name="route",
    )(logits_t)


def _dispatch_kernel(pos_ref, poff_ref, plen_ref, h_ref, xs_ref, zero_scr, sem, zsem, *, n_tok, n_exp, seg, tm):
    td = h_ref.shape[0] // seg
    step = pl.program_id(0)
    base = step * td

    def pad_copies(e, act):
        off, ln = poff_ref[e], plen_ref[e]
        for bit in range(tm.bit_length() - 1):
            size = 1 << bit

            @pl.when(((ln >> bit) & 1) == 1)
            def _():
                row = pl.multiple_of((off + (ln & (size - 1))) * seg, seg)
                act(pltpu.make_async_copy(zero_scr.at[pl.ds(0, size * seg)],
                                          xs_ref.at[pl.ds(row, size * seg)], zsem))

    def for_pads(act):
        def body(e, carry):
            pad_copies(e, act)
            return carry
        lax.fori_loop(0, n_exp, body, 0)

        half = zero_scr.shape[0]
        end = (poff_ref[n_exp - 1] + plen_ref[n_exp - 1]) * seg

        def tail(n, carry):
            row = pl.multiple_of(end + n * half, seg)
            act(pltpu.make_async_copy(zero_scr, xs_ref.at[pl.ds(row, half)], zsem))
            return carry
        lax.fori_loop(0, (xs_ref.shape[0] - end) // half, tail, 0)

    @pl.when(step == 0)
    def _():
        zero_scr[...] = jnp.zeros_like(zero_scr)
        for_pads(lambda cp: cp.start())

    def row_copy(t, k):
        p = pos_ref[k * n_tok + base + t]
        return pltpu.make_async_copy(h_ref.at[pl.ds(pl.multiple_of(t * seg, seg), seg)],
                                     xs_ref.at[pl.ds(pl.multiple_of(p * seg, seg), seg)], sem)

    def start(t, carry):
        for k in range(TOP_K):
            row_copy(t, k).start()
        return carry

    lax.fori_loop(0, td, start, 0)
    for k in range(TOP_K):
        pltpu.make_async_copy(h_ref, h_ref, sem).wait()

    @pl.when(step == 0)
    def _():
        for_pads(lambda cp: cp.wait())


def _dispatch(pos_flat, pad_off, pad_len, h2p, n_slots, seg, tm):
    n = h2p.shape[0] // seg
    n_exp = pad_off.shape[0]
    assert tm & (tm - 1) == 0
    td = _tile(n, 256)
    return pl.pallas_call(
        functools.partial(_dispatch_kernel, n_tok=n, n_exp=n_exp, seg=seg, tm=tm),
        out_shape=jax.ShapeDtypeStruct((n_slots * seg, LANES), U32),
        grid_spec=pltpu.PrefetchScalarGridSpec(
            num_scalar_prefetch=3,
            grid=(n // td,),
            in_specs=[pl.BlockSpec((td * seg, LANES), lambda i, *_: (i, 0))],
            out_specs=pl.BlockSpec(memory_space=pl.ANY),
            scratch_shapes=[pltpu.VMEM((max(tm // 2, 1) * seg, LANES), U32),
                            pltpu.SemaphoreType.DMA(()), pltpu.SemaphoreType.DMA(())]),
        compiler_params=_params(("arbitrary",)),
        name="dispatch",
    )(pos_flat, pad_off, pad_len, h2p)


def _expert_changed(te_ref, i):
    return jnp.logical_or(i == 0, te_ref[i] != te_ref[jnp.maximum(i - 1, 0)])


def _ffn1_kernel(te_ref, nu_ref, xs_ref, wg_ref, wl_ref, bg_ref, bl_ref, act_ref, wg_scr, wl_scr, x_scr, *, seg):
    i = pl.program_id(1)
    tm = x_scr.shape[0]
    half = seg * LANES

    @pl.when(_expert_changed(te_ref, i))
    def _():
        wg_scr[...] = wg_ref[0].astype(BF16)
        wl_scr[...] = wl_ref[0].astype(BF16)

    @pl.when(i < nu_ref[0])
    def _():
        for s in range(seg):
            w = xs_ref[pl.ds(s, tm, stride=seg), :]
            x_scr[:, s * LANES:(s + 1) * LANES] = lax.bitcast_convert_type(w << 16, F32).astype(BF16)
            x_scr[:, half + s * LANES:half + (s + 1) * LANES] = (
                lax.bitcast_convert_type(w & jnp.uint32(HI16), F32).astype(BF16))
        x = x_scr[...]
        glu = jnp.minimum(jnp.dot(x, wg_scr[...], preferred_element_type=F32) + bg_ref[0], SWIGLU_LIMIT)
        lin = jnp.clip(jnp.dot(x, wl_scr[...], preferred_element_type=F32) + bl_ref[0],
                       -SWIGLU_LIMIT, SWIGLU_LIMIT)
        act_ref[...] = (glu * jax.nn.sigmoid(SWIGLU_ALPHA * glu) * (lin + 1.0)).astype(act_ref.dtype)

    @pl.when(i >= nu_ref[0])
    def _():
        act_ref[...] = jnp.zeros_like(act_ref)


def _ffn1(tile_expert, n_used, xs, w1, b1_3, tm, seg):
    d = seg * LANES * 2
    slots = xs.shape[0] // seg
    n_exp, _, two_ff = w1.shape
    d_ff = two_ff // 2
    tn = _tile(d_ff, 1024)
    nj = d_ff // tn
    n_tiles = slots // tm
    row = lambda i, nu: jnp.minimum(i, nu[0] - 1)
    return pl.pallas_call(
        functools.partial(_ffn1_kernel, seg=seg),
        out_shape=jax.ShapeDtypeStruct((slots, d_ff), BF16),
        grid_spec=pltpu.PrefetchScalarGridSpec(
            num_scalar_prefetch=2,
            grid=(nj, n_tiles),
            in_specs=[pl.BlockSpec((tm * seg, LANES), lambda j, i, te, nu: (row(i, nu), 0)),
                      pl.BlockSpec((1, d, tn), lambda j, i, te, nu: (te[i], 0, j)),
                      pl.BlockSpec((1, d, tn), lambda j, i, te, nu: (te[i], 0, nj + j)),
                      pl.BlockSpec((1, 1, tn), lambda j, i, te, nu: (te[i], 0, j)),
                      pl.BlockSpec((1, 1, tn), lambda j, i, te, nu: (te[i], 0, nj + j))],
            out_specs=pl.BlockSpec((tm, tn), lambda j, i, te, nu: (i, j)),
            scratch_shapes=[pltpu.VMEM((d, tn), BF16), pltpu.VMEM((d, tn), BF16), pltpu.VMEM((tm, d), BF16)]),
        compiler_params=_params(("arbitrary", "arbitrary"), 60000 * 1024),
        name="ffn1",
    )(tile_expert, n_used, xs, w1, w1, b1_3, b1_3)


def _ffn2_kernel(te_ref, nu_ref, act_ref, w_ref, b_ref, ys_ref, w_scr, *, nseg):
    i = pl.program_id(1)
    tm = act_ref.shape[0]

    @pl.when(_expert_changed(te_ref, i))
    def _():
        w_scr[...] = w_ref[0].astype(BF16)

    @pl.when(i < nu_ref[0])
    def _():
        y = jnp.dot(act_ref[...], w_scr[...], preferred_element_type=F32) + b_ref[0]
        for s in range(nseg):
            ys_ref[0, pl.ds(s, tm, stride=nseg), :] = y[:, s * LANES:(s + 1) * LANES]

    @pl.when(i >= nu_ref[0])
    def _():
        ys_ref[...] = jnp.zeros_like(ys_ref)


def _ffn2(tile_expert, n_used, act, w2, b2_3, tm):
    slots, d_ff = act.shape
    d = w2.shape[2]
    tn = _tile(d, SUBLANES * LANES)
    nseg = tn // LANES
    n_tiles = slots // tm
    row = lambda i, nu: jnp.minimum(i, nu[0] - 1)
    return pl.pallas_call(
        functools.partial(_ffn2_kernel, nseg=nseg),
        out_shape=jax.ShapeDtypeStruct((d // tn, slots * nseg, LANES), F32),
        grid_spec=pltpu.PrefetchScalarGridSpec(
            num_scalar_prefetch=2,
            grid=(d // tn, n_tiles),
            in_specs=[pl.BlockSpec((tm, d_ff), lambda j, i, te, nu: (row(i, nu), 0)),
                      pl.BlockSpec((1, d_ff, tn), lambda j, i, te, nu: (te[i], 0, j)),
                      pl.BlockSpec((1, 1, tn), lambda j, i, te, nu: (te[i], 0, j))],
            out_specs=pl.BlockSpec((1, tm * nseg, LANES), lambda j, i, te, nu: (j, i, 0)),
            scratch_shapes=[pltpu.VMEM((d_ff, tn), BF16)]),
        compiler_params=_params(("arbitrary", "arbitrary")),
        name="ffn2",
    )(tile_expert, n_used, act, w2, b2_3)


def _combine_kernel(pos_ref, ys_ref, gate_ref, x1_ref, g2_ref, pfn_ref, o_ref, buf, moe_scr, sem, *, n_tok):
    tc = x1_ref.shape[0]
    planes = ys_ref.shape[0]
    seg = buf.shape[2] // tc
    base = pl.program_id(0) * tc

    def row_copy(t, k):
        p = pos_ref[k * n_tok + base + t]
        return pltpu.make_async_copy(ys_ref.at[:, pl.ds(pl.multiple_of(p * seg, seg), seg)],
                                     buf.at[k, :, pl.ds(pl.multiple_of(t * seg, seg), seg)], sem.at[k])

    def start(t, carry):
        for k in range(TOP_K):
            row_copy(t, k).start()
        return carry

    lax.fori_loop(0, tc, start, 0)
    for k in range(TOP_K):
        pltpu.make_async_copy(buf.at[k], buf.at[k], sem.at[k]).wait()
    gates = [jnp.broadcast_to(gate_ref[:, k:k + 1], (tc, LANES)) for k in range(TOP_K)]
    for pl_i in range(planes):
        for s in range(seg):
            m = buf[0, pl_i, pl.ds(s, tc, stride=seg), :] * gates[0]
            for k in range(1, TOP_K):
                m = m + buf[k, pl_i, pl.ds(s, tc, stride=seg), :] * gates[k]
            col = (pl_i * seg + s) * LANES
            moe_scr[:, col:col + LANES] = m
    o_ref[...] = x1_ref[...] + g2_ref[0] * _rms(moe_scr[...], pfn_ref[...])


def _combine(pos_flat, ys, gates_t, x1, mod3, post_ffn_norm, seq):
    n, d = x1.shape
    planes = ys.shape[0]
    seg = d // planes // LANES
    tc = _tile(seq, 128)
    per_batch = seq // tc
    return pl.pallas_call(
        functools.partial(_combine_kernel, n_tok=n),
        out_shape=jax.ShapeDtypeStruct((n, d), F32),
        grid_spec=pltpu.PrefetchScalarGridSpec(
            num_scalar_prefetch=1,
            grid=(n // tc,),
            in_specs=[pl.BlockSpec(memory_space=pl.ANY),
                      pl.BlockSpec((tc, TOP_K), lambda i, pos: (i, 0)),
                      pl.BlockSpec((tc, d), lambda i, pos: (i, 0)),
                      pl.BlockSpec((1, 1, d), lambda i, pos: (i // per_batch, 0, 5)),
                      pl.BlockSpec((1, d), lambda i, pos: (0, 0))],
            out_specs=pl.BlockSpec((tc, d), lambda i, pos: (i, 0)),
            scratch_shapes=[pltpu.VMEM((TOP_K, planes, tc * seg, LANES), F32), pltpu.VMEM((tc, d), F32),
                            pltpu.SemaphoreType.DMA((TOP_K,))]),
        compiler_params=_params(("arbitrary",)),
        name="combine",
    )(pos_flat, ys, gates_t, x1, mod3, post_ffn_norm)


def _moe(h2p, logits_t, x1, mod3, post_ffn_norm, w1, b1, w2, b2, seq, n_exp):
    n, d = x1.shape
    seg = d // 2 // LANES
    tm = 1 << (min(n * TOP_K, 512).bit_length() - 1)
    idx, gates, rank, cnt = _route(logits_t, n_exp)

    counts = cnt[:n_exp, 0].astype(jnp.int32)
    padded = (counts + tm - 1) // tm * tm
    ends = jnp.cumsum(padded)
    starts = ends - padded
    hot = idx[:, :, None] == jnp.arange(n_exp, dtype=jnp.int32)
    pos_flat = (jnp.sum(jnp.where(hot, starts, 0), axis=-1) + rank).reshape(-1)
    n_tiles = -(-n * TOP_K // tm) + n_exp
    n_used = (ends[-1] // tm).astype(jnp.int32)
    tile_start = jnp.minimum(jnp.arange(n_tiles, dtype=jnp.int32), n_used - 1) * tm
    tile_expert = jnp.sum(tile_start[:, None] >= ends[None, :], axis=1).astype(jnp.int32)
    n_used = n_used.reshape(1)

    xs = _dispatch(pos_flat, starts + counts, padded - counts, h2p, n_tiles * tm, seg, tm)
    act = _ffn1(tile_expert, n_used, xs, w1, b1[:, None, :], tm, seg)
    ys = _ffn2(tile_expert, n_used, act, w2, b2[:, None, :], tm)
    return _combine(pos_flat, ys, gates.T, x1, mod3, post_ffn_norm, seq)


def kernel(x, c, ctx, c_ctx, ada_w, ada_b, pre_mix_norm, post_mix_norm, pre_ffn_norm, post_ffn_norm,
           w_in, ret_decay_fwd, ret_decay_bwd, ret_gn_w, conv_w, conv_b, conv_ln_w, conv_ln_b, w_out,
           router_w, router_b, w1, b1, w2, b2):
    assert ada_w.shape[0] == 1, "single-layer stack only"
    b, t, d = x.shape
    n_ctx = ctx.shape[1]
    n_heads = ret_decay_fwd.shape[1]
    d_ret = n_heads * HEAD_DIM
    n_exp = router_w.shape[2]
    assert b < MOD_ROWS and ret_gn_w.shape[1] == d_ret and d % (2 * LANES) == 0

    cc = jnp.zeros((MOD_ROWS, d), F32).at[:b].set(c).at[b].set(c_ctx)
    mod3 = _ada(cc, ada_w[0], ada_b)[:, None, :]
    cos, sin = _rope_tables(n_ctx + t)

    w_in_bf = w_in[0].astype(BF16)
    proj = _in_proj(x, pre_mix_norm, mod3, lambda bi: bi, w_in_bf, 0, w_in.shape[2])
    ctx_kv = _in_proj(ctx, pre_mix_norm, mod3, lambda bi: b, w_in_bf, d_ret, 2 * d_ret)

    lane_bcast = lambda v: jnp.broadcast_to(v.reshape(n_heads, 1, 1), (n_heads, 1, LANES))
    y_ret = _retention(proj, ctx_kv, cos, sin, lane_bcast(ret_decay_fwd[0]), lane_bcast(ret_decay_bwd[0]),
                       ret_gn_w, n_heads)
    y_conv = _conv(proj, 4 * d_ret, conv_w[0], conv_b, conv_ln_w, conv_ln_b)

    e_rows = -(-n_exp // SUBLANES) * SUBLANES
    rw_t = jnp.zeros((e_rows, d), F32).at[:n_exp].set(router_w[0].T)
    rw_hi = rw_t.astype(BF16)
    rw_lo = (rw_t - rw_hi.astype(F32)).astype(BF16)
    rb_col = jnp.zeros((e_rows, 1), F32).at[:n_exp, 0].set(router_b[0])
    x1, h2p, logits_t = _out_proj(y_ret, y_conv, w_out[0].astype(BF16), x, mod3, post_mix_norm, pre_ffn_norm,
                                  rw_hi, rw_lo, rb_col)

    out = _moe(h2p, logits_t, x1.reshape(b * t, d), mod3, post_ffn_norm, w1[0], b1[0], w2[0], b2[0], t, n_exp)
    return out.reshape(b, t, d)
```

```python
import functools
import math

import jax
import jax.numpy as jnp
from jax import lax
from jax.experimental import pallas as pl
from jax.experimental.pallas import tpu as pltpu

F32 = jnp.float32
BF16 = jnp.bfloat16
U32 = jnp.uint32
HIGHEST = lax.Precision.HIGHEST

GRID_W = 64
HEAD_DIM = 128
RET_CHUNK = 256
ROPE_BASE = 10000.0
TOP_K = 4
SWIGLU_ALPHA = 1.702
SWIGLU_LIMIT = 7.0
EPS = 1e-6
GN_EPS = 1e-5
LANES = 128
SUBLANES = 8
MOD_ROWS = 16
VMEM_LIMIT = 56 * 1024 * 1024
HI16 = 0xFFFF0000


def _tile(n, pref):
    t = min(n, pref)
    while n % t:
        t -= 1
    return t


def _params(sem, vmem=VMEM_LIMIT):
    return pltpu.CompilerParams(dimension_semantics=sem, vmem_limit_bytes=vmem)


def _rms(x, w):
    return x * lax.rsqrt(jnp.mean(x * x, axis=-1, keepdims=True) + EPS) * w


def _silu(x):
    return x * jax.nn.sigmoid(x)


def _dot_nt(a, b):
    return lax.dot_general(a, b, (((1,), (1,)), ((), ())), preferred_element_type=F32)


def _dot_tn(a, b):
    return lax.dot_general(a, b, (((0,), (0,)), ((), ())), preferred_element_type=F32)


def _ada_kernel(c_ref, w_ref, b_ref, o_ref):
    s = _silu(c_ref[...])
    o_ref[...] = jnp.dot(s, w_ref[...], preferred_element_type=F32, precision=HIGHEST) + b_ref[...]


def _ada(cc, w, b):
    d, n = w.shape
    tn = _tile(n, 1024)
    return pl.pallas_call(
        _ada_kernel,
        out_shape=jax.ShapeDtypeStruct((MOD_ROWS, n), F32),
        grid=(n // tn,),
        in_specs=[pl.BlockSpec((MOD_ROWS, d), lambda j: (0, 0)),
                  pl.BlockSpec((d, tn), lambda j: (0, j)),
                  pl.BlockSpec((1, tn), lambda j: (0, j))],
        out_specs=pl.BlockSpec((MOD_ROWS, tn), lambda j: (0, j)),
        compiler_params=_params(("parallel",)),
        name="ada",
    )(cc, w, b)


def _rope_kernel(cos_ref, sin_ref):
    p, _ = cos_ref.shape
    half = HEAD_DIM // 2
    lane = lax.broadcasted_iota(jnp.int32, (p, HEAD_DIM), 1)
    pos = lax.broadcasted_iota(jnp.int32, (p, HEAD_DIM), 0).astype(F32)
    j = jnp.where(lane < half, lane, lane - half).astype(F32)
    inv = jnp.exp(j * (-jnp.log(ROPE_BASE) / half))
    ang = pos * inv
    cos_ref[...] = jnp.cos(ang)
    sin_ref[...] = jnp.where(lane < half, -1.0, 1.0) * jnp.sin(ang)


def _rope_tables(p):
    return pl.pallas_call(
        _rope_kernel,
        out_shape=(jax.ShapeDtypeStruct((p, HEAD_DIM), F32), jax.ShapeDtypeStruct((p, HEAD_DIM), F32)),
        name="rope",
    )()


def _rot(t, cos, sin_signed):
    return t * cos + pltpu.roll(t, HEAD_DIM // 2, axis=1) * sin_signed


def _inproj_kernel(x_ref, nw_ref, sh_ref, sc_ref, w_ref, o_ref, h_scr):
    @pl.when(pl.program_id(2) == 0)
    def _():
        h = _rms(x_ref[0], nw_ref[...]) * (1.0 + sc_ref[0]) + sh_ref[0]
        h_scr[...] = h.astype(BF16)

    o_ref[0] = jnp.dot(h_scr[...], w_ref[...], preferred_element_type=F32)


def _in_proj(x, norm_w, mod3, mod_row, w_bf, col0, ncols):
    b, t, d = x.shape
    tm = _tile(t, 1024)
    tn = _tile(math.gcd(ncols, col0), 1536)
    joff = col0 // tn
    return pl.pallas_call(
        _inproj_kernel,
        out_shape=jax.ShapeDtypeStruct((b, t, ncols), F32),
        grid=(b, t // tm, ncols // tn),
        in_specs=[pl.BlockSpec((1, tm, d), lambda bi, i, j: (bi, i, 0)),
                  pl.BlockSpec((1, d), lambda bi, i, j: (0, 0)),
                  pl.BlockSpec((1, 1, d), lambda bi, i, j: (mod_row(bi), 0, 0)),
                  pl.BlockSpec((1, 1, d), lambda bi, i, j: (mod_row(bi), 0, 1)),
                  pl.BlockSpec((d, tn), lambda bi, i, j: (0, j + joff))],
        out_specs=pl.BlockSpec((1, tm, tn), lambda bi, i, j: (bi, i, j)),
        scratch_shapes=[pltpu.VMEM((tm, d), BF16)],
        compiler_params=_params(("parallel", "parallel", "arbitrary")),
        name="in_proj",
    )(x, norm_w, mod3, mod3, w_bf)


def _log_sigmoid(x):
    return jnp.minimum(x, 0.0) - jnp.log(1.0 + jnp.exp(-jnp.abs(x)))


def _ret_kernel(q_ref, k_ref, v_ref, g_ref, kc_ref, vc_ref, cos_ref, sin_ref, decf_ref, decb_ref,
                gnw_ref, o_ref, q_scr, k_scr, kf_scr, kb_scr, v_scr, o_scr, din_scr, kvf_scr, kvb_scr,
                rf_scr, rb_scr, *, n_ctx):
    t = q_ref.shape[1]
    c = din_scr.shape[0]
    nc = t // c
    lgf = _log_sigmoid(decf_ref[0])[:, 0:1]
    lgb = _log_sigmoid(decb_ref[0])[:, 0:1]

    kc = _rot(kc_ref[0], cos_ref[0:n_ctx, :], sin_ref[0:n_ctx, :])
    vc = vc_ref[0].astype(BF16)
    tc = lax.broadcasted_iota(jnp.int32, (n_ctx, 1), 0).astype(F32)
    rf = _dot_tn((kc * jnp.exp(lgf * (n_ctx - 1.0 - tc))).astype(BF16), vc)
    rb = _dot_tn((kc * jnp.exp(lgb * tc)).astype(BF16), vc)

    scale = HEAD_DIM ** -0.5
    cos, sin = cos_ref[n_ctx:n_ctx + t, :], sin_ref[n_ctx:n_ctx + t, :]
    q_scr[...] = (_rot(q_ref[0], cos, sin) * scale).astype(BF16)
    kr = _rot(k_ref[0], cos, sin)
    ri_all = (lax.broadcasted_iota(jnp.int32, (t, 1), 0) % c).astype(F32)
    k_scr[...] = kr.astype(BF16)
    kf_scr[...] = (kr * jnp.exp(lgf * (c - 1.0 - ri_all))).astype(BF16)
    kb_scr[...] = (kr * jnp.exp(lgb * ri_all)).astype(BF16)
    v_scr[...] = v_ref[0].astype(BF16)

    diff = (lax.broadcasted_iota(jnp.int32, (c, c), 0) - lax.broadcasted_iota(jnp.int32, (c, c), 1)).astype(F32)
    din_scr[...] = jnp.where(diff > 0, jnp.exp(lgf * jnp.maximum(diff, 0.0)),
                             jnp.where(diff < 0, jnp.exp(lgb * jnp.maximum(-diff, 0.0)), 2.0))

    for i in range(nc):
        sl = slice(i * c, (i + 1) * c)
        vi = v_scr[sl, :]
        scores = _dot_nt(q_scr[sl, :], k_scr[sl, :]) * din_scr[...]
        o_scr[sl, :] = jnp.dot(scores.astype(BF16), vi, preferred_element_type=F32)
        kvf_scr[i] = _dot_tn(kf_scr[sl, :], vi)
        kvb_scr[i] = _dot_tn(kb_scr[sl, :], vi)

    gc_f = jnp.exp(lgf * float(c))
    gc_b = jnp.exp(lgb * float(c))
    for i in range(nc):
        rf_scr[i] = rf.astype(BF16)
        rf = rf * gc_f + kvf_scr[i]
    for i in reversed(range(nc)):
        rb_scr[i] = rb.astype(BF16)
        rb = rb * gc_b + kvb_scr[i]

    ri = lax.broadcasted_iota(jnp.int32, (c, 1), 0).astype(F32)
    dq_f = jnp.exp(lgf * (ri + 1.0))
    dq_b = jnp.exp(lgb * (c - ri))
    for i in range(nc):
        sl = slice(i * c, (i + 1) * c)
        qi = q_scr[sl, :]
        o = (o_scr[sl, :] + jnp.dot(qi, rf_scr[i], preferred_element_type=F32) * dq_f
             + jnp.dot(qi, rb_scr[i], preferred_element_type=F32) * dq_b)
        mu = jnp.mean(o, axis=-1, keepdims=True)
        var = jnp.mean(jnp.square(o - mu), axis=-1, keepdims=True)
        on = (o - mu) * lax.rsqrt(var + GN_EPS) * gnw_ref[...]
        o_ref[0, sl, :] = (on * _silu(g_ref[0, sl, :])).astype(o_ref.dtype)


def _retention(proj, ctx_kv, cos, sin, decf3, decb3, gn_w, n_heads):
    b, t, _ = proj.shape
    n_ctx = ctx_kv.shape[1]
    hd = HEAD_DIM
    c = _tile(t, RET_CHUNK)
    nc = t // c
    lat = lambda g: pl.BlockSpec((1, t, hd), lambda bi, h: (bi, 0, g * n_heads + h))
    seq = lambda dt: pltpu.VMEM((t, hd), dt)
    return pl.pallas_call(
        functools.partial(_ret_kernel, n_ctx=n_ctx),
        out_shape=jax.ShapeDtypeStruct((b, t, n_heads * hd), BF16),
        grid=(b, n_heads),
        in_specs=[lat(0), lat(1), lat(2), lat(3),
                  pl.BlockSpec((1, n_ctx, hd), lambda bi, h: (bi, 0, h)),
                  pl.BlockSpec((1, n_ctx, hd), lambda bi, h: (bi, 0, n_heads + h)),
                  pl.BlockSpec(cos.shape, lambda bi, h: (0, 0)),
                  pl.BlockSpec(sin.shape, lambda bi, h: (0, 0)),
                  pl.BlockSpec((1, 1, LANES), lambda bi, h: (h, 0, 0)),
                  pl.BlockSpec((1, 1, LANES), lambda bi, h: (h, 0, 0)),
                  pl.BlockSpec((1, hd), lambda bi, h: (0, h))],
        out_specs=pl.BlockSpec((1, t, hd), lambda bi, h: (bi, 0, h)),
        scratch_shapes=[seq(BF16), seq(BF16), seq(BF16), seq(BF16), seq(BF16), seq(F32),
                        pltpu.VMEM((c, c), F32),
                        pltpu.VMEM((nc, hd, hd), F32), pltpu.VMEM((nc, hd, hd), F32),
                        pltpu.VMEM((nc, hd, hd), BF16), pltpu.VMEM((nc, hd, hd), BF16)],
        compiler_params=_params(("parallel", "parallel")),
        name="ret",
    )(proj, proj, proj, proj, ctx_kv, ctx_kv, cos, sin, decf3, decb3, gn_w)


CONV_PAD_ROWS = 16
CONV_LANE_CHUNK = 256


def _conv_kernel(a_ref, b_ref, w_ref, cb_ref, lnw_ref, lnb_ref, o_ref, up_scr, sh_scr, y_scr):
    tt, ch = a_ref.shape[1], a_ref.shape[2]
    kw = w_ref.shape[0]
    n_seq = tt // GRID_W
    lead = CONV_PAD_ROWS - kw // 2
    rows = GRID_W + 2 * CONV_PAD_ROWS
    u = a_ref[0] * jax.nn.sigmoid(b_ref[0])
    zeros = jnp.zeros((CONV_PAD_ROWS, ch), F32)
    for s in range(n_seq):
        up_scr[s, 0:CONV_PAD_ROWS, :] = zeros
        up_scr[s, CONV_PAD_ROWS:CONV_PAD_ROWS + GRID_W, :] = u[s * GRID_W:(s + 1) * GRID_W, :]
        up_scr[s, CONV_PAD_ROWS + GRID_W:, :] = zeros

    cw = sh_scr.shape[2]

    def seq(s, carry):
        row0 = pl.multiple_of(s * GRID_W, GRID_W)
        for c0 in range(0, ch, cw):
            for r in range(SUBLANES):
                sh_scr[r] = up_scr[s, r:r + rows - SUBLANES, c0:c0 + cw]
            acc = jnp.broadcast_to(cb_ref[:, c0:c0 + cw], (GRID_W, cw))
            for k in range(kw):
                a8, r = divmod(lead + k, SUBLANES)
                acc = acc + sh_scr[r, a8 * SUBLANES:a8 * SUBLANES + GRID_W, :] * w_ref[k:k + 1, c0:c0 + cw]
            y_scr[pl.ds(row0, GRID_W), c0:c0 + cw] = acc
        return carry

    lax.fori_loop(0, n_seq, seq, 0)
    y = y_scr[...]
    mu = jnp.mean(y, axis=-1, keepdims=True)
    var = jnp.mean(jnp.square(y - mu), axis=-1, keepdims=True)
    yn = (y - mu) * lax.rsqrt(var + EPS) * lnw_ref[...] + lnb_ref[...]
    o_ref[0] = _silu(yn).astype(o_ref.dtype)


def _conv(proj, col0, conv_w, conv_b, ln_w, ln_b):
    b, t, _ = proj.shape
    kw, ch = conv_w.shape
    assert col0 % ch == 0 and kw // 2 <= CONV_PAD_ROWS and t % GRID_W == 0
    tt = _tile(t, 4 * GRID_W)
    cw = min(ch, CONV_LANE_CHUNK)
    rows = GRID_W + 2 * CONV_PAD_ROWS
    ca, cb = col0 // ch, col0 // ch + 1
    vec = pl.BlockSpec((1, ch), lambda bi, i: (0, 0))
    return pl.pallas_call(
        _conv_kernel,
        out_shape=jax.ShapeDtypeStruct((b, t, ch), BF16),
        grid=(b, t // tt),
        in_specs=[pl.BlockSpec((1, tt, ch), lambda bi, i: (bi, i, ca)),
                  pl.BlockSpec((1, tt, ch), lambda bi, i: (bi, i, cb)),
                  pl.BlockSpec((kw, ch), lambda bi, i: (0, 0)),
                  vec, vec, vec],
        out_specs=pl.BlockSpec((1, tt, ch), lambda bi, i: (bi, i, 0)),
        scratch_shapes=[pltpu.VMEM((tt // GRID_W, rows, ch), F32),
                        pltpu.VMEM((SUBLANES, rows - SUBLANES, cw), F32),
                        pltpu.VMEM((tt, ch), F32)],
        compiler_params=_params(("parallel", "parallel")),
        name="conv",
    )(proj, proj, conv_w, conv_b, ln_w, ln_b)


def _outproj_kernel(yr_ref, yc_ref, wr_ref, wc_ref, x_ref, g1_ref, sh2_ref, sc2_ref, pmn_ref, pfn_ref,
                    rwh_ref, rwl_ref, rb_ref, x1_ref, h2p_ref, lg_ref):
    tm = x_ref.shape[1]
    mix = (jnp.dot(yr_ref[0], wr_ref[...], preferred_element_type=F32)
           + jnp.dot(yc_ref[0], wc_ref[...], preferred_element_type=F32))
    x1 = x_ref[0] + g1_ref[0] * _rms(mix, pmn_ref[...])
    x1_ref[0] = x1
    h2 = _rms(x1, pfn_ref[...]) * (1.0 + sc2_ref[0]) + sh2_ref[0]
    h2_hi = h2.astype(BF16)
    h2_lo = (h2 - h2_hi.astype(F32)).astype(BF16)
    lg_ref[...] = (_dot_nt(rwh_ref[...], h2_hi) + _dot_nt(rwh_ref[...], h2_lo)
                   + _dot_nt(rwl_ref[...], h2_hi) + rb_ref[...])
    _pack_rows(h2, h2p_ref, tm)


def _out_proj(y_ret, y_conv, w_out_bf, x, mod3, post_mix_norm, pre_ffn_norm, rw_hi, rw_lo, rb_col):
    b, t, d = x.shape
    d_ret, d_conv = y_ret.shape[2], y_conv.shape[2]
    e_rows = rw_hi.shape[0]
    seg = d // 2 // LANES
    tm = _tile(t, 512)
    per_b = t // tm
    mod = lambda col: pl.BlockSpec((1, 1, d), lambda bi, i: (bi, 0, col))
    vec = pl.BlockSpec((1, d), lambda bi, i: (0, 0))
    row = lambda width: pl.BlockSpec((1, tm, width), lambda bi, i: (bi, i, 0))
    rw = pl.BlockSpec((e_rows, d), lambda bi, i: (0, 0))
    w_ret, w_conv = w_out_bf[:d_ret], w_out_bf[d_ret:]
    return pl.pallas_call(
        _outproj_kernel,
        out_shape=(jax.ShapeDtypeStruct((b, t, d), F32),
                   jax.ShapeDtypeStruct((b * t * seg, LANES), U32),
                   jax.ShapeDtypeStruct((e_rows, b * t), F32)),
        grid=(b, per_b),
        in_specs=[row(d_ret), row(d_conv),
                  pl.BlockSpec((d_ret, d), lambda bi, i: (0, 0)),
                  pl.BlockSpec((d_conv, d), lambda bi, i: (0, 0)),
                  row(d), mod(2), mod(3), mod(4), vec, vec, rw, rw,
                  pl.BlockSpec((e_rows, 1), lambda bi, i: (0, 0))],
        out_specs=(row(d),
                   pl.BlockSpec((tm * seg, LANES), lambda bi, i: (bi * per_b + i, 0)),
                   pl.BlockSpec((e_rows, tm), lambda bi, i: (0, bi * per_b + i))),
        compiler_params=_params(("parallel", "parallel")),
        name="out_proj",
    )(y_ret, y_conv, w_ret, w_conv, x, mod3, mod3, mod3, post_mix_norm, pre_ffn_norm, rw_hi, rw_lo, rb_col)


def _route_kernel(lg_ref, idx_ref, gate_ref, rank_ref, cnt_ref, tri_scr, run_scr, *, n_experts):
    e_rows, tr = lg_ref.shape

    @pl.when(pl.program_id(0) == 0)
    def _():
        r = lax.broadcasted_iota(jnp.int32, (tr, tr), 0)
        c = lax.broadcasted_iota(jnp.int32, (tr, tr), 1)
        tri_scr[...] = jnp.where(r <= c, 1.0, 0.0).astype(BF16)
        run_scr[...] = jnp.zeros_like(run_scr)

    e_iota = lax.broadcasted_iota(jnp.int32, (e_rows, tr), 0)
    neg = jnp.float32(-jnp.inf)
    logits = jnp.where(e_iota < n_experts, lg_ref[...], neg)
    vals, hots = [], []
    for k in range(TOP_K):
        m = jnp.max(logits, axis=0, keepdims=True)
        ik = jnp.min(jnp.where(logits == m, e_iota, e_rows), axis=0, keepdims=True)
        hot = e_iota == ik
        logits = jnp.where(hot, neg, logits)
        vals.append(m)
        hots.append(hot)
        idx_ref[k:k + 1, :] = ik
    exps = [jnp.exp(v - vals[0]) for v in vals]
    den = exps[0]
    for e in exps[1:]:
        den = den + e
    for k in range(TOP_K):
        gate_ref[k:k + 1, :] = exps[k] / den

    sel = jnp.zeros((e_rows, tr), F32)
    for hot in hots:
        sel = sel + jnp.where(hot, 1.0, 0.0)
    csum = jnp.dot(sel.astype(BF16), tri_scr[...], preferred_element_type=F32)
    before = run_scr[:, 0:1] + csum - sel
    for k in range(TOP_K):
        rank_ref[k:k + 1, :] = jnp.sum(jnp.where(hots[k], before, 0.0), axis=0, keepdims=True).astype(jnp.int32)
    run_scr[...] = run_scr[...] + jnp.sum(sel, axis=1, keepdims=True)
    cnt_ref[...] = run_scr[...]


def _route(logits_t, n_experts):
    e_rows, n = logits_t.shape
    tr = _tile(n, 512)
    kt = lambda dt: jax.ShapeDtypeStruct((TOP_K, n), dt)
    blk = pl.BlockSpec((TOP_K, tr), lambda i: (0, i))
    return pl.pallas_call(
        functools.partial(_route_kernel, n_experts=n_experts),
        out_shape=(kt(jnp.int32), kt(F32), kt(jnp.int32), jax.ShapeDtypeStruct((e_rows, LANES), F32)),
        grid=(n // tr,),
        in_specs=[pl.BlockSpec((e_rows, tr), lambda i: (0, i))],
        out_specs=(blk, blk, blk, pl.BlockSpec((e_rows, LANES), lambda i: (0, 0))),
        scratch_shapes=[pltpu.VMEM((tr, tr), BF16), pltpu.VMEM((e_rows, LANES), F32)],
        compiler_params=_params(("arbitrary",)),
        name="route",
    )(logits_t)


def _dispatch_kernel(pos_ref, poff_ref, plen_ref, h_ref, xs_ref, zero_scr, sem, zsem, *, n_tok, n_exp, seg, tm):
    td = h_ref.shape[0] // seg
    step = pl.program_id(0)
    base = step * td

    def pad_copies(e, act):
        off, ln = poff_ref[e], plen_ref[e]
        for bit in range(tm.bit_length() - 1):
            size = 1 << bit

            @pl.when(((ln >> bit) & 1) == 1)
            def _():
                row = pl.multiple_of((off + (ln & (size - 1))) * seg, seg)
                act(pltpu.make_async_copy(zero_scr.at[pl.ds(0, size * seg)],
                                          xs_ref.at[pl.ds(row, size * seg)], zsem))

    def for_pads(act):
        def body(e, carry):
            pad_copies(e, act)
            return carry
        lax.fori_loop(0, n_exp, body, 0)

        half = zero_scr.shape[0]
        end = (poff_ref[n_exp - 1] + plen_ref[n_exp - 1]) * seg

        def tail(n, carry):
            row = pl.multiple_of(end + n * half, seg)
            act(pltpu.make_async_copy(zero_scr, xs_ref.at[pl.ds(row, half)], zsem))
            return carry
        lax.fori_loop(0, (xs_ref.shape[0] - end) // half, tail, 0)

    @pl.when(step == 0)
    def _():
        zero_scr[...] = jnp.zeros_like(zero_scr)
        for_pads(lambda cp: cp.start())

    def row_copy(t, k):
        p = pos_ref[k * n_tok + base + t]
        return pltpu.make_async_copy(h_ref.at[pl.ds(pl.multiple_of(t * seg, seg), seg)],
                                     xs_ref.at[pl.ds(pl.multiple_of(p * seg, seg), seg)], sem)

    def start(t, carry):
        for k in range(TOP_K):
            row_copy(t, k).start()
        return carry

    lax.fori_loop(0, td, start, 0)
    for k in range(TOP_K):
        pltpu.make_async_copy(h_ref, h_ref, sem).wait()

    @pl.when(step == 0)
    def _():
        for_pads(lambda cp: cp.wait())


def _dispatch(pos_flat, pad_off, pad_len, h2p, n_slots, seg, tm):
    n = h2p.shape[0] // seg
    n_exp = pad_off.shape[0]
    assert tm & (tm - 1) == 0
    td = _tile(n, 256)
    return pl.pallas_call(
        functools.partial(_dispatch_kernel, n_tok=n, n_exp=n_exp, seg=seg, tm=tm),
        out_shape=jax.ShapeDtypeStruct((n_slots * seg, LANES), U32),
        grid_spec=pltpu.PrefetchScalarGridSpec(
            num_scalar_prefetch=3,
            grid=(n // td,),
            in_specs=[pl.BlockSpec((td * seg, LANES), lambda i, *_: (i, 0))],
            out_specs=pl.BlockSpec(memory_space=pl.ANY),
            scratch_shapes=[pltpu.VMEM((max(tm // 2, 1) * seg, LANES), U32),
                            pltpu.SemaphoreType.DMA(()), pltpu.SemaphoreType.DMA(())]),
        compiler_params=_params(("arbitrary",)),
        name="dispatch",
    )(pos_flat, pad_off, pad_len, h2p)


def _pack_rows(val, ref, tm):
    half = val.shape[1] // 2
    seg = half // LANES
    vb = val.astype(BF16).astype(F32)
    packed = ((lax.bitcast_convert_type(vb[:, :half], U32) >> 16)
              | (lax.bitcast_convert_type(vb[:, half:], U32) & jnp.uint32(HI16)))
    for s in range(seg):
        ref[pl.ds(s, tm, stride=seg), :] = packed[:, s * LANES:(s + 1) * LANES]


def _unpack_seg(words):
    return (lax.bitcast_convert_type(words << 16, F32),
            lax.bitcast_convert_type(words & jnp.uint32(HI16), F32))


def _weight_runs(tile_expert, n_passes):
    n_tiles = tile_expert.shape[0]
    total = n_tiles * n_passes
    e_lin = jnp.tile(tile_expert, n_passes)
    j_lin = jnp.repeat(jnp.arange(n_passes, dtype=jnp.int32), n_tiles)
    i_lin = jnp.tile(jnp.arange(n_tiles, dtype=jnp.int32), n_passes)
    start = (i_lin == 0) | (e_lin != jnp.roll(e_lin, 1))
    idx = jnp.arange(total, dtype=jnp.int32)
    nxt = jnp.min(jnp.where(start[None, :] & (idx[None, :] > idx[:, None]), idx[None, :], total), axis=1)
    has = nxt < total
    nxt = jnp.minimum(nxt, total - 1)
    return start.astype(jnp.int32), jnp.where(has, e_lin[nxt], -1).astype(jnp.int32), j_lin[nxt]


def _stream_weights(step, first_ref, ne_ref, nj_ref, cur, copies, cast):
    @pl.when(step == 0)
    def _():
        for cp in copies(*cur):
            cp.start()

    @pl.when(first_ref[step] == 1)
    def _():
        for cp in copies(*cur):
            cp.wait()
        cast()

        @pl.when(ne_ref[step] >= 0)
        def _():
            for cp in copies(ne_ref[step], nj_ref[step]):
                cp.start()


def _ffn1_kernel(te_ref, nu_ref, first_ref, ne_ref, nj_ref, xs_ref, w_hbm, bg_ref, bl_ref, act_ref,
                 wg_stage, wl_stage, wg_scr, wl_scr, x_scr, sem, *, seg, d_ff):
    j, i = pl.program_id(0), pl.program_id(1)
    tm = x_scr.shape[0]
    tn = wg_scr.shape[1]
    half = seg * LANES

    def copies(e, jj):
        col = pl.multiple_of(jj * tn, tn)
        return (pltpu.make_async_copy(w_hbm.at[e, :, pl.ds(col, tn)], wg_stage, sem.at[0]),
                pltpu.make_async_copy(w_hbm.at[e, :, pl.ds(d_ff + col, tn)], wl_stage, sem.at[1]))

    def cast():
        wg_scr[...] = wg_stage[...].astype(BF16)
        wl_scr[...] = wl_stage[...].astype(BF16)

    _stream_weights(j * pl.num_programs(1) + i, first_ref, ne_ref, nj_ref, (te_ref[i], j), copies, cast)

    @pl.when(i < nu_ref[0])
    def _():
        for s in range(seg):
            lo, hi = _unpack_seg(xs_ref[pl.ds(s, tm, stride=seg), :])
            x_scr[:, s * LANES:(s + 1) * LANES] = lo.astype(BF16)
            x_scr[:, half + s * LANES:half + (s + 1) * LANES] = hi.astype(BF16)
        x = x_scr[...]
        glu = jnp.minimum(jnp.dot(x, wg_scr[...], preferred_element_type=F32) + bg_ref[0], SWIGLU_LIMIT)
        lin = jnp.clip(jnp.dot(x, wl_scr[...], preferred_element_type=F32) + bl_ref[0],
                       -SWIGLU_LIMIT, SWIGLU_LIMIT)
        act_ref[...] = (glu * jax.nn.sigmoid(SWIGLU_ALPHA * glu) * (lin + 1.0)).astype(act_ref.dtype)

    @pl.when(i >= nu_ref[0])
    def _():
        act_ref[...] = jnp.zeros_like(act_ref)


def _ffn1(tile_expert, n_used, xs, w1, b1_3, tm, seg):
    d = seg * LANES * 2
    slots = xs.shape[0] // seg
    d_ff = w1.shape[2] // 2
    tn = _tile(d_ff, 1024)
    nj = d_ff // tn
    n_tiles = slots // tm
    row = lambda i, nu: jnp.minimum(i, nu[0] - 1)
    return pl.pallas_call(
        functools.partial(_ffn1_kernel, seg=seg, d_ff=d_ff),
        out_shape=jax.ShapeDtypeStruct((slots, d_ff), BF16),
        grid_spec=pltpu.PrefetchScalarGridSpec(
            num_scalar_prefetch=5,
            grid=(nj, n_tiles),
            in_specs=[pl.BlockSpec((tm * seg, LANES), lambda j, i, te, nu, *_: (row(i, nu), 0)),
                      pl.BlockSpec(memory_space=pl.ANY),
                      pl.BlockSpec((1, 1, tn), lambda j, i, te, *_: (te[i], 0, j)),
                      pl.BlockSpec((1, 1, tn), lambda j, i, te, *_: (te[i], 0, nj + j))],
            out_specs=pl.BlockSpec((tm, tn), lambda j, i, *_: (i, j)),
            scratch_shapes=[pltpu.VMEM((d, tn), F32), pltpu.VMEM((d, tn), F32),
                            pltpu.VMEM((d, tn), BF16), pltpu.VMEM((d, tn), BF16),
                            pltpu.VMEM((tm, d), BF16), pltpu.SemaphoreType.DMA((2,))]),
        compiler_params=_params(("arbitrary", "arbitrary")),
        name="ffn1",
    )(tile_expert, n_used, *_weight_runs(tile_expert, nj), xs, w1, b1_3, b1_3)


def _ffn2_kernel(te_ref, nu_ref, first_ref, ne_ref, nj_ref, act_ref, w_hbm, b_ref, ys_ref, w_stage, w_scr, sem):
    i = pl.program_id(0)
    tm = act_ref.shape[0]

    def copies(e, jj):
        del jj
        return (pltpu.make_async_copy(w_hbm.at[e], w_stage, sem),)

    def cast():
        w_scr[...] = w_stage[...].astype(BF16)

    _stream_weights(i, first_ref, ne_ref, nj_ref, (te_ref[i], 0), copies, cast)

    @pl.when(i < nu_ref[0])
    def _():
        _pack_rows(jnp.dot(act_ref[...], w_scr[...], preferred_element_type=F32) + b_ref[0], ys_ref, tm)

    @pl.when(i >= nu_ref[0])
    def _():
        ys_ref[...] = jnp.zeros_like(ys_ref)


def _ffn2(tile_expert, n_used, act, w2, b2_3, tm):
    slots, d_ff = act.shape
    d = w2.shape[2]
    seg = d // 2 // LANES
    n_tiles = slots // tm
    row = lambda i, nu: jnp.minimum(i, nu[0] - 1)
    return pl.pallas_call(
        _ffn2_kernel,
        out_shape=jax.ShapeDtypeStruct((slots * seg, LANES), U32),
        grid_spec=pltpu.PrefetchScalarGridSpec(
            num_scalar_prefetch=5,
            grid=(n_tiles,),
            in_specs=[pl.BlockSpec((tm, d_ff), lambda i, te, nu, *_: (row(i, nu), 0)),
                      pl.BlockSpec(memory_space=pl.ANY),
                      pl.BlockSpec((1, 1, d), lambda i, te, *_: (te[i], 0, 0))],
            out_specs=pl.BlockSpec((tm * seg, LANES), lambda i, *_: (i, 0)),
            scratch_shapes=[pltpu.VMEM((d_ff, d), F32), pltpu.VMEM((d_ff, d), BF16),
                            pltpu.SemaphoreType.DMA(())]),
        compiler_params=_params(("arbitrary",)),
        name="ffn2",
    )(tile_expert, n_used, *_weight_runs(tile_expert, 1), act, w2, b2_3)


def _combine_kernel(pos_ref, ys_ref, gate_ref, x1_ref, g2_ref, pfn_ref, o_ref, buf, moe_scr, sem, *, n_tok):
    tc, d = x1_ref.shape
    seg = buf.shape[2] // tc
    half = seg * LANES
    step = pl.program_id(0)

    def gather(st, slot):
        def body(t, carry):
            for k in range(TOP_K):
                p = pos_ref[k * n_tok + st * tc + t]
                pltpu.make_async_copy(ys_ref.at[pl.ds(pl.multiple_of(p * seg, seg), seg)],
                                      buf.at[slot, k, pl.ds(pl.multiple_of(t * seg, seg), seg)],
                                      sem.at[slot, k]).start()
            return carry
        lax.fori_loop(0, tc, body, 0)

    @pl.when(step == 0)
    def _():
        gather(0, 0)

    @pl.when(step + 1 < pl.num_programs(0))
    def _():
        gather(step + 1, (step + 1) % 2)

    slot = step % 2
    for k in range(TOP_K):
        pltpu.make_async_copy(buf.at[slot, k], buf.at[slot, k], sem.at[slot, k]).wait()
    gates = [jnp.broadcast_to(gate_ref[:, k:k + 1], (tc, LANES)) for k in range(TOP_K)]
    for s in range(seg):
        m_lo = m_hi = None
        for k in range(TOP_K):
            lo, hi = _unpack_seg(buf[slot, k, pl.ds(s, tc, stride=seg), :])
            m_lo = lo * gates[k] if m_lo is None else m_lo + lo * gates[k]
            m_hi = hi * gates[k] if m_hi is None else m_hi + hi * gates[k]
        moe_scr[:, s * LANES:(s + 1) * LANES] = m_lo
        moe_scr[:, half + s * LANES:half + (s + 1) * LANES] = m_hi
    o_ref[...] = x1_ref[...] + g2_ref[0] * _rms(moe_scr[...], pfn_ref[...])


def _combine(pos_flat, ys, gates_t, x1, mod3, post_ffn_norm, seq):
    n, d = x1.shape
    seg = d // 2 // LANES
    tc = _tile(seq, 128)
    per_batch = seq // tc
    return pl.pallas_call(
        functools.partial(_combine_kernel, n_tok=n),
        out_shape=jax.ShapeDtypeStruct((n, d), F32),
        grid_spec=pltpu.PrefetchScalarGridSpec(
            num_scalar_prefetch=1,
            grid=(n // tc,),
            in_specs=[pl.BlockSpec(memory_space=pl.ANY),
                      pl.BlockSpec((tc, TOP_K), lambda i, pos: (i, 0)),
                      pl.BlockSpec((tc, d), lambda i, pos: (i, 0)),
                      pl.BlockSpec((1, 1, d), lambda i, pos: (i // per_batch, 0, 5)),
                      pl.BlockSpec((1, d), lambda i, pos: (0, 0))],
            out_specs=pl.BlockSpec((tc, d), lambda i, pos: (i, 0)),
            scratch_shapes=[pltpu.VMEM((2, TOP_K, tc * seg, LANES), U32), pltpu.VMEM((tc, d), F32),
                            pltpu.SemaphoreType.DMA((2, TOP_K))]),
        compiler_params=_params(("arbitrary",)),
        name="combine",
    )(pos_flat, ys, gates_t, x1, mod3, post_ffn_norm)


def _moe(h2p, logits_t, x1, mod3, post_ffn_norm, w1, b1, w2, b2, seq, n_exp):
    n, d = x1.shape
    seg = d // 2 // LANES
    tm = 1 << (min(n * TOP_K, 512).bit_length() - 1)
    idx, gates, rank, cnt = _route(logits_t, n_exp)

    counts = cnt[:n_exp, 0].astype(jnp.int32)
    padded = (counts + tm - 1) // tm * tm
    ends = jnp.cumsum(padded)
    starts = ends - padded
    hot = idx[:, :, None] == jnp.arange(n_exp, dtype=jnp.int32)
    pos_flat = (jnp.sum(jnp.where(hot, starts, 0), axis=-1) + rank).reshape(-1)
    n_tiles = -(-n * TOP_K // tm) + n_exp
    n_used = (ends[-1] // tm).astype(jnp.int32)
    tile_start = jnp.minimum(jnp.arange(n_tiles, dtype=jnp.int32), n_used - 1) * tm
    tile_expert = jnp.sum(tile_start[:, None] >= ends[None, :], axis=1).astype(jnp.int32)
    n_used = n_used.reshape(1)

    xs = _dispatch(pos_flat, starts + counts, padded - counts, h2p, n_tiles * tm, seg, tm)
    act = _ffn1(tile_expert, n_used, xs, w1, b1[:, None, :], tm, seg)
    ys = _ffn2(tile_expert, n_used, act, w2, b2[:, None, :], tm)
    return _combine(pos_flat, ys, gates.T, x1, mod3, post_ffn_norm, seq)


def kernel(x, c, ctx, c_ctx, ada_w, ada_b, pre_mix_norm, post_mix_norm, pre_ffn_norm, post_ffn_norm,
           w_in, ret_decay_fwd, ret_decay_bwd, ret_gn_w, conv_w, conv_b, conv_ln_w, conv_ln_b, w_out,
           router_w, router_b, w1, b1, w2, b2):
    assert ada_w.shape[0] == 1, "single-layer stack only"
    b, t, d = x.shape
    n_ctx = ctx.shape[1]
    n_heads = ret_decay_fwd.shape[1]
    d_ret = n_heads * HEAD_DIM
    n_exp = router_w.shape[2]
    assert b < MOD_ROWS and ret_gn_w.shape[1] == d_ret and d % (2 * LANES) == 0

    cc = jnp.zeros((MOD_ROWS, d), F32).at[:b].set(c).at[b].set(c_ctx)
    mod3 = _ada(cc, ada_w[0], ada_b)[:, None, :]
    cos, sin = _rope_tables(n_ctx + t)

    w_in_bf = w_in[0].astype(BF16)
    proj = _in_proj(x, pre_mix_norm, mod3, lambda bi: bi, w_in_bf, 0, w_in.shape[2])
    ctx_kv = _in_proj(ctx, pre_mix_norm, mod3, lambda bi: b, w_in_bf, d_ret, 2 * d_ret)

    lane_bcast = lambda v: jnp.broadcast_to(v.reshape(n_heads, 1, 1), (n_heads, 1, LANES))
    y_ret = _retention(proj, ctx_kv, cos, sin, lane_bcast(ret_decay_fwd[0]), lane_bcast(ret_decay_bwd[0]),
                       ret_gn_w, n_heads)
    y_conv = _conv(proj, 4 * d_ret, conv_w[0], conv_b, conv_ln_w, conv_ln_b)

    e_rows = -(-n_exp // SUBLANES) * SUBLANES
    rw_t = jnp.zeros((e_rows, d), F32).at[:n_exp].set(router_w[0].T)
    rw_hi = rw_t.astype(BF16)
    rw_lo = (rw_t - rw_hi.astype(F32)).astype(BF16)
    rb_col = jnp.zeros((e_rows, 1), F32).at[:n_exp, 0].set(router_b[0])
    x1, h2p, logits_t = _out_proj(y_ret, y_conv, w_out[0].astype(BF16), x, mod3, post_mix_norm, pre_ffn_norm,
                                  rw_hi, rw_lo, rb_col)

    out = _moe(h2p, logits_t, x1.reshape(b * t, d), mod3, post_ffn_norm, w1[0], b1[0], w2[0], b2[0], t, n_exp)
    return out.reshape(b, t, d)
```

```python
import functools
import math

import jax
import jax.numpy as jnp
from jax import lax
from jax.experimental import pallas as pl
from jax.experimental.pallas import tpu as pltpu

F32 = jnp.float32
BF16 = jnp.bfloat16
U32 = jnp.uint32
HIGHEST = lax.Precision.HIGHEST

GRID_W = 64
HEAD_DIM = 128
RET_CHUNK = 256
ROPE_BASE = 10000.0
TOP_K = 4
SWIGLU_ALPHA = 1.702
SWIGLU_LIMIT = 7.0
EPS = 1e-6
GN_EPS = 1e-5
LANES = 128
SUBLANES = 8
MOD_ROWS = 16
VMEM_LIMIT = 56 * 1024 * 1024
HI16 = 0xFFFF0000
ROW_GROUPS = 1


def _tile(n, pref):
    t = min(n, pref)
    while n % t:
        t -= 1
    return t


def _params(sem, vmem=VMEM_LIMIT):
    return pltpu.CompilerParams(dimension_semantics=sem, vmem_limit_bytes=vmem)


def _rms(x, w):
    return x * lax.rsqrt(jnp.mean(x * x, axis=-1, keepdims=True) + EPS) * w


def _silu(x):
    return x * jax.nn.sigmoid(x)


def _dot_nt(a, b):
    return lax.dot_general(a, b, (((1,), (1,)), ((), ())), preferred_element_type=F32)


def _dot_tn(a, b):
    return lax.dot_general(a, b, (((0,), (0,)), ((), ())), preferred_element_type=F32)


def _ada_kernel(c_ref, w_ref, b_ref, o_ref):
    s = _silu(c_ref[...])
    o_ref[...] = jnp.dot(s, w_ref[...], preferred_element_type=F32, precision=HIGHEST) + b_ref[...]


def _ada(cc, w, b):
    d, n = w.shape
    tn = _tile(n, 1024)
    return pl.pallas_call(
        _ada_kernel,
        out_shape=jax.ShapeDtypeStruct((MOD_ROWS, n), F32),
        grid=(n // tn,),
        in_specs=[pl.BlockSpec((MOD_ROWS, d), lambda j: (0, 0)),
                  pl.BlockSpec((d, tn), lambda j: (0, j)),
                  pl.BlockSpec((1, tn), lambda j: (0, j))],
        out_specs=pl.BlockSpec((MOD_ROWS, tn), lambda j: (0, j)),
        compiler_params=_params(("parallel",)),
        name="ada",
    )(cc, w, b)


def _rope_kernel(cos_ref, sin_ref):
    p, _ = cos_ref.shape
    half = HEAD_DIM // 2
    lane = lax.broadcasted_iota(jnp.int32, (p, HEAD_DIM), 1)
    pos = lax.broadcasted_iota(jnp.int32, (p, HEAD_DIM), 0).astype(F32)
    j = jnp.where(lane < half, lane, lane - half).astype(F32)
    inv = jnp.exp(j * (-jnp.log(ROPE_BASE) / half))
    ang = pos * inv
    cos_ref[...] = jnp.cos(ang)
    sin_ref[...] = jnp.where(lane < half, -1.0, 1.0) * jnp.sin(ang)


def _rope_tables(p):
    return pl.pallas_call(
        _rope_kernel,
        out_shape=(jax.ShapeDtypeStruct((p, HEAD_DIM), F32), jax.ShapeDtypeStruct((p, HEAD_DIM), F32)),
        name="rope",
    )()


def _rot(t, cos, sin_signed):
    return t * cos + pltpu.roll(t, HEAD_DIM // 2, axis=1) * sin_signed


def _inproj_kernel(x_ref, nw_ref, sh_ref, sc_ref, w_ref, o_ref, h_scr):
    @pl.when(pl.program_id(2) == 0)
    def _():
        rows = math.gcd(x_ref.shape[1], LANES)

        def body(r, carry):
            sl = pl.ds(pl.multiple_of(r * rows, rows), rows)
            h = _rms(x_ref[0, sl, :], nw_ref[...]) * (1.0 + sc_ref[0]) + sh_ref[0]
            h_scr[sl, :] = h.astype(BF16)
            return carry
        lax.fori_loop(0, x_ref.shape[1] // rows, body, 0)

    o_ref[0] = jnp.dot(h_scr[...], w_ref[...], preferred_element_type=F32)


def _in_proj(x, norm_w, mod3, mod_row, w_bf, col0, ncols):
    b, t, d = x.shape
    tm = _tile(t, 1024)
    tn = _tile(math.gcd(ncols, col0), 1536)
    joff = col0 // tn
    return pl.pallas_call(
        _inproj_kernel,
        out_shape=jax.ShapeDtypeStruct((b, t, ncols), F32),
        grid=(b, t // tm, ncols // tn),
        in_specs=[pl.BlockSpec((1, tm, d), lambda bi, i, j: (bi, i, 0)),
                  pl.BlockSpec((1, d), lambda bi, i, j: (0, 0)),
                  pl.BlockSpec((1, 1, d), lambda bi, i, j: (mod_row(bi), 0, 0)),
                  pl.BlockSpec((1, 1, d), lambda bi, i, j: (mod_row(bi), 0, 1)),
                  pl.BlockSpec((d, tn), lambda bi, i, j: (0, j + joff))],
        out_specs=pl.BlockSpec((1, tm, tn), lambda bi, i, j: (bi, i, j)),
        scratch_shapes=[pltpu.VMEM((tm, d), BF16)],
        compiler_params=_params(("parallel", "parallel", "arbitrary")),
        name="in_proj",
    )(x, norm_w, mod3, mod3, w_bf)


def _log_sigmoid(x):
    return jnp.minimum(x, 0.0) - jnp.log(1.0 + jnp.exp(-jnp.abs(x)))


def _ret_kernel(q_ref, k_ref, v_ref, g_ref, kc_ref, vc_ref, cos_ref, sin_ref, decf_ref, decb_ref,
                gnw_ref, o_ref, q_scr, k_scr, kf_scr, kb_scr, v_scr, o_scr, din_scr, kvf_scr, kvb_scr,
                rf_scr, rb_scr, *, n_ctx):
    t = q_ref.shape[1]
    c = din_scr.shape[0]
    nc = t // c
    lgf = _log_sigmoid(decf_ref[0])[:, 0:1]
    lgb = _log_sigmoid(decb_ref[0])[:, 0:1]

    kc = _rot(kc_ref[0], cos_ref[0:n_ctx, :], sin_ref[0:n_ctx, :])
    vc = vc_ref[0].astype(BF16)
    tc = lax.broadcasted_iota(jnp.int32, (n_ctx, 1), 0).astype(F32)
    rf = _dot_tn((kc * jnp.exp(lgf * (n_ctx - 1.0 - tc))).astype(BF16), vc)
    rb = _dot_tn((kc * jnp.exp(lgb * tc)).astype(BF16), vc)

    scale = HEAD_DIM ** -0.5
    cos, sin = cos_ref[n_ctx:n_ctx + t, :], sin_ref[n_ctx:n_ctx + t, :]
    q_scr[...] = (_rot(q_ref[0], cos, sin) * scale).astype(BF16)
    kr = _rot(k_ref[0], cos, sin)
    ri_all = (lax.broadcasted_iota(jnp.int32, (t, HEAD_DIM), 0) % c).astype(F32)
    k_scr[...] = kr.astype(BF16)
    kf_scr[...] = (kr * jnp.exp(lgf * (c - 1.0 - ri_all))).astype(BF16)
    kb_scr[...] = (kr * jnp.exp(lgb * ri_all)).astype(BF16)
    v_scr[...] = v_ref[0].astype(BF16)

    diff = (lax.broadcasted_iota(jnp.int32, (c, c), 0) - lax.broadcasted_iota(jnp.int32, (c, c), 1)).astype(F32)
    din_scr[...] = jnp.where(diff > 0, jnp.exp(lgf * jnp.maximum(diff, 0.0)),
                             jnp.where(diff < 0, jnp.exp(lgb * jnp.maximum(-diff, 0.0)), 2.0))

    for i in range(nc):
        sl = slice(i * c, (i + 1) * c)
        vi = v_scr[sl, :]
        scores = _dot_nt(q_scr[sl, :], k_scr[sl, :]) * din_scr[...]
        o_scr[sl, :] = jnp.dot(scores.astype(BF16), vi, preferred_element_type=F32)
        kvf_scr[i] = _dot_tn(kf_scr[sl, :], vi)
        kvb_scr[i] = _dot_tn(kb_scr[sl, :], vi)

    gc_f = jnp.exp(lgf * float(c))
    gc_b = jnp.exp(lgb * float(c))
    for i in range(nc):
        rf_scr[i] = rf.astype(BF16)
        rf = rf * gc_f + kvf_scr[i]
    for i in reversed(range(nc)):
        rb_scr[i] = rb.astype(BF16)
        rb = rb * gc_b + kvb_scr[i]

    ri = lax.broadcasted_iota(jnp.int32, (c, HEAD_DIM), 0).astype(F32)
    dq_f = jnp.exp(lgf * (ri + 1.0))
    dq_b = jnp.exp(lgb * (c - ri))
    for i in range(nc):
        sl = slice(i * c, (i + 1) * c)
        qi = q_scr[sl, :]
        o = (o_scr[sl, :] + jnp.dot(qi, rf_scr[i], preferred_element_type=F32) * dq_f
             + jnp.dot(qi, rb_scr[i], preferred_element_type=F32) * dq_b)
        mu = jnp.mean(o, axis=-1, keepdims=True)
        var = jnp.mean(jnp.square(o - mu), axis=-1, keepdims=True)
        on = (o - mu) * lax.rsqrt(var + GN_EPS) * gnw_ref[...]
        o_ref[0, sl, :] = (on * _silu(g_ref[0, sl, :])).astype(o_ref.dtype)


def _retention(proj, ctx_kv, cos, sin, decf3, decb3, gn_w, n_heads):
    b, t, _ = proj.shape
    n_ctx = ctx_kv.shape[1]
    hd = HEAD_DIM
    c = _tile(t, RET_CHUNK)
    nc = t // c
    lat = lambda g: pl.BlockSpec((1, t, hd), lambda bi, h: (bi, 0, g * n_heads + h))
    seq = lambda dt: pltpu.VMEM((t, hd), dt)
    return pl.pallas_call(
        functools.partial(_ret_kernel, n_ctx=n_ctx),
        out_shape=jax.ShapeDtypeStruct((b, t, n_heads * hd), BF16),
        grid=(b, n_heads),
        in_specs=[lat(0), lat(1), lat(2), lat(3),
                  pl.BlockSpec((1, n_ctx, hd), lambda bi, h: (bi, 0, h)),
                  pl.BlockSpec((1, n_ctx, hd), lambda bi, h: (bi, 0, n_heads + h)),
                  pl.BlockSpec(cos.shape, lambda bi, h: (0, 0)),
                  pl.BlockSpec(sin.shape, lambda bi, h: (0, 0)),
                  pl.BlockSpec((1, 1, LANES), lambda bi, h: (h, 0, 0)),
                  pl.BlockSpec((1, 1, LANES), lambda bi, h: (h, 0, 0)),
                  pl.BlockSpec((1, hd), lambda bi, h: (0, h))],
        out_specs=pl.BlockSpec((1, t, hd), lambda bi, h: (bi, 0, h)),
        scratch_shapes=[seq(BF16), seq(BF16), seq(BF16), seq(BF16), seq(BF16), seq(F32),
                        pltpu.VMEM((c, c), F32),
                        pltpu.VMEM((nc, hd, hd), F32), pltpu.VMEM((nc, hd, hd), F32),
                        pltpu.VMEM((nc, hd, hd), BF16), pltpu.VMEM((nc, hd, hd), BF16)],
        compiler_params=_params(("parallel", "parallel")),
        name="ret",
    )(proj, proj, proj, proj, ctx_kv, ctx_kv, cos, sin, decf3, decb3, gn_w)


CONV_PAD_ROWS = 16
CONV_LANE_CHUNK = 256


def _conv_kernel(a_ref, b_ref, w_ref, cb_ref, lnw_ref, lnb_ref, o_ref, up_scr, sh_scr, y_scr):
    tt, ch = a_ref.shape[1], a_ref.shape[2]
    kw = w_ref.shape[0]
    n_seq = tt // GRID_W
    lead = CONV_PAD_ROWS - kw // 2
    rows = GRID_W + 2 * CONV_PAD_ROWS
    u = a_ref[0] * jax.nn.sigmoid(b_ref[0])
    zeros = jnp.zeros((CONV_PAD_ROWS, ch), F32)
    for s in range(n_seq):
        up_scr[s, 0:CONV_PAD_ROWS, :] = zeros
        up_scr[s, CONV_PAD_ROWS:CONV_PAD_ROWS + GRID_W, :] = u[s * GRID_W:(s + 1) * GRID_W, :]
        up_scr[s, CONV_PAD_ROWS + GRID_W:, :] = zeros

    cw = sh_scr.shape[2]

    def seq(s, carry):
        row0 = pl.multiple_of(s * GRID_W, GRID_W)
        for c0 in range(0, ch, cw):
            for r in range(SUBLANES):
                sh_scr[r] = up_scr[s, r:r + rows - SUBLANES, c0:c0 + cw]
            acc = jnp.broadcast_to(cb_ref[:, c0:c0 + cw], (GRID_W, cw))
            for k in range(kw):
                a8, r = divmod(lead + k, SUBLANES)
                acc = acc + sh_scr[r, a8 * SUBLANES:a8 * SUBLANES + GRID_W, :] * w_ref[k:k + 1, c0:c0 + cw]
            y_scr[pl.ds(row0, GRID_W), c0:c0 + cw] = acc
        return carry

    lax.fori_loop(0, n_seq, seq, 0)
    y = y_scr[...]
    mu = jnp.mean(y, axis=-1, keepdims=True)
    var = jnp.mean(jnp.square(y - mu), axis=-1, keepdims=True)
    yn = (y - mu) * lax.rsqrt(var + EPS) * lnw_ref[...] + lnb_ref[...]
    o_ref[0] = _silu(yn).astype(o_ref.dtype)


def _conv(proj, col0, conv_w, conv_b, ln_w, ln_b):
    b, t, _ = proj.shape
    kw, ch = conv_w.shape
    assert col0 % ch == 0 and kw // 2 <= CONV_PAD_ROWS and t % GRID_W == 0
    tt = _tile(t, 4 * GRID_W)
    cw = min(ch, CONV_LANE_CHUNK)
    rows = GRID_W + 2 * CONV_PAD_ROWS
    ca, cb = col0 // ch, col0 // ch + 1
    vec = pl.BlockSpec((1, ch), lambda bi, i: (0, 0))
    return pl.pallas_call(
        _conv_kernel,
        out_shape=jax.ShapeDtypeStruct((b, t, ch), BF16),
        grid=(b, t // tt),
        in_specs=[pl.BlockSpec((1, tt, ch), lambda bi, i: (bi, i, ca)),
                  pl.BlockSpec((1, tt, ch), lambda bi, i: (bi, i, cb)),
                  pl.BlockSpec((kw, ch), lambda bi, i: (0, 0)),
                  vec, vec, vec],
        out_specs=pl.BlockSpec((1, tt, ch), lambda bi, i: (bi, i, 0)),
        scratch_shapes=[pltpu.VMEM((tt // GRID_W, rows, ch), F32),
                        pltpu.VMEM((SUBLANES, rows - SUBLANES, cw), F32),
                        pltpu.VMEM((tt, ch), F32)],
        compiler_params=_params(("parallel", "parallel")),
        name="conv",
    )(proj, proj, conv_w, conv_b, ln_w, ln_b)


def _outproj_kernel(yr_ref, yc_ref, wr_ref, wc_ref, x_ref, g1_ref, sh2_ref, sc2_ref, pmn_ref, pfn_ref,
                    rwh_ref, rwl_ref, rb_ref, x1_ref, h2p_ref, lg_ref):
    tm = x_ref.shape[1]
    mix = (jnp.dot(yr_ref[0], wr_ref[...], preferred_element_type=F32)
           + jnp.dot(yc_ref[0], wc_ref[...], preferred_element_type=F32))
    x1 = x_ref[0] + g1_ref[0] * _rms(mix, pmn_ref[...])
    x1_ref[0] = x1
    h2 = _rms(x1, pfn_ref[...]) * (1.0 + sc2_ref[0]) + sh2_ref[0]
    h2_hi = h2.astype(BF16)
    h2_lo = (h2 - h2_hi.astype(F32)).astype(BF16)
    lg_ref[...] = (_dot_nt(rwh_ref[...], h2_hi) + _dot_nt(rwh_ref[...], h2_lo)
                   + _dot_nt(rwl_ref[...], h2_hi) + rb_ref[...])
    _pack_rows(h2, h2p_ref, tm)


def _out_proj(y_ret, y_conv, w_out_bf, x, mod3, post_mix_norm, pre_ffn_norm, rw_hi, rw_lo, rb_col):
    b, t, d = x.shape
    d_ret, d_conv = y_ret.shape[2], y_conv.shape[2]
    e_rows = rw_hi.shape[0]
    seg = d // 2 // LANES
    tm = _tile(t, 512)
    per_b = t // tm
    mod = lambda col: pl.BlockSpec((1, 1, d), lambda bi, i: (bi, 0, col))
    vec = pl.BlockSpec((1, d), lambda bi, i: (0, 0))
    row = lambda width: pl.BlockSpec((1, tm, width), lambda bi, i: (bi, i, 0))
    rw = pl.BlockSpec((e_rows, d), lambda bi, i: (0, 0))
    w_ret, w_conv = w_out_bf[:d_ret], w_out_bf[d_ret:]
    return pl.pallas_call(
        _outproj_kernel,
        out_shape=(jax.ShapeDtypeStruct((b, t, d), F32),
                   jax.ShapeDtypeStruct((b * t * seg, LANES), U32),
                   jax.ShapeDtypeStruct((e_rows, b * t), F32)),
        grid=(b, per_b),
        in_specs=[row(d_ret), row(d_conv),
                  pl.BlockSpec((d_ret, d), lambda bi, i: (0, 0)),
                  pl.BlockSpec((d_conv, d), lambda bi, i: (0, 0)),
                  row(d), mod(2), mod(3), mod(4), vec, vec, rw, rw,
                  pl.BlockSpec((e_rows, 1), lambda bi, i: (0, 0))],
        out_specs=(row(d),
                   pl.BlockSpec((tm * seg, LANES), lambda bi, i: (bi * per_b + i, 0)),
                   pl.BlockSpec((e_rows, tm), lambda bi, i: (0, bi * per_b + i))),
        compiler_params=_params(("parallel", "parallel")),
        name="out_proj",
    )(y_ret, y_conv, w_ret, w_conv, x, mod3, mod3, mod3, post_mix_norm, pre_ffn_norm, rw_hi, rw_lo, rb_col)


def _route_kernel(lg_ref, idx_ref, gate_ref, rank_ref, cnt_ref, tri_scr, run_scr, *, n_experts):
    e_rows, tr = lg_ref.shape

    @pl.when(pl.program_id(0) == 0)
    def _():
        r = lax.broadcasted_iota(jnp.int32, (tr, tr), 0)
        c = lax.broadcasted_iota(jnp.int32, (tr, tr), 1)
        tri_scr[...] = jnp.where(r <= c, 1.0, 0.0).astype(BF16)
        run_scr[...] = jnp.zeros_like(run_scr)

    e_iota = lax.broadcasted_iota(jnp.int32, (e_rows, tr), 0)
    neg = jnp.float32(-jnp.inf)
    logits = jnp.where(e_iota < n_experts, lg_ref[...], neg)
    vals, hots = [], []
    for k in range(TOP_K):
        m = jnp.max(logits, axis=0, keepdims=True)
        ik = jnp.min(jnp.where(logits == m, e_iota, e_rows), axis=0, keepdims=True)
        hot = e_iota == ik
        logits = jnp.where(hot, neg, logits)
        vals.append(m)
        hots.append(hot)
        idx_ref[k:k + 1, :] = ik
    exps = [jnp.exp(v - vals[0]) for v in vals]
    den = exps[0]
    for e in exps[1:]:
        den = den + e
    for k in range(TOP_K):
        gate_ref[k:k + 1, :] = exps[k] / den

    sel = jnp.zeros((e_rows, tr), F32)
    for hot in hots:
        sel = sel + jnp.where(hot, 1.0, 0.0)
    csum = jnp.dot(sel.astype(BF16), tri_scr[...], preferred_element_type=F32)
    before = run_scr[:, 0:1] + csum - sel
    for k in range(TOP_K):
        rank_ref[k:k + 1, :] = jnp.sum(jnp.where(hots[k], before, 0.0), axis=0, keepdims=True).astype(jnp.int32)
    run_scr[...] = run_scr[...] + jnp.sum(sel, axis=1, keepdims=True)
    cnt_ref[...] = run_scr[...]


def _route(logits_t, n_experts):
    e_rows, n = logits_t.shape
    tr = _tile(n, 512)
    kt = lambda dt: jax.ShapeDtypeStruct((TOP_K, n), dt)
    blk = pl.BlockSpec((TOP_K, tr), lambda i: (0, i))
    return pl.pallas_call(
        functools.partial(_route_kernel, n_experts=n_experts),
        out_shape=(kt(jnp.int32), kt(F32), kt(jnp.int32), jax.ShapeDtypeStruct((e_rows, LANES), F32)),
        grid=(n // tr,),
        in_specs=[pl.BlockSpec((e_rows, tr), lambda i: (0, i))],
        out_specs=(blk, blk, blk, pl.BlockSpec((e_rows, LANES), lambda i: (0, 0))),
        scratch_shapes=[pltpu.VMEM((tr, tr), BF16), pltpu.VMEM((e_rows, LANES), F32)],
        compiler_params=_params(("arbitrary",)),
        name="route",
    )(logits_t)


def _dispatch_kernel(pos_ref, poff_ref, plen_ref, h_ref, xs_ref, zero_scr, sem, zsem, *, n_tok, n_exp, seg, tm):
    td = h_ref.shape[0] // seg
    step = pl.program_id(0)
    base = step * td

    def pad_copies(e, act):
        off, ln = poff_ref[e], plen_ref[e]
        for bit in range(tm.bit_length() - 1):
            size = 1 << bit

            @pl.when(((ln >> bit) & 1) == 1)
            def _():
                row = pl.multiple_of((off + (ln & (size - 1))) * seg, seg)
                act(pltpu.make_async_copy(zero_scr.at[pl.ds(0, size * seg)],
                                          xs_ref.at[pl.ds(row, size * seg)], zsem))

    def for_pads(act):
        def body(e, carry):
            pad_copies(e, act)
            return carry
        lax.fori_loop(0, n_exp, body, 0)

        half = zero_scr.shape[0]
        end = (poff_ref[n_exp - 1] + plen_ref[n_exp - 1]) * seg

        def tail(n, carry):
            row = pl.multiple_of(end + n * half, seg)
            act(pltpu.make_async_copy(zero_scr, xs_ref.at[pl.ds(row, half)], zsem))
            return carry
        lax.fori_loop(0, (xs_ref.shape[0] - end) // half, tail, 0)

    @pl.when(step == 0)
    def _():
        zero_scr[...] = jnp.zeros_like(zero_scr)
        for_pads(lambda cp: cp.start())

    def row_copy(t, k):
        p = pos_ref[k * n_tok + base + t]
        return pltpu.make_async_copy(h_ref.at[pl.ds(pl.multiple_of(t * seg, seg), seg)],
                                     xs_ref.at[pl.ds(pl.multiple_of(p * seg, seg), seg)], sem)

    def start(t, carry):
        for k in range(TOP_K):
            row_copy(t, k).start()
        return carry

    lax.fori_loop(0, td, start, 0)
    for k in range(TOP_K):
        pltpu.make_async_copy(h_ref, h_ref, sem).wait()

    @pl.when(step == 0)
    def _():
        for_pads(lambda cp: cp.wait())


def _dispatch(pos_flat, pad_off, pad_len, h2p, n_slots, seg, tm):
    n = h2p.shape[0] // seg
    n_exp = pad_off.shape[0]
    assert tm & (tm - 1) == 0
    td = _tile(n, 512)
    return pl.pallas_call(
        functools.partial(_dispatch_kernel, n_tok=n, n_exp=n_exp, seg=seg, tm=tm),
        out_shape=jax.ShapeDtypeStruct((n_slots * seg, LANES), U32),
        grid_spec=pltpu.PrefetchScalarGridSpec(
            num_scalar_prefetch=3,
            grid=(n // td,),
            in_specs=[pl.BlockSpec((td * seg, LANES), lambda i, *_: (i, 0))],
            out_specs=pl.BlockSpec(memory_space=pl.ANY),
            scratch_shapes=[pltpu.VMEM((max(tm // 2, 1) * seg, LANES), U32),
                            pltpu.SemaphoreType.DMA(()), pltpu.SemaphoreType.DMA(())]),
        compiler_params=_params(("arbitrary",)),
        name="dispatch",
    )(pos_flat, pad_off, pad_len, h2p)


def _pack_rows(val, ref, tm):
    half = val.shape[1] // 2
    seg = half // LANES
    vb = val.astype(BF16).astype(F32)
    packed = ((lax.bitcast_convert_type(vb[:, :half], U32) >> 16)
              | (lax.bitcast_convert_type(vb[:, half:], U32) & jnp.uint32(HI16)))
    for s in range(seg):
        ref[pl.ds(s, tm, stride=seg), :] = packed[:, s * LANES:(s + 1) * LANES]


def _unpack_seg(words):
    return (lax.bitcast_convert_type(words << 16, F32),
            lax.bitcast_convert_type(words & jnp.uint32(HI16), F32))


def _weight_runs(tile_expert, n_passes):
    n_tiles = tile_expert.shape[0]
    total = n_tiles * n_passes
    e_lin = jnp.tile(tile_expert, n_passes)
    j_lin = jnp.repeat(jnp.arange(n_passes, dtype=jnp.int32), n_tiles)
    i_lin = jnp.tile(jnp.arange(n_tiles, dtype=jnp.int32), n_passes)
    start = (i_lin == 0) | (e_lin != jnp.roll(e_lin, 1))
    idx = jnp.arange(total, dtype=jnp.int32)
    nxt = jnp.min(jnp.where(start[None, :] & (idx[None, :] > idx[:, None]), idx[None, :], total), axis=1)
    has = nxt < total
    nxt = jnp.minimum(nxt, total - 1)
    return start.astype(jnp.int32), jnp.where(has, e_lin[nxt], -1).astype(jnp.int32), j_lin[nxt]


def _stream_weights(step, first_ref, ne_ref, nj_ref, cur, copies, cast):
    @pl.when(step == 0)
    def _():
        for cp in copies(*cur):
            cp.start()

    @pl.when(first_ref[step] == 1)
    def _():
        for cp in copies(*cur):
            cp.wait()
        cast()

        @pl.when(ne_ref[step] >= 0)
        def _():
            for cp in copies(ne_ref[step], nj_ref[step]):
                cp.start()


def _ffn1_kernel(te_ref, nu_ref, first_ref, ne_ref, nj_ref, tv_ref, xs_ref, w_hbm, bg_ref, bl_ref, act_ref,
                 wg_stage, wl_stage, wg_scr, wl_scr, x_scr, sem, *, seg, d_ff):
    del nu_ref
    j, i = pl.program_id(0), pl.program_id(1)
    tm = x_scr.shape[0]
    tn = wg_scr.shape[1]
    half = seg * LANES

    def copies(e, jj):
        col = pl.multiple_of(jj * tn, tn)
        return (pltpu.make_async_copy(w_hbm.at[e, :, pl.ds(col, tn)], wg_stage, sem.at[0]),
                pltpu.make_async_copy(w_hbm.at[e, :, pl.ds(d_ff + col, tn)], wl_stage, sem.at[1]))

    def cast():
        wg_scr[...] = wg_stage[...].astype(BF16)
        wl_scr[...] = wl_stage[...].astype(BF16)

    _stream_weights(j * pl.num_programs(1) + i, first_ref, ne_ref, nj_ref, (te_ref[i], j), copies, cast)

    rows = tm // ROW_GROUPS
    for r0 in range(0, tm, rows):
        @pl.when(tv_ref[i] > r0)
        def _():
            for s in range(seg):
                lo, hi = _unpack_seg(xs_ref[pl.ds(r0 * seg + s, rows, stride=seg), :])
                x_scr[r0:r0 + rows, s * LANES:(s + 1) * LANES] = lo.astype(BF16)
                x_scr[r0:r0 + rows, half + s * LANES:half + (s + 1) * LANES] = hi.astype(BF16)
            x = x_scr[r0:r0 + rows, :]
            glu = jnp.minimum(jnp.dot(x, wg_scr[...], preferred_element_type=F32) + bg_ref[0], SWIGLU_LIMIT)
            lin = jnp.clip(jnp.dot(x, wl_scr[...], preferred_element_type=F32) + bl_ref[0],
                           -SWIGLU_LIMIT, SWIGLU_LIMIT)
            act_ref[r0:r0 + rows, :] = (glu * jax.nn.sigmoid(SWIGLU_ALPHA * glu) * (lin + 1.0)).astype(act_ref.dtype)

        @pl.when(tv_ref[i] <= r0)
        def _():
            act_ref[r0:r0 + rows, :] = jnp.zeros((rows, tn), act_ref.dtype)


def _ffn1(tile_expert, n_used, tile_valid, xs, w1, b1_3, tm, seg):
    d = seg * LANES * 2
    slots = xs.shape[0] // seg
    d_ff = w1.shape[2] // 2
    tn = _tile(d_ff, 1024)
    nj = d_ff // tn
    n_tiles = slots // tm
    row = lambda i, nu: jnp.minimum(i, nu[0] - 1)
    return pl.pallas_call(
        functools.partial(_ffn1_kernel, seg=seg, d_ff=d_ff),
        out_shape=jax.ShapeDtypeStruct((slots, d_ff), BF16),
        grid_spec=pltpu.PrefetchScalarGridSpec(
            num_scalar_prefetch=6,
            grid=(nj, n_tiles),
            in_specs=[pl.BlockSpec((tm * seg, LANES), lambda j, i, te, nu, *_: (row(i, nu), 0)),
                      pl.BlockSpec(memory_space=pl.ANY),
                      pl.BlockSpec((1, 1, tn), lambda j, i, te, *_: (te[i], 0, j)),
                      pl.BlockSpec((1, 1, tn), lambda j, i, te, *_: (te[i], 0, nj + j))],
            out_specs=pl.BlockSpec((tm, tn), lambda j, i, *_: (i, j)),
            scratch_shapes=[pltpu.VMEM((d, tn), F32), pltpu.VMEM((d, tn), F32),
                            pltpu.VMEM((d, tn), BF16), pltpu.VMEM((d, tn), BF16),
                            pltpu.VMEM((tm, d), BF16), pltpu.SemaphoreType.DMA((2,))]),
        compiler_params=_params(("arbitrary", "arbitrary")),
        name="ffn1",
    )(tile_expert, n_used, *_weight_runs(tile_expert, nj), tile_valid, xs, w1, b1_3, b1_3)


def _ffn2_kernel(te_ref, nu_ref, first_ref, ne_ref, nj_ref, tv_ref, act_ref, w_hbm, b_ref, ys_ref,
                 w_stage, w_scr, sem):
    del nu_ref
    i = pl.program_id(0)
    tm = act_ref.shape[0]

    def copies(e, jj):
        del jj
        return (pltpu.make_async_copy(w_hbm.at[e], w_stage, sem),)

    def cast():
        w_scr[...] = w_stage[...].astype(BF16)

    _stream_weights(i, first_ref, ne_ref, nj_ref, (te_ref[i], 0), copies, cast)

    rows = tm // ROW_GROUPS
    seg = ys_ref.shape[0] // tm
    for r0 in range(0, tm, rows):
        ys_rows = ys_ref.at[pl.ds(r0 * seg, rows * seg)]

        @pl.when(tv_ref[i] > r0)
        def _():
            y = jnp.dot(act_ref[r0:r0 + rows, :], w_scr[...], preferred_element_type=F32) + b_ref[0]
            _pack_rows(y, ys_rows, rows)

        @pl.when(tv_ref[i] <= r0)
        def _():
            ys_rows[...] = jnp.zeros(ys_rows.shape, ys_rows.dtype)


def _ffn2(tile_expert, n_used, tile_valid, act, w2, b2_3, tm):
    slots, d_ff = act.shape
    d = w2.shape[2]
    seg = d // 2 // LANES
    n_tiles = slots // tm
    row = lambda i, nu: jnp.minimum(i, nu[0] - 1)
    return pl.pallas_call(
        _ffn2_kernel,
        out_shape=jax.ShapeDtypeStruct((slots * seg, LANES), U32),
        grid_spec=pltpu.PrefetchScalarGridSpec(
            num_scalar_prefetch=6,
            grid=(n_tiles,),
            in_specs=[pl.BlockSpec((tm, d_ff), lambda i, te, nu, *_: (row(i, nu), 0)),
                      pl.BlockSpec(memory_space=pl.ANY),
                      pl.BlockSpec((1, 1, d), lambda i, te, *_: (te[i], 0, 0))],
            out_specs=pl.BlockSpec((tm * seg, LANES), lambda i, *_: (i, 0)),
            scratch_shapes=[pltpu.VMEM((d_ff, d), F32), pltpu.VMEM((d_ff, d), BF16),
                            pltpu.SemaphoreType.DMA(())]),
        compiler_params=_params(("arbitrary",)),
        name="ffn2",
    )(tile_expert, n_used, *_weight_runs(tile_expert, 1), tile_valid, act, w2, b2_3)


def _combine_kernel(pos_ref, ys_ref, gate_ref, x1_ref, g2_ref, pfn_ref, o_ref, buf, moe_scr, sem, *, n_tok):
    tc, d = x1_ref.shape
    seg = buf.shape[2] // tc
    half = seg * LANES
    step = pl.program_id(0)

    def gather(st, slot):
        def body(t, carry):
            for k in range(TOP_K):
                p = pos_ref[k * n_tok + st * tc + t]
                pltpu.make_async_copy(ys_ref.at[pl.ds(pl.multiple_of(p * seg, seg), seg)],
                                      buf.at[slot, k, pl.ds(pl.multiple_of(t * seg, seg), seg)],
                                      sem.at[slot, k]).start()
            return carry
        lax.fori_loop(0, tc, body, 0)

    @pl.when(step == 0)
    def _():
        gather(0, 0)

    @pl.when(step + 1 < pl.num_programs(0))
    def _():
        gather(step + 1, (step + 1) % 2)

    slot = step % 2
    for k in range(TOP_K):
        pltpu.make_async_copy(buf.at[slot, k], buf.at[slot, k], sem.at[slot, k]).wait()
    gates = [jnp.broadcast_to(gate_ref[:, k:k + 1], (tc, LANES)) for k in range(TOP_K)]
    for s in range(seg):
        m_lo = m_hi = None
        for k in range(TOP_K):
            lo, hi = _unpack_seg(buf[slot, k, pl.ds(s, tc, stride=seg), :])
            m_lo = lo * gates[k] if m_lo is None else m_lo + lo * gates[k]
            m_hi = hi * gates[k] if m_hi is None else m_hi + hi * gates[k]
        moe_scr[:, s * LANES:(s + 1) * LANES] = m_lo
        moe_scr[:, half + s * LANES:half + (s + 1) * LANES] = m_hi
    o_ref[...] = x1_ref[...] + g2_ref[0] * _rms(moe_scr[...], pfn_ref[...])


def _combine(pos_flat, ys, gates_t, x1, mod3, post_ffn_norm, seq):
    n, d = x1.shape
    seg = d // 2 // LANES
    tc = _tile(seq, 256)
    per_batch = seq // tc
    return pl.pallas_call(
        functools.partial(_combine_kernel, n_tok=n),
        out_shape=jax.ShapeDtypeStruct((n, d), F32),
        grid_spec=pltpu.PrefetchScalarGridSpec(
            num_scalar_prefetch=1,
            grid=(n // tc,),
            in_specs=[pl.BlockSpec(memory_space=pl.ANY),
                      pl.BlockSpec((tc, TOP_K), lambda i, pos: (i, 0)),
                      pl.BlockSpec((tc, d), lambda i, pos: (i, 0)),
                      pl.BlockSpec((1, 1, d), lambda i, pos: (i // per_batch, 0, 5)),
                      pl.BlockSpec((1, d), lambda i, pos: (0, 0))],
            out_specs=pl.BlockSpec((tc, d), lambda i, pos: (i, 0)),
            scratch_shapes=[pltpu.VMEM((2, TOP_K, tc * seg, LANES), U32), pltpu.VMEM((tc, d), F32),
                            pltpu.SemaphoreType.DMA((2, TOP_K))]),
        compiler_params=_params(("arbitrary",)),
        name="combine",
    )(pos_flat, ys, gates_t, x1, mod3, post_ffn_norm)


def _moe(h2p, logits_t, x1, mod3, post_ffn_norm, w1, b1, w2, b2, seq, n_exp):
    n, d = x1.shape
    seg = d // 2 // LANES
    tm = 1 << (min(n * TOP_K, 512).bit_length() - 1)
    idx, gates, rank, cnt = _route(logits_t, n_exp)

    counts = cnt[:n_exp, 0].astype(jnp.int32)
    padded = (counts + tm - 1) // tm * tm
    ends = jnp.cumsum(padded)
    starts = ends - padded
    hot = idx[:, :, None] == jnp.arange(n_exp, dtype=jnp.int32)
    pos_flat = (jnp.sum(jnp.where(hot, starts, 0), axis=-1) + rank).reshape(-1)
    n_tiles = -(-n * TOP_K // tm) + n_exp
    n_used = (ends[-1] // tm).astype(jnp.int32)
    tile_start = jnp.minimum(jnp.arange(n_tiles, dtype=jnp.int32), n_used - 1) * tm
    tile_expert = jnp.sum(tile_start[:, None] >= ends[None, :], axis=1).astype(jnp.int32)
    tile_hot = tile_expert[:, None] == jnp.arange(n_exp, dtype=jnp.int32)
    tile_rows_left = jnp.sum(jnp.where(tile_hot, starts + counts, 0), axis=1) - tile_start
    tile_valid = jnp.where(jnp.arange(n_tiles) < n_used, jnp.clip(tile_rows_left, 0, tm), 0).astype(jnp.int32)
    n_used = n_used.reshape(1)

    xs = _dispatch(pos_flat, starts + counts, padded - counts, h2p, n_tiles * tm, seg, tm)
    act = _ffn1(tile_expert, n_used, tile_valid, xs, w1, b1[:, None, :], tm, seg)
    ys = _ffn2(tile_expert, n_used, tile_valid, act, w2, b2[:, None, :], tm)
    return _combine(pos_flat, ys, gates.T, x1, mod3, post_ffn_norm, seq)


def kernel(x, c, ctx, c_ctx, ada_w, ada_b, pre_mix_norm, post_mix_norm, pre_ffn_norm, post_ffn_norm,
           w_in, ret_decay_fwd, ret_decay_bwd, ret_gn_w, conv_w, conv_b, conv_ln_w, conv_ln_b, w_out,
           router_w, router_b, w1, b1, w2, b2):
    assert ada_w.shape[0] == 1, "single-layer stack only"
    b, t, d = x.shape
    n_ctx = ctx.shape[1]
    n_heads = ret_decay_fwd.shape[1]
    d_ret = n_heads * HEAD_DIM
    n_exp = router_w.shape[2]
    assert b < MOD_ROWS and ret_gn_w.shape[1] == d_ret and d % (2 * LANES) == 0

    cc = jnp.zeros((MOD_ROWS, d), F32).at[:b].set(c).at[b].set(c_ctx)
    mod3 = _ada(cc, ada_w[0], ada_b)[:, None, :]
    cos, sin = _rope_tables(n_ctx + t)

    w_in_bf = w_in[0].astype(BF16)
    proj = _in_proj(x, pre_mix_norm, mod3, lambda bi: bi, w_in_bf, 0, w_in.shape[2])
    ctx_kv = _in_proj(ctx, pre_mix_norm, mod3, lambda bi: b, w_in_bf, d_ret, 2 * d_ret)

    lane_bcast = lambda v: jnp.broadcast_to(v.reshape(n_heads, 1, 1), (n_heads, 1, LANES))
    y_ret = _retention(proj, ctx_kv, cos, sin, lane_bcast(ret_decay_fwd[0]), lane_bcast(ret_decay_bwd[0]),
                       ret_gn_w, n_heads)
    y_conv = _conv(proj, 4 * d_ret, conv_w[0], conv_b, conv_ln_w, conv_ln_b)

    e_rows = -(-n_exp // SUBLANES) * SUBLANES
    rw_t = jnp.zeros((e_rows, d), F32).at[:n_exp].set(router_w[0].T)
    rw_hi = rw_t.astype(BF16)
    rw_lo = (rw_t - rw_hi.astype(F32)).astype(BF16)
    rb_col = jnp.zeros((e_rows, 1), F32).at[:n_exp, 0].set(router_b[0])
    x1, h2p, logits_t = _out_proj(y_ret, y_conv, w_out[0].astype(BF16), x, mod3, post_mix_norm, pre_ffn_norm,
                                  rw_hi, rw_lo, rb_col)

    out = _moe(h2p, logits_t, x1.reshape(b * t, d), mod3, post_ffn_norm, w1[0], b1[0], w2[0], b2[0], t, n_exp)
    return out.reshape(b, t, d)
```

```python
import functools
import math

import jax
import jax.numpy as jnp
from jax import lax
from jax.experimental import pallas as pl
from jax.experimental.pallas import tpu as pltpu

F32 = jnp.float32
BF16 = jnp.bfloat16
U32 = jnp.uint32
HIGHEST = lax.Precision.HIGHEST

GRID_W = 64
HEAD_DIM = 128
RET_CHUNK = 256
ROPE_BASE = 10000.0
TOP_K = 4
SWIGLU_ALPHA = 1.702
SWIGLU_LIMIT = 7.0
EPS = 1e-6
GN_EPS = 1e-5
LANES = 128
SUBLANES = 8
MOD_ROWS = 16
VMEM_LIMIT = 56 * 1024 * 1024
HI16 = 0xFFFF0000
ROW_GROUPS = 1


def _tile(n, pref):
    t = min(n, pref)
    while n % t:
        t -= 1
    return t


def _params(sem, vmem=VMEM_LIMIT):
    return pltpu.CompilerParams(dimension_semantics=sem, vmem_limit_bytes=vmem)


def _rms(x, w):
    return x * lax.rsqrt(jnp.mean(x * x, axis=-1, keepdims=True) + EPS) * w


def _silu(x):
    return x * jax.nn.sigmoid(x)


def _dot_nt(a, b):
    return lax.dot_general(a, b, (((1,), (1,)), ((), ())), preferred_element_type=F32)


def _dot_tn(a, b):
    return lax.dot_general(a, b, (((0,), (0,)), ((), ())), preferred_element_type=F32)


def _ada_kernel(c_ref, w_ref, b_ref, o_ref):
    s = _silu(c_ref[...])
    o_ref[...] = jnp.dot(s, w_ref[...], preferred_element_type=F32, precision=HIGHEST) + b_ref[...]


def _ada(cc, w, b):
    d, n = w.shape
    tn = _tile(n, 1024)
    return pl.pallas_call(
        _ada_kernel,
        out_shape=jax.ShapeDtypeStruct((MOD_ROWS, n), F32),
        grid=(n // tn,),
        in_specs=[pl.BlockSpec((MOD_ROWS, d), lambda j: (0, 0)),
                  pl.BlockSpec((d, tn), lambda j: (0, j)),
                  pl.BlockSpec((1, tn), lambda j: (0, j))],
        out_specs=pl.BlockSpec((MOD_ROWS, tn), lambda j: (0, j)),
        compiler_params=_params(("parallel",)),
        name="ada",
    )(cc, w, b)


def _rope_kernel(cos_ref, sin_ref):
    p, _ = cos_ref.shape
    half = HEAD_DIM // 2
    lane = lax.broadcasted_iota(jnp.int32, (p, HEAD_DIM), 1)
    pos = lax.broadcasted_iota(jnp.int32, (p, HEAD_DIM), 0).astype(F32)
    j = jnp.where(lane < half, lane, lane - half).astype(F32)
    inv = jnp.exp(j * (-jnp.log(ROPE_BASE) / half))
    ang = pos * inv
    cos_ref[...] = jnp.cos(ang)
    sin_ref[...] = jnp.where(lane < half, -1.0, 1.0) * jnp.sin(ang)


def _rope_tables(p):
    return pl.pallas_call(
        _rope_kernel,
        out_shape=(jax.ShapeDtypeStruct((p, HEAD_DIM), F32), jax.ShapeDtypeStruct((p, HEAD_DIM), F32)),
        name="rope",
    )()


def _rot(t, cos, sin_signed):
    return t * cos + pltpu.roll(t, HEAD_DIM // 2, axis=1) * sin_signed


def _inproj_kernel(x_ref, nw_ref, sh_ref, sc_ref, w_ref, o_ref, h_scr):
    @pl.when(pl.program_id(2) == 0)
    def _():
        rows = math.gcd(x_ref.shape[1], LANES)
        gain = nw_ref[...] * (1.0 + sc_ref[0])
        shift = sh_ref[0]

        def body(r, carry):
            sl = pl.ds(pl.multiple_of(r * rows, rows), rows)
            x = x_ref[0, sl, :]
            rs = lax.rsqrt(jnp.mean(x * x, axis=-1, keepdims=True) + EPS)
            h_scr[sl, :] = (x * rs * gain + shift).astype(BF16)
            return carry
        lax.fori_loop(0, x_ref.shape[1] // rows, body, 0)

    res = jnp.dot(h_scr[...], w_ref[...], preferred_element_type=F32)
    for cb in range(o_ref.shape[1]):
        o_ref[0, cb] = res[:, cb * LANES:(cb + 1) * LANES]


def _in_proj(x, norm_w, mod3, mod_row, w_bf, col0, ncols):
    b, t, d = x.shape
    tm = _tile(t, 1024)
    tn = _tile(math.gcd(ncols, col0), 1536)
    joff = col0 // tn
    nblk = tn // LANES
    return pl.pallas_call(
        _inproj_kernel,
        out_shape=jax.ShapeDtypeStruct((b, ncols // LANES, t, LANES), F32),
        grid=(b, t // tm, ncols // tn),
        in_specs=[pl.BlockSpec((1, tm, d), lambda bi, i, j: (bi, i, 0)),
                  pl.BlockSpec((1, d), lambda bi, i, j: (0, 0)),
                  pl.BlockSpec((1, 1, d), lambda bi, i, j: (mod_row(bi), 0, 0)),
                  pl.BlockSpec((1, 1, d), lambda bi, i, j: (mod_row(bi), 0, 1)),
                  pl.BlockSpec((d, tn), lambda bi, i, j: (0, j + joff))],
        out_specs=pl.BlockSpec((1, nblk, tm, LANES), lambda bi, i, j: (bi, j, i, 0)),
        scratch_shapes=[pltpu.VMEM((tm, d), BF16)],
        compiler_params=_params(("parallel", "parallel", "arbitrary")),
        name="in_proj",
    )(x, norm_w, mod3, mod3, w_bf)


def _log_sigmoid(x):
    return jnp.minimum(x, 0.0) - jnp.log(1.0 + jnp.exp(-jnp.abs(x)))


def _ret_kernel(q_ref, k_ref, v_ref, g_ref, kc_ref, vc_ref, cos_ref, sin_ref, decf_ref, decb_ref,
                gnw_ref, o_ref, q_scr, k_scr, kf_scr, kb_scr, v_scr, o_scr, din_scr, kvf_scr, kvb_scr,
                rf_scr, rb_scr, *, n_ctx):
    t = q_ref.shape[2]
    c = din_scr.shape[0]
    nc = t // c
    lgf = _log_sigmoid(decf_ref[0])[:, 0:1]
    lgb = _log_sigmoid(decb_ref[0])[:, 0:1]

    kc = _rot(kc_ref[0, 0], cos_ref[0:n_ctx, :], sin_ref[0:n_ctx, :])
    vc = vc_ref[0, 0].astype(BF16)
    tc = lax.broadcasted_iota(jnp.int32, (n_ctx, 1), 0).astype(F32)
    rf = _dot_tn((kc * jnp.exp(lgf * (n_ctx - 1.0 - tc))).astype(BF16), vc)
    rb = _dot_tn((kc * jnp.exp(lgb * tc)).astype(BF16), vc)

    scale = HEAD_DIM ** -0.5
    cos, sin = cos_ref[n_ctx:n_ctx + t, :], sin_ref[n_ctx:n_ctx + t, :]
    q_scr[...] = (_rot(q_ref[0, 0], cos, sin) * scale).astype(BF16)
    kr = _rot(k_ref[0, 0], cos, sin)
    ri_all = (lax.broadcasted_iota(jnp.int32, (t, HEAD_DIM), 0) % c).astype(F32)
    k_scr[...] = kr.astype(BF16)
    kf_scr[...] = (kr * jnp.exp(lgf * (c - 1.0 - ri_all))).astype(BF16)
    kb_scr[...] = (kr * jnp.exp(lgb * ri_all)).astype(BF16)
    v_scr[...] = v_ref[0, 0].astype(BF16)

    diff = (lax.broadcasted_iota(jnp.int32, (c, c), 0) - lax.broadcasted_iota(jnp.int32, (c, c), 1)).astype(F32)
    din_scr[...] = jnp.where(diff > 0, jnp.exp(lgf * jnp.maximum(diff, 0.0)),
                             jnp.where(diff < 0, jnp.exp(lgb * jnp.maximum(-diff, 0.0)), 2.0))

    for i in range(nc):
        sl = slice(i * c, (i + 1) * c)
        vi = v_scr[sl, :]
        scores = _dot_nt(q_scr[sl, :], k_scr[sl, :]) * din_scr[...]
        o_scr[sl, :] = jnp.dot(scores.astype(BF16), vi, preferred_element_type=F32)
        kvf_scr[i] = _dot_tn(kf_scr[sl, :], vi)
        kvb_scr[i] = _dot_tn(kb_scr[sl, :], vi)

    gc_f = jnp.exp(lgf * float(c))
    gc_b = jnp.exp(lgb * float(c))
    for i in range(nc):
        rf_scr[i] = rf.astype(BF16)
        rf = rf * gc_f + kvf_scr[i]
    for i in reversed(range(nc)):
        rb_scr[i] = rb.astype(BF16)
        rb = rb * gc_b + kvb_scr[i]

    ri = lax.broadcasted_iota(jnp.int32, (c, HEAD_DIM), 0).astype(F32)
    dq_f = jnp.exp(lgf * (ri + 1.0))
    dq_b = jnp.exp(lgb * (c - ri))
    for i in range(nc):
        sl = slice(i * c, (i + 1) * c)
        qi = q_scr[sl, :]
        o = (o_scr[sl, :] + jnp.dot(qi, rf_scr[i], preferred_element_type=F32) * dq_f
             + jnp.dot(qi, rb_scr[i], preferred_element_type=F32) * dq_b)
        mu = jnp.mean(o, axis=-1, keepdims=True)
        var = jnp.mean(jnp.square(o - mu), axis=-1, keepdims=True)
        on = (o - mu) * lax.rsqrt(var + GN_EPS) * gnw_ref[...]
        o_ref[0, sl, :] = (on * _silu(g_ref[0, 0, sl, :])).astype(o_ref.dtype)


def _retention(proj, ctx_kv, cos, sin, decf3, decb3, gn_w, n_heads, n_ctx):
    b, _, t, hd = proj.shape
    c = _tile(t, RET_CHUNK)
    nc = t // c
    lat = lambda g: pl.BlockSpec((1, 1, t, hd), lambda bi, h: (bi, g * n_heads + h, 0, 0))
    seq = lambda dt: pltpu.VMEM((t, hd), dt)
    return pl.pallas_call(
        functools.partial(_ret_kernel, n_ctx=n_ctx),
        out_shape=jax.ShapeDtypeStruct((b, t, n_heads * hd), BF16),
        grid=(b, n_heads),
        in_specs=[lat(0), lat(1), lat(2), lat(3),
                  pl.BlockSpec((1, 1, n_ctx, hd), lambda bi, h: (0, h, bi, 0)),
                  pl.BlockSpec((1, 1, n_ctx, hd), lambda bi, h: (0, n_heads + h, bi, 0)),
                  pl.BlockSpec(cos.shape, lambda bi, h: (0, 0)),
                  pl.BlockSpec(sin.shape, lambda bi, h: (0, 0)),
                  pl.BlockSpec((1, 1, LANES), lambda bi, h: (h, 0, 0)),
                  pl.BlockSpec((1, 1, LANES), lambda bi, h: (h, 0, 0)),
                  pl.BlockSpec((1, hd), lambda bi, h: (0, h))],
        out_specs=pl.BlockSpec((1, t, hd), lambda bi, h: (bi, 0, h)),
        scratch_shapes=[seq(BF16), seq(BF16), seq(BF16), seq(BF16), seq(BF16), seq(F32),
                        pltpu.VMEM((c, c), F32),
                        pltpu.VMEM((nc, hd, hd), F32), pltpu.VMEM((nc, hd, hd), F32),
                        pltpu.VMEM((nc, hd, hd), BF16), pltpu.VMEM((nc, hd, hd), BF16)],
        compiler_params=_params(("parallel", "parallel")),
        name="ret",
    )(proj, proj, proj, proj, ctx_kv, ctx_kv, cos, sin, decf3, decb3, gn_w)


CONV_PAD_ROWS = 16
CONV_LANE_CHUNK = 256


def _conv_kernel(a_ref, b_ref, w_ref, cb_ref, lnw_ref, lnb_ref, o_ref, up_scr, sh_scr, y_scr):
    nblk, tt = a_ref.shape[1], a_ref.shape[2]
    ch = nblk * LANES
    kw = w_ref.shape[0]
    n_seq = tt // GRID_W
    lead = CONV_PAD_ROWS - kw // 2
    rows = GRID_W + 2 * CONV_PAD_ROWS
    zeros = jnp.zeros((CONV_PAD_ROWS, ch), F32)
    for s in range(n_seq):
        up_scr[s, 0:CONV_PAD_ROWS, :] = zeros
        up_scr[s, CONV_PAD_ROWS + GRID_W:, :] = zeros
    for cb in range(nblk):
        u = a_ref[0, cb] * jax.nn.sigmoid(b_ref[0, cb])
        for s in range(n_seq):
            up_scr[s, CONV_PAD_ROWS:CONV_PAD_ROWS + GRID_W, cb * LANES:(cb + 1) * LANES] = (
                u[s * GRID_W:(s + 1) * GRID_W, :])

    cw = sh_scr.shape[2]

    def seq(s, carry):
        row0 = pl.multiple_of(s * GRID_W, GRID_W)
        for c0 in range(0, ch, cw):
            for r in range(SUBLANES):
                sh_scr[r] = up_scr[s, r:r + rows - SUBLANES, c0:c0 + cw]
            acc = jnp.broadcast_to(cb_ref[:, c0:c0 + cw], (GRID_W, cw))
            for k in range(kw):
                a8, r = divmod(lead + k, SUBLANES)
                acc = acc + sh_scr[r, a8 * SUBLANES:a8 * SUBLANES + GRID_W, :] * w_ref[k:k + 1, c0:c0 + cw]
            y_scr[pl.ds(row0, GRID_W), c0:c0 + cw] = acc
        return carry

    lax.fori_loop(0, n_seq, seq, 0)
    y = y_scr[...]
    mu = jnp.mean(y, axis=-1, keepdims=True)
    var = jnp.mean(jnp.square(y - mu), axis=-1, keepdims=True)
    yn = (y - mu) * lax.rsqrt(var + EPS) * lnw_ref[...] + lnb_ref[...]
    o_ref[0] = _silu(yn).astype(o_ref.dtype)


def _conv(proj, col0, conv_w, conv_b, ln_w, ln_b):
    b, _, t, _ = proj.shape
    kw, ch = conv_w.shape
    assert col0 % ch == 0 and ch % LANES == 0 and kw // 2 <= CONV_PAD_ROWS and t % GRID_W == 0
    nblk = ch // LANES
    tt = _tile(t, 4 * GRID_W)
    cw = min(ch, CONV_LANE_CHUNK)
    rows = GRID_W + 2 * CONV_PAD_ROWS
    ca, cb = col0 // ch, col0 // ch + 1
    vec = pl.BlockSpec((1, ch), lambda bi, i: (0, 0))
    return pl.pallas_call(
        _conv_kernel,
        out_shape=jax.ShapeDtypeStruct((b, t, ch), BF16),
        grid=(b, t // tt),
        in_specs=[pl.BlockSpec((1, nblk, tt, LANES), lambda bi, i: (bi, ca, i, 0)),
                  pl.BlockSpec((1, nblk, tt, LANES), lambda bi, i: (bi, cb, i, 0)),
                  pl.BlockSpec((kw, ch), lambda bi, i: (0, 0)),
                  vec, vec, vec],
        out_specs=pl.BlockSpec((1, tt, ch), lambda bi, i: (bi, i, 0)),
        scratch_shapes=[pltpu.VMEM((tt // GRID_W, rows, ch), F32),
                        pltpu.VMEM((SUBLANES, rows - SUBLANES, cw), F32),
                        pltpu.VMEM((tt, ch), F32)],
        compiler_params=_params(("parallel", "parallel")),
        name="conv",
    )(proj, proj, conv_w, conv_b, ln_w, ln_b)


def _outproj_kernel(yr_ref, yc_ref, wr_ref, wc_ref, x_ref, g1_ref, sh2_ref, sc2_ref, pmn_ref, pfn_ref,
                    rwh_ref, rwl_ref, rb_ref, x1_ref, h2p_ref, lg_ref):
    tm = x_ref.shape[1]
    mix = (jnp.dot(yr_ref[0], wr_ref[...], preferred_element_type=F32)
           + jnp.dot(yc_ref[0], wc_ref[...], preferred_element_type=F32))
    x1 = x_ref[0] + g1_ref[0] * _rms(mix, pmn_ref[...])
    x1_ref[0] = x1
    h2 = _rms(x1, pfn_ref[...]) * (1.0 + sc2_ref[0]) + sh2_ref[0]
    h2_hi = h2.astype(BF16)
    h2_lo = (h2 - h2_hi.astype(F32)).astype(BF16)
    lg_ref[...] = (_dot_nt(rwh_ref[...], h2_hi) + _dot_nt(rwh_ref[...], h2_lo)
                   + _dot_nt(rwl_ref[...], h2_hi) + rb_ref[...])
    _pack_rows(h2, h2p_ref, tm)


def _out_proj(y_ret, y_conv, w_out_bf, x, mod3, post_mix_norm, pre_ffn_norm, rw_hi, rw_lo, rb_col):
    b, t, d = x.shape
    d_ret, d_conv = y_ret.shape[2], y_conv.shape[2]
    e_rows = rw_hi.shape[0]
    seg = d // 2 // LANES
    tm = _tile(t, 512)
    per_b = t // tm
    mod = lambda col: pl.BlockSpec((1, 1, d), lambda bi, i: (bi, 0, col))
    vec = pl.BlockSpec((1, d), lambda bi, i: (0, 0))
    row = lambda width: pl.BlockSpec((1, tm, width), lambda bi, i: (bi, i, 0))
    rw = pl.BlockSpec((e_rows, d), lambda bi, i: (0, 0))
    w_ret, w_conv = w_out_bf[:d_ret], w_out_bf[d_ret:]
    return pl.pallas_call(
        _outproj_kernel,
        out_shape=(jax.ShapeDtypeStruct((b, t, d), F32),
                   jax.ShapeDtypeStruct((b * t * seg, LANES), U32),
                   jax.ShapeDtypeStruct((e_rows, b * t), F32)),
        grid=(b, per_b),
        in_specs=[row(d_ret), row(d_conv),
                  pl.BlockSpec((d_ret, d), lambda bi, i: (0, 0)),
                  pl.BlockSpec((d_conv, d), lambda bi, i: (0, 0)),
                  row(d), mod(2), mod(3), mod(4), vec, vec, rw, rw,
                  pl.BlockSpec((e_rows, 1), lambda bi, i: (0, 0))],
        out_specs=(row(d),
                   pl.BlockSpec((tm * seg, LANES), lambda bi, i: (bi * per_b + i, 0)),
                   pl.BlockSpec((e_rows, tm), lambda bi, i: (0, bi * per_b + i))),
        compiler_params=_params(("parallel", "parallel")),
        name="out_proj",
    )(y_ret, y_conv, w_ret, w_conv, x, mod3, mod3, mod3, post_mix_norm, pre_ffn_norm, rw_hi, rw_lo, rb_col)


def _route_kernel(lg_ref, idx_ref, gate_ref, rank_ref, cnt_ref, tri_scr, run_scr, *, n_experts):
    e_rows, tr = lg_ref.shape

    @pl.when(pl.program_id(0) == 0)
    def _():
        r = lax.broadcasted_iota(jnp.int32, (tr, tr), 0)
        c = lax.broadcasted_iota(jnp.int32, (tr, tr), 1)
        tri_scr[...] = jnp.where(r <= c, 1.0, 0.0).astype(BF16)
        run_scr[...] = jnp.zeros_like(run_scr)

    e_iota = lax.broadcasted_iota(jnp.int32, (e_rows, tr), 0)
    neg = jnp.float32(-jnp.inf)
    logits = jnp.where(e_iota < n_experts, lg_ref[...], neg)
    vals, hots = [], []
    for k in range(TOP_K):
        m = jnp.max(logits, axis=0, keepdims=True)
        ik = jnp.min(jnp.where(logits == m, e_iota, e_rows), axis=0, keepdims=True)
        hot = e_iota == ik
        logits = jnp.where(hot, neg, logits)
        vals.append(m)
        hots.append(hot)
        idx_ref[k:k + 1, :] = ik
    exps = [jnp.exp(v - vals[0]) for v in vals]
    den = exps[0]
    for e in exps[1:]:
        den = den + e
    for k in range(TOP_K):
        gate_ref[k:k + 1, :] = exps[k] / den

    sel = jnp.zeros((e_rows, tr), F32)
    for hot in hots:
        sel = sel + jnp.where(hot, 1.0, 0.0)
    csum = jnp.dot(sel.astype(BF16), tri_scr[...], preferred_element_type=F32)
    before = run_scr[:, 0:1] + csum - sel
    for k in range(TOP_K):
        rank_ref[k:k + 1, :] = jnp.sum(jnp.where(hots[k], before, 0.0), axis=0, keepdims=True).astype(jnp.int32)
    run_scr[...] = run_scr[...] + jnp.sum(sel, axis=1, keepdims=True)
    cnt_ref[...] = run_scr[...]


def _route(logits_t, n_experts):
    e_rows, n = logits_t.shape
    tr = _tile(n, 512)
    kt = lambda dt: jax.ShapeDtypeStruct((TOP_K, n), dt)
    blk = pl.BlockSpec((TOP_K, tr), lambda i: (0, i))
    return pl.pallas_call(
        functools.partial(_route_kernel, n_experts=n_experts),
        out_shape=(kt(jnp.int32), kt(F32), kt(jnp.int32), jax.ShapeDtypeStruct((e_rows, LANES), F32)),
        grid=(n // tr,),
        in_specs=[pl.BlockSpec((e_rows, tr), lambda i: (0, i))],
        out_specs=(blk, blk, blk, pl.BlockSpec((e_rows, LANES), lambda i: (0, 0))),
        scratch_shapes=[pltpu.VMEM((tr, tr), BF16), pltpu.VMEM((e_rows, LANES), F32)],
        compiler_params=_params(("arbitrary",)),
        name="route",
    )(logits_t)


def _dispatch_kernel(pos_ref, poff_ref, plen_ref, h_ref, xs_ref, zero_scr, sem, zsem, *, n_tok, n_exp, seg, tm):
    td = h_ref.shape[0] // seg
    step = pl.program_id(0)
    base = step * td

    def pad_copies(e, act):
        off, ln = poff_ref[e], plen_ref[e]
        for bit in range(tm.bit_length() - 1):
            size = 1 << bit

            @pl.when(((ln >> bit) & 1) == 1)
            def _():
                row = pl.multiple_of((off + (ln & (size - 1))) * seg, seg)
                act(pltpu.make_async_copy(zero_scr.at[pl.ds(0, size * seg)],
                                          xs_ref.at[pl.ds(row, size * seg)], zsem))

    def for_pads(act):
        def body(e, carry):
            pad_copies(e, act)
            return carry
        lax.fori_loop(0, n_exp, body, 0)

        half = zero_scr.shape[0]
        end = (poff_ref[n_exp - 1] + plen_ref[n_exp - 1]) * seg

        def tail(n, carry):
            row = pl.multiple_of(end + n * half, seg)
            act(pltpu.make_async_copy(zero_scr, xs_ref.at[pl.ds(row, half)], zsem))
            return carry
        lax.fori_loop(0, (xs_ref.shape[0] - end) // half, tail, 0)

    @pl.when(step == 0)
    def _():
        zero_scr[...] = jnp.zeros_like(zero_scr)
        for_pads(lambda cp: cp.start())

    def row_copy(t, k):
        p = pos_ref[k * n_tok + base + t]
        return pltpu.make_async_copy(h_ref.at[pl.ds(pl.multiple_of(t * seg, seg), seg)],
                                     xs_ref.at[pl.ds(pl.multiple_of(p * seg, seg), seg)], sem)

    def start(t, carry):
        for k in range(TOP_K):
            row_copy(t, k).start()
        return carry

    lax.fori_loop(0, td, start, 0)
    for k in range(TOP_K):
        pltpu.make_async_copy(h_ref, h_ref, sem).wait()

    @pl.when(step == 0)
    def _():
        for_pads(lambda cp: cp.wait())


def _dispatch(pos_flat, pad_off, pad_len, h2p, n_slots, seg, tm):
    n = h2p.shape[0] // seg
    n_exp = pad_off.shape[0]
    assert tm & (tm - 1) == 0
    td = _tile(n, 512)
    return pl.pallas_call(
        functools.partial(_dispatch_kernel, n_tok=n, n_exp=n_exp, seg=seg, tm=tm),
        out_shape=jax.ShapeDtypeStruct((n_slots * seg, LANES), U32),
        grid_spec=pltpu.PrefetchScalarGridSpec(
            num_scalar_prefetch=3,
            grid=(n // td,),
            in_specs=[pl.BlockSpec((td * seg, LANES), lambda i, *_: (i, 0))],
            out_specs=pl.BlockSpec(memory_space=pl.ANY),
            scratch_shapes=[pltpu.VMEM((max(tm // 2, 1) * seg, LANES), U32),
                            pltpu.SemaphoreType.DMA(()), pltpu.SemaphoreType.DMA(())]),
        compiler_params=_params(("arbitrary",)),
        name="dispatch",
    )(pos_flat, pad_off, pad_len, h2p)


def _pack_rows(val, ref, tm):
    half = val.shape[1] // 2
    seg = half // LANES
    vb = val.astype(BF16).astype(F32)
    packed = ((lax.bitcast_convert_type(vb[:, :half], U32) >> 16)
              | (lax.bitcast_convert_type(vb[:, half:], U32) & jnp.uint32(HI16)))
    for s in range(seg):
        ref[pl.ds(s, tm, stride=seg), :] = packed[:, s * LANES:(s + 1) * LANES]


def _unpack_seg(words):
    return (lax.bitcast_convert_type(words << 16, F32),
            lax.bitcast_convert_type(words & jnp.uint32(HI16), F32))


def _weight_runs(tile_expert, n_passes):
    n_tiles = tile_expert.shape[0]
    total = n_tiles * n_passes
    e_lin = jnp.tile(tile_expert, n_passes)
    j_lin = jnp.repeat(jnp.arange(n_passes, dtype=jnp.int32), n_tiles)
    i_lin = jnp.tile(jnp.arange(n_tiles, dtype=jnp.int32), n_passes)
    start = (i_lin == 0) | (e_lin != jnp.roll(e_lin, 1))
    idx = jnp.arange(total, dtype=jnp.int32)
    nxt = jnp.min(jnp.where(start[None, :] & (idx[None, :] > idx[:, None]), idx[None, :], total), axis=1)
    has = nxt < total
    nxt = jnp.minimum(nxt, total - 1)
    return start.astype(jnp.int32), jnp.where(has, e_lin[nxt], -1).astype(jnp.int32), j_lin[nxt]


def _stream_weights(step, first_ref, ne_ref, nj_ref, cur, copies, cast):
    @pl.when(step == 0)
    def _():
        for cp in copies(*cur):
            cp.start()

    @pl.when(first_ref[step] == 1)
    def _():
        for cp in copies(*cur):
            cp.wait()
        cast()

        @pl.when(ne_ref[step] >= 0)
        def _():
            for cp in copies(ne_ref[step], nj_ref[step]):
                cp.start()


def _ffn1_kernel(te_ref, nu_ref, first_ref, ne_ref, nj_ref, tv_ref, xs_ref, w_hbm, bg_ref, bl_ref, act_ref,
                 wg_stage, wl_stage, wg_scr, wl_scr, x_scr, sem, *, seg, d_ff):
    del nu_ref
    j, i = pl.program_id(0), pl.program_id(1)
    tm = x_scr.shape[0]
    tn = wg_scr.shape[1]
    half = seg * LANES

    def copies(e, jj):
        col = pl.multiple_of(jj * tn, tn)
        return (pltpu.make_async_copy(w_hbm.at[e, :, pl.ds(col, tn)], wg_stage, sem.at[0]),
                pltpu.make_async_copy(w_hbm.at[e, :, pl.ds(d_ff + col, tn)], wl_stage, sem.at[1]))

    def cast():
        wg_scr[...] = wg_stage[...].astype(BF16)
        wl_scr[...] = wl_stage[...].astype(BF16)

    _stream_weights(j * pl.num_programs(1) + i, first_ref, ne_ref, nj_ref, (te_ref[i], j), copies, cast)

    rows = tm // ROW_GROUPS
    for r0 in range(0, tm, rows):
        @pl.when(tv_ref[i] > r0)
        def _():
            for s in range(seg):
                lo, hi = _unpack_seg(xs_ref[pl.ds(r0 * seg + s, rows, stride=seg), :])
                x_scr[r0:r0 + rows, s * LANES:(s + 1) * LANES] = lo.astype(BF16)
                x_scr[r0:r0 + rows, half + s * LANES:half + (s + 1) * LANES] = hi.astype(BF16)
            x = x_scr[r0:r0 + rows, :]
            glu = jnp.minimum(jnp.dot(x, wg_scr[...], preferred_element_type=F32) + bg_ref[0], SWIGLU_LIMIT)
            lin = jnp.clip(jnp.dot(x, wl_scr[...], preferred_element_type=F32) + bl_ref[0],
                           -SWIGLU_LIMIT, SWIGLU_LIMIT)
            act_ref[r0:r0 + rows, :] = (glu * jax.nn.sigmoid(SWIGLU_ALPHA * glu) * (lin + 1.0)).astype(act_ref.dtype)

        @pl.when(tv_ref[i] <= r0)
        def _():
            act_ref[r0:r0 + rows, :] = jnp.zeros((rows, tn), act_ref.dtype)


def _ffn1(tile_expert, n_used, tile_valid, xs, w1, b1_3, tm, seg):
    d = seg * LANES * 2
    slots = xs.shape[0] // seg
    d_ff = w1.shape[2] // 2
    tn = _tile(d_ff, 1024)
    nj = d_ff // tn
    n_tiles = slots // tm
    row = lambda i, nu: jnp.minimum(i, nu[0] - 1)
    return pl.pallas_call(
        functools.partial(_ffn1_kernel, seg=seg, d_ff=d_ff),
        out_shape=jax.ShapeDtypeStruct((slots, d_ff), BF16),
        grid_spec=pltpu.PrefetchScalarGridSpec(
            num_scalar_prefetch=6,
            grid=(nj, n_tiles),
            in_specs=[pl.BlockSpec((tm * seg, LANES), lambda j, i, te, nu, *_: (row(i, nu), 0)),
                      pl.BlockSpec(memory_space=pl.ANY),
                      pl.BlockSpec((1, 1, tn), lambda j, i, te, *_: (te[i], 0, j)),
                      pl.BlockSpec((1, 1, tn), lambda j, i, te, *_: (te[i], 0, nj + j))],
            out_specs=pl.BlockSpec((tm, tn), lambda j, i, *_: (i, j)),
            scratch_shapes=[pltpu.VMEM((d, tn), F32), pltpu.VMEM((d, tn), F32),
                            pltpu.VMEM((d, tn), BF16), pltpu.VMEM((d, tn), BF16),
                            pltpu.VMEM((tm, d), BF16), pltpu.SemaphoreType.DMA((2,))]),
        compiler_params=_params(("arbitrary", "arbitrary")),
        name="ffn1",
    )(tile_expert, n_used, *_weight_runs(tile_expert, nj), tile_valid, xs, w1, b1_3, b1_3)


def _ffn2_kernel(te_ref, nu_ref, first_ref, ne_ref, nj_ref, tv_ref, act_ref, w_hbm, b_ref, ys_ref,
                 w_stage, w_scr, sem):
    del nu_ref
    i = pl.program_id(0)
    tm = act_ref.shape[0]

    def copies(e, jj):
        del jj
        return (pltpu.make_async_copy(w_hbm.at[e], w_stage, sem),)

    def cast():
        w_scr[...] = w_stage[...].astype(BF16)

    _stream_weights(i, first_ref, ne_ref, nj_ref, (te_ref[i], 0), copies, cast)

    rows = tm // ROW_GROUPS
    seg = ys_ref.shape[0] // tm
    for r0 in range(0, tm, rows):
        ys_rows = ys_ref.at[pl.ds(r0 * seg, rows * seg)]

        @pl.when(tv_ref[i] > r0)
        def _():
            y = jnp.dot(act_ref[r0:r0 + rows, :], w_scr[...], preferred_element_type=F32) + b_ref[0]
            _pack_rows(y, ys_rows, rows)

        @pl.when(tv_ref[i] <= r0)
        def _():
            ys_rows[...] = jnp.zeros(ys_rows.shape, ys_rows.dtype)


def _ffn2(tile_expert, n_used, tile_valid, act, w2, b2_3, tm):
    slots, d_ff = act.shape
    d = w2.shape[2]
    seg = d // 2 // LANES
    n_tiles = slots // tm
    row = lambda i, nu: jnp.minimum(i, nu[0] - 1)
    return pl.pallas_call(
        _ffn2_kernel,
        out_shape=jax.ShapeDtypeStruct((slots * seg, LANES), U32),
        grid_spec=pltpu.PrefetchScalarGridSpec(
            num_scalar_prefetch=6,
            grid=(n_tiles,),
            in_specs=[pl.BlockSpec((tm, d_ff), lambda i, te, nu, *_: (row(i, nu), 0)),
                      pl.BlockSpec(memory_space=pl.ANY),
                      pl.BlockSpec((1, 1, d), lambda i, te, *_: (te[i], 0, 0))],
            out_specs=pl.BlockSpec((tm * seg, LANES), lambda i, *_: (i, 0)),
            scratch_shapes=[pltpu.VMEM((d_ff, d), F32), pltpu.VMEM((d_ff, d), BF16),
                            pltpu.SemaphoreType.DMA(())]),
        compiler_params=_params(("arbitrary",)),
        name="ffn2",
    )(tile_expert, n_used, *_weight_runs(tile_expert, 1), tile_valid, act, w2, b2_3)


def _combine_kernel(pos_ref, ys_ref, gate_ref, x1_ref, g2_ref, pfn_ref, o_ref, buf, moe_scr, sem, *, n_tok):
    tc, d = x1_ref.shape
    seg = buf.shape[2] // tc
    half = seg * LANES
    step = pl.program_id(0)

    def gather(st, slot):
        def body(t, carry):
            for k in range(TOP_K):
                p = pos_ref[k * n_tok + st * tc + t]
                pltpu.make_async_copy(ys_ref.at[pl.ds(pl.multiple_of(p * seg, seg), seg)],
                                      buf.at[slot, k, pl.ds(pl.multiple_of(t * seg, seg), seg)],
                                      sem.at[slot, k]).start()
            return carry
        lax.fori_loop(0, tc, body, 0)

    @pl.when(step == 0)
    def _():
        gather(0, 0)

    @pl.when(step + 1 < pl.num_programs(0))
    def _():
        gather(step + 1, (step + 1) % 2)

    slot = step % 2
    for k in range(TOP_K):
        pltpu.make_async_copy(buf.at[slot, k], buf.at[slot, k], sem.at[slot, k]).wait()
    gates = [jnp.broadcast_to(gate_ref[:, k:k + 1], (tc, LANES)) for k in range(TOP_K)]
    for s in range(seg):
        m_lo = m_hi = None
        for k in range(TOP_K):
            lo, hi = _unpack_seg(buf[slot, k, pl.ds(s, tc, stride=seg), :])
            m_lo = lo * gates[k] if m_lo is None else m_lo + lo * gates[k]
            m_hi = hi * gates[k] if m_hi is None else m_hi + hi * gates[k]
        moe_scr[:, s * LANES:(s + 1) * LANES] = m_lo
        moe_scr[:, half + s * LANES:half + (s + 1) * LANES] = m_hi
    o_ref[...] = x1_ref[...] + g2_ref[0] * _rms(moe_scr[...], pfn_ref[...])


def _combine(pos_flat, ys, gates_t, x1, mod3, post_ffn_norm, seq):
    n, d = x1.shape
    seg = d // 2 // LANES
    tc = _tile(seq, 256)
    per_batch = seq // tc
    return pl.pallas_call(
        functools.partial(_combine_kernel, n_tok=n),
        out_shape=jax.ShapeDtypeStruct((n, d), F32),
        grid_spec=pltpu.PrefetchScalarGridSpec(
            num_scalar_prefetch=1,
            grid=(n // tc,),
            in_specs=[pl.BlockSpec(memory_space=pl.ANY),
                      pl.BlockSpec((tc, TOP_K), lambda i, pos: (i, 0)),
                      pl.BlockSpec((tc, d), lambda i, pos: (i, 0)),
                      pl.BlockSpec((1, 1, d), lambda i, pos: (i // per_batch, 0, 5)),
                      pl.BlockSpec((1, d), lambda i, pos: (0, 0))],
            out_specs=pl.BlockSpec((tc, d), lambda i, pos: (i, 0)),
            scratch_shapes=[pltpu.VMEM((2, TOP_K, tc * seg, LANES), U32), pltpu.VMEM((tc, d), F32),
                            pltpu.SemaphoreType.DMA((2, TOP_K))]),
        compiler_params=_params(("arbitrary",)),
        name="combine",
    )(pos_flat, ys, gates_t, x1, mod3, post_ffn_norm)


def _moe(h2p, logits_t, x1, mod3, post_ffn_norm, w1, b1, w2, b2, seq, n_exp):
    n, d = x1.shape
    seg = d // 2 // LANES
    tm = 1 << (min(n * TOP_K, 512).bit_length() - 1)
    idx, gates, rank, cnt = _route(logits_t, n_exp)

    counts = cnt[:n_exp, 0].astype(jnp.int32)
    padded = (counts + tm - 1) // tm * tm
    ends = jnp.cumsum(padded)
    starts = ends - padded
    hot = idx[:, :, None] == jnp.arange(n_exp, dtype=jnp.int32)
    pos_flat = (jnp.sum(jnp.where(hot, starts, 0), axis=-1) + rank).reshape(-1)
    n_tiles = -(-n * TOP_K // tm) + n_exp
    n_used = (ends[-1] // tm).astype(jnp.int32)
    tile_start = jnp.minimum(jnp.arange(n_tiles, dtype=jnp.int32), n_used - 1) * tm
    tile_expert = jnp.sum(tile_start[:, None] >= ends[None, :], axis=1).astype(jnp.int32)
    tile_hot = tile_expert[:, None] == jnp.arange(n_exp, dtype=jnp.int32)
    tile_rows_left = jnp.sum(jnp.where(tile_hot, starts + counts, 0), axis=1) - tile_start
    tile_valid = jnp.where(jnp.arange(n_tiles) < n_used, jnp.clip(tile_rows_left, 0, tm), 0).astype(jnp.int32)
    n_used = n_used.reshape(1)

    xs = _dispatch(pos_flat, starts + counts, padded - counts, h2p, n_tiles * tm, seg, tm)
    act = _ffn1(tile_expert, n_used, tile_valid, xs, w1, b1[:, None, :], tm, seg)
    ys = _ffn2(tile_expert, n_used, tile_valid, act, w2, b2[:, None, :], tm)
    return _combine(pos_flat, ys, gates.T, x1, mod3, post_ffn_norm, seq)


def kernel(x, c, ctx, c_ctx, ada_w, ada_b, pre_mix_norm, post_mix_norm, pre_ffn_norm, post_ffn_norm,
           w_in, ret_decay_fwd, ret_decay_bwd, ret_gn_w, conv_w, conv_b, conv_ln_w, conv_ln_b, w_out,
           router_w, router_b, w1, b1, w2, b2):
    assert ada_w.shape[0] == 1, "single-layer stack only"
    b, t, d = x.shape
    n_ctx = ctx.shape[1]
    n_heads = ret_decay_fwd.shape[1]
    d_ret = n_heads * HEAD_DIM
    n_exp = router_w.shape[2]
    assert b < MOD_ROWS and ret_gn_w.shape[1] == d_ret and d % (2 * LANES) == 0

    cc = jnp.zeros((MOD_ROWS, d), F32).at[:b].set(c).at[b].set(c_ctx)
    mod3 = _ada(cc, ada_w[0], ada_b)[:, None, :]
    cos, sin = _rope_tables(n_ctx + t)

    w_in_bf = w_in[0].astype(BF16)
    proj = _in_proj(x, pre_mix_norm, mod3, lambda bi: bi, w_in_bf, 0, w_in.shape[2])
    ctx_kv = _in_proj(ctx.reshape(1, b * n_ctx, d), pre_mix_norm, mod3, lambda bi: b, w_in_bf, d_ret, 2 * d_ret)

    lane_bcast = lambda v: jnp.broadcast_to(v.reshape(n_heads, 1, 1), (n_heads, 1, LANES))
    y_ret = _retention(proj, ctx_kv, cos, sin, lane_bcast(ret_decay_fwd[0]), lane_bcast(ret_decay_bwd[0]),
                       ret_gn_w, n_heads, n_ctx)
    y_conv = _conv(proj, 4 * d_ret, conv_w[0], conv_b, conv_ln_w, conv_ln_b)

    e_rows = -(-n_exp // SUBLANES) * SUBLANES
    rw_t = jnp.zeros((e_rows, d), F32).at[:n_exp].set(router_w[0].T)
    rw_hi = rw_t.astype(BF16)
    rw_lo = (rw_t - rw_hi.astype(F32)).astype(BF16)
    rb_col = jnp.zeros((e_rows, 1), F32).at[:n_exp, 0].set(router_b[0])
    x1, h2p, logits_t = _out_proj(y_ret, y_conv, w_out[0].astype(BF16), x, mod3, post_mix_norm, pre_ffn_norm,
                                  rw_hi, rw_lo, rb_col)

    out = _moe(h2p, logits_t, x1.reshape(b * t, d), mod3, post_ffn_norm, w1[0], b1[0], w2[0], b2[0], t, n_exp)
    return out.reshape(b, t, d)
```

```python
import functools
import math

import jax
import jax.numpy as jnp
from jax import lax
from jax.experimental import pallas as pl
from jax.experimental.pallas import tpu as pltpu

F32 = jnp.float32
BF16 = jnp.bfloat16
U32 = jnp.uint32

GRID_W = 64
HEAD_DIM = 128
RET_CHUNK = 256
ROPE_BASE = 10000.0
TOP_K = 4
SWIGLU_ALPHA = 1.702
SWIGLU_LIMIT = 7.0
EPS = 1e-6
GN_EPS = 1e-5
LANES = 128
SUBLANES = 8
MOD_ROWS = 16
VMEM_LIMIT = 56 * 1024 * 1024
HI16 = 0xFFFF0000
ROW_GROUPS = 1


def _tile(n, pref):
    t = min(n, pref)
    while n % t:
        t -= 1
    return t


def _params(sem, vmem=VMEM_LIMIT):
    return pltpu.CompilerParams(dimension_semantics=sem, vmem_limit_bytes=vmem)


def _rms(x, w):
    return x * lax.rsqrt(jnp.mean(x * x, axis=-1, keepdims=True) + EPS) * w


def _silu(x):
    return x * jax.nn.sigmoid(x)


def _dot_nt(a, b):
    return lax.dot_general(a, b, (((1,), (1,)), ((), ())), preferred_element_type=F32)


def _dot_tn(a, b):
    return lax.dot_general(a, b, (((0,), (0,)), ((), ())), preferred_element_type=F32)


def _ada_kernel(c_ref, w_ref, b_ref, o_ref):
    s = _silu(c_ref[...])
    w = w_ref[...]
    s_hi, w_hi = s.astype(BF16), w.astype(BF16)
    s_lo = (s - s_hi.astype(F32)).astype(BF16)
    w_lo = (w - w_hi.astype(F32)).astype(BF16)
    o_ref[...] = (jnp.dot(s_hi, w_hi, preferred_element_type=F32) + jnp.dot(s_lo, w_hi, preferred_element_type=F32)
                  + jnp.dot(s_hi, w_lo, preferred_element_type=F32) + b_ref[...])


def _ada(cc, w, b):
    d, n = w.shape
    tn = _tile(n, 1024)
    return pl.pallas_call(
        _ada_kernel,
        out_shape=jax.ShapeDtypeStruct((MOD_ROWS, n), F32),
        grid=(n // tn,),
        in_specs=[pl.BlockSpec((MOD_ROWS, d), lambda j: (0, 0)),
                  pl.BlockSpec((d, tn), lambda j: (0, j)),
                  pl.BlockSpec((1, tn), lambda j: (0, j))],
        out_specs=pl.BlockSpec((MOD_ROWS, tn), lambda j: (0, j)),
        compiler_params=_params(("parallel",)),
        name="ada",
    )(cc, w, b)


def _rope_kernel(cos_ref, sin_ref):
    p, _ = cos_ref.shape
    half = HEAD_DIM // 2
    lane = lax.broadcasted_iota(jnp.int32, (p, HEAD_DIM), 1)
    pos = lax.broadcasted_iota(jnp.int32, (p, HEAD_DIM), 0).astype(F32)
    j = jnp.where(lane < half, lane, lane - half).astype(F32)
    inv = jnp.exp(j * (-jnp.log(ROPE_BASE) / half))
    ang = pos * inv
    cos_ref[...] = jnp.cos(ang)
    sin_ref[...] = jnp.where(lane < half, -1.0, 1.0) * jnp.sin(ang)


def _rope_tables(p):
    return pl.pallas_call(
        _rope_kernel,
        out_shape=(jax.ShapeDtypeStruct((p, HEAD_DIM), F32), jax.ShapeDtypeStruct((p, HEAD_DIM), F32)),
        name="rope",
    )()


def _rot(t, cos, sin_signed):
    return t * cos + pltpu.roll(t, HEAD_DIM // 2, axis=1) * sin_signed


def _inproj_kernel(x_ref, nw_ref, sh_ref, sc_ref, w_ref, o_ref, h_scr):
    @pl.when(pl.program_id(2) == 0)
    def _():
        rows = math.gcd(x_ref.shape[1], LANES)
        gain = nw_ref[...] * (1.0 + sc_ref[0])
        shift = sh_ref[0]

        def body(r, carry):
            sl = pl.ds(pl.multiple_of(r * rows, rows), rows)
            x = x_ref[0, sl, :]
            rs = lax.rsqrt(jnp.mean(x * x, axis=-1, keepdims=True) + EPS)
            h_scr[sl, :] = (x * rs * gain + shift).astype(BF16)
            return carry
        lax.fori_loop(0, x_ref.shape[1] // rows, body, 0)

    res = jnp.dot(h_scr[...], w_ref[...], preferred_element_type=F32)
    for cb in range(o_ref.shape[1]):
        o_ref[0, cb] = res[:, cb * LANES:(cb + 1) * LANES]


def _in_proj(x, norm_w, mod3, mod_row, w_bf, col0, ncols):
    b, t, d = x.shape
    tm = _tile(t, 1024)
    tn = _tile(math.gcd(ncols, col0), 1536)
    joff = col0 // tn
    nblk = tn // LANES
    return pl.pallas_call(
        _inproj_kernel,
        out_shape=jax.ShapeDtypeStruct((b, ncols // LANES, t, LANES), F32),
        grid=(b, t // tm, ncols // tn),
        in_specs=[pl.BlockSpec((1, tm, d), lambda bi, i, j: (bi, i, 0)),
                  pl.BlockSpec((1, d), lambda bi, i, j: (0, 0)),
                  pl.BlockSpec((1, 1, d), lambda bi, i, j: (mod_row(bi), 0, 0)),
                  pl.BlockSpec((1, 1, d), lambda bi, i, j: (mod_row(bi), 0, 1)),
                  pl.BlockSpec((d, tn), lambda bi, i, j: (0, j + joff))],
        out_specs=pl.BlockSpec((1, nblk, tm, LANES), lambda bi, i, j: (bi, j, i, 0)),
        scratch_shapes=[pltpu.VMEM((tm, d), BF16)],
        compiler_params=_params(("parallel", "parallel", "arbitrary")),
        name="in_proj",
    )(x, norm_w, mod3, mod3, w_bf)


def _log_sigmoid(x):
    return jnp.minimum(x, 0.0) - jnp.log(1.0 + jnp.exp(-jnp.abs(x)))


def _ret_kernel(q_ref, k_ref, v_ref, g_ref, kc_ref, vc_ref, cos_ref, sin_ref, decf_ref, decb_ref,
                gnw_ref, o_ref, q_scr, k_scr, kf_scr, kb_scr, v_scr, o_scr, din_scr, kvf_scr, kvb_scr,
                rf_scr, rb_scr, *, n_ctx):
    t = q_ref.shape[2]
    c = din_scr.shape[0]
    nc = t // c
    lgf = _log_sigmoid(decf_ref[0])[:, 0:1]
    lgb = _log_sigmoid(decb_ref[0])[:, 0:1]

    kc = _rot(kc_ref[0, 0], cos_ref[0:n_ctx, :], sin_ref[0:n_ctx, :])
    vc = vc_ref[0, 0].astype(BF16)
    tc = lax.broadcasted_iota(jnp.int32, (n_ctx, 1), 0).astype(F32)
    rf = _dot_tn((kc * jnp.exp(lgf * (n_ctx - 1.0 - tc))).astype(BF16), vc)
    rb = _dot_tn((kc * jnp.exp(lgb * tc)).astype(BF16), vc)

    scale = HEAD_DIM ** -0.5
    cos, sin = cos_ref[n_ctx:n_ctx + t, :], sin_ref[n_ctx:n_ctx + t, :]
    q_scr[...] = (_rot(q_ref[0, 0], cos, sin) * scale).astype(BF16)
    kr = _rot(k_ref[0, 0], cos, sin)
    ri_all = (lax.broadcasted_iota(jnp.int32, (t, HEAD_DIM), 0) % c).astype(F32)
    k_scr[...] = kr.astype(BF16)
    kf_scr[...] = (kr * jnp.exp(lgf * (c - 1.0 - ri_all))).astype(BF16)
    kb_scr[...] = (kr * jnp.exp(lgb * ri_all)).astype(BF16)
    v_scr[...] = v_ref[0, 0].astype(BF16)

    diff = (lax.broadcasted_iota(jnp.int32, (c, c), 0) - lax.broadcasted_iota(jnp.int32, (c, c), 1)).astype(F32)
    din_scr[...] = jnp.where(diff > 0, jnp.exp(lgf * jnp.maximum(diff, 0.0)),
                             jnp.where(diff < 0, jnp.exp(lgb * jnp.maximum(-diff, 0.0)), 2.0))

    for i in range(nc):
        sl = slice(i * c, (i + 1) * c)
        vi = v_scr[sl, :]
        scores = _dot_nt(q_scr[sl, :], k_scr[sl, :]) * din_scr[...]
        o_scr[sl, :] = jnp.dot(scores.astype(BF16), vi, preferred_element_type=F32)
        kvf_scr[i] = _dot_tn(kf_scr[sl, :], vi)
        kvb_scr[i] = _dot_tn(kb_scr[sl, :], vi)

    gc_f = jnp.exp(lgf * float(c))
    gc_b = jnp.exp(lgb * float(c))
    for i in range(nc):
        rf_scr[i] = rf.astype(BF16)
        rf = rf * gc_f + kvf_scr[i]
    for i in reversed(range(nc)):
        rb_scr[i] = rb.astype(BF16)
        rb = rb * gc_b + kvb_scr[i]

    ri = lax.broadcasted_iota(jnp.int32, (c, HEAD_DIM), 0).astype(F32)
    dq_f = jnp.exp(lgf * (ri + 1.0))
    dq_b = jnp.exp(lgb * (c - ri))
    for i in range(nc):
        sl = slice(i * c, (i + 1) * c)
        qi = q_scr[sl, :]
        o = (o_scr[sl, :] + jnp.dot(qi, rf_scr[i], preferred_element_type=F32) * dq_f
             + jnp.dot(qi, rb_scr[i], preferred_element_type=F32) * dq_b)
        mu = jnp.mean(o, axis=-1, keepdims=True)
        var = jnp.mean(jnp.square(o - mu), axis=-1, keepdims=True)
        on = (o - mu) * lax.rsqrt(var + GN_EPS) * gnw_ref[...]
        o_ref[0, sl, :] = (on * _silu(g_ref[0, 0, sl, :])).astype(o_ref.dtype)


def _retention(proj, ctx_kv, cos, sin, decf3, decb3, gn_w, n_heads, n_ctx):
    b, _, t, hd = proj.shape
    c = _tile(t, RET_CHUNK)
    nc = t // c
    lat = lambda g: pl.BlockSpec((1, 1, t, hd), lambda bi, h: (bi, g * n_heads + h, 0, 0))
    seq = lambda dt: pltpu.VMEM((t, hd), dt)
    return pl.pallas_call(
        functools.partial(_ret_kernel, n_ctx=n_ctx),
        out_shape=jax.ShapeDtypeStruct((b, t, n_heads * hd), BF16),
        grid=(b, n_heads),
        in_specs=[lat(0), lat(1), lat(2), lat(3),
                  pl.BlockSpec((1, 1, n_ctx, hd), lambda bi, h: (0, h, bi, 0)),
                  pl.BlockSpec((1, 1, n_ctx, hd), lambda bi, h: (0, n_heads + h, bi, 0)),
                  pl.BlockSpec(cos.shape, lambda bi, h: (0, 0)),
                  pl.BlockSpec(sin.shape, lambda bi, h: (0, 0)),
                  pl.BlockSpec((1, 1, LANES), lambda bi, h: (h, 0, 0)),
                  pl.BlockSpec((1, 1, LANES), lambda bi, h: (h, 0, 0)),
                  pl.BlockSpec((1, hd), lambda bi, h: (0, h))],
        out_specs=pl.BlockSpec((1, t, hd), lambda bi, h: (bi, 0, h)),
        scratch_shapes=[seq(BF16), seq(BF16), seq(BF16), seq(BF16), seq(BF16), seq(F32),
                        pltpu.VMEM((c, c), F32),
                        pltpu.VMEM((nc, hd, hd), F32), pltpu.VMEM((nc, hd, hd), F32),
                        pltpu.VMEM((nc, hd, hd), BF16), pltpu.VMEM((nc, hd, hd), BF16)],
        compiler_params=_params(("parallel", "parallel")),
        name="ret",
    )(proj, proj, proj, proj, ctx_kv, ctx_kv, cos, sin, decf3, decb3, gn_w)


CONV_PAD_ROWS = 16
CONV_LANE_CHUNK = 256


def _conv_kernel(a_ref, b_ref, w_ref, cb_ref, lnw_ref, lnb_ref, o_ref, up_scr, sh_scr, y_scr):
    nblk, tt = a_ref.shape[1], a_ref.shape[2]
    ch = nblk * LANES
    kw = w_ref.shape[0]
    n_seq = tt // GRID_W
    lead = CONV_PAD_ROWS - kw // 2
    rows = GRID_W + 2 * CONV_PAD_ROWS
    zeros = jnp.zeros((CONV_PAD_ROWS, ch), F32)
    for s in range(n_seq):
        up_scr[s, 0:CONV_PAD_ROWS, :] = zeros
        up_scr[s, CONV_PAD_ROWS + GRID_W:, :] = zeros
    for cb in range(nblk):
        u = a_ref[0, cb] * jax.nn.sigmoid(b_ref[0, cb])
        for s in range(n_seq):
            up_scr[s, CONV_PAD_ROWS:CONV_PAD_ROWS + GRID_W, cb * LANES:(cb + 1) * LANES] = (
                u[s * GRID_W:(s + 1) * GRID_W, :])

    cw = sh_scr.shape[2]

    def seq(s, carry):
        row0 = pl.multiple_of(s * GRID_W, GRID_W)
        for c0 in range(0, ch, cw):
            for r in range(SUBLANES):
                sh_scr[r] = up_scr[s, r:r + rows - SUBLANES, c0:c0 + cw]
            acc = jnp.broadcast_to(cb_ref[:, c0:c0 + cw], (GRID_W, cw))
            for k in range(kw):
                a8, r = divmod(lead + k, SUBLANES)
                acc = acc + sh_scr[r, a8 * SUBLANES:a8 * SUBLANES + GRID_W, :] * w_ref[k:k + 1, c0:c0 + cw]
            y_scr[pl.ds(row0, GRID_W), c0:c0 + cw] = acc
        return carry

    lax.fori_loop(0, n_seq, seq, 0)
    y = y_scr[...]
    mu = jnp.mean(y, axis=-1, keepdims=True)
    var = jnp.mean(jnp.square(y - mu), axis=-1, keepdims=True)
    yn = (y - mu) * lax.rsqrt(var + EPS) * lnw_ref[...] + lnb_ref[...]
    o_ref[0] = _silu(yn).astype(o_ref.dtype)


def _conv(proj, col0, conv_w, conv_b, ln_w, ln_b):
    b, _, t, _ = proj.shape
    kw, ch = conv_w.shape
    assert col0 % ch == 0 and ch % LANES == 0 and kw // 2 <= CONV_PAD_ROWS and t % GRID_W == 0
    nblk = ch // LANES
    tt = _tile(t, 8 * GRID_W)
    cw = min(ch, CONV_LANE_CHUNK)
    rows = GRID_W + 2 * CONV_PAD_ROWS
    ca, cb = col0 // ch, col0 // ch + 1
    vec = pl.BlockSpec((1, ch), lambda bi, i: (0, 0))
    return pl.pallas_call(
        _conv_kernel,
        out_shape=jax.ShapeDtypeStruct((b, t, ch), BF16),
        grid=(b, t // tt),
        in_specs=[pl.BlockSpec((1, nblk, tt, LANES), lambda bi, i: (bi, ca, i, 0)),
                  pl.BlockSpec((1, nblk, tt, LANES), lambda bi, i: (bi, cb, i, 0)),
                  pl.BlockSpec((kw, ch), lambda bi, i: (0, 0)),
                  vec, vec, vec],
        out_specs=pl.BlockSpec((1, tt, ch), lambda bi, i: (bi, i, 0)),
        scratch_shapes=[pltpu.VMEM((tt // GRID_W, rows, ch), F32),
                        pltpu.VMEM((SUBLANES, rows - SUBLANES, cw), F32),
                        pltpu.VMEM((tt, ch), F32)],
        compiler_params=_params(("parallel", "parallel")),
        name="conv",
    )(proj, proj, conv_w, conv_b, ln_w, ln_b)


def _outproj_kernel(yr_ref, yc_ref, wr_ref, wc_ref, x_ref, g1_ref, sh2_ref, sc2_ref, pmn_ref, pfn_ref,
                    rw_ref, rb_ref, x1_ref, h2p_ref, lg_ref):
    tm = x_ref.shape[1]
    e_rows = lg_ref.shape[0]
    mix = (jnp.dot(yr_ref[0], wr_ref[...], preferred_element_type=F32)
           + jnp.dot(yc_ref[0], wc_ref[...], preferred_element_type=F32))
    x1 = x_ref[0] + _rms(mix, g1_ref[0] * pmn_ref[...])
    x1_ref[0] = x1
    h2 = _rms(x1, pfn_ref[...] * (1.0 + sc2_ref[0])) + sh2_ref[0]
    h2_hi = h2.astype(BF16)
    h2_lo = (h2 - h2_hi.astype(F32)).astype(BF16)
    by_hi = _dot_nt(rw_ref[...], h2_hi)
    lg_ref[...] = by_hi[:e_rows] + by_hi[e_rows:] + _dot_nt(rw_ref[:e_rows, :], h2_lo) + rb_ref[...]
    _pack_rows(h2, h2p_ref, tm)


def _out_proj(y_ret, y_conv, w_out_bf, x, mod3, post_mix_norm, pre_ffn_norm, rw_hi_lo, rb_col):
    b, t, d = x.shape
    d_ret, d_conv = y_ret.shape[2], y_conv.shape[2]
    e_rows = rb_col.shape[0]
    seg = d // 2 // LANES
    tm = _tile(t, 512)
    per_b = t // tm
    mod = lambda col: pl.BlockSpec((1, 1, d), lambda bi, i: (bi, 0, col))
    vec = pl.BlockSpec((1, d), lambda bi, i: (0, 0))
    row = lambda width: pl.BlockSpec((1, tm, width), lambda bi, i: (bi, i, 0))
    rw = pl.BlockSpec((2 * e_rows, d), lambda bi, i: (0, 0))
    w_ret, w_conv = w_out_bf[:d_ret], w_out_bf[d_ret:]
    return pl.pallas_call(
        _outproj_kernel,
        out_shape=(jax.ShapeDtypeStruct((b, t, d), F32),
                   jax.ShapeDtypeStruct((b * t * seg, LANES), U32),
                   jax.ShapeDtypeStruct((e_rows, b * t), F32)),
        grid=(b, per_b),
        in_specs=[row(d_ret), row(d_conv),
                  pl.BlockSpec((d_ret, d), lambda bi, i: (0, 0)),
                  pl.BlockSpec((d_conv, d), lambda bi, i: (0, 0)),
                  row(d), mod(2), mod(3), mod(4), vec, vec, rw,
                  pl.BlockSpec((e_rows, 1), lambda bi, i: (0, 0))],
        out_specs=(row(d),
                   pl.BlockSpec((tm * seg, LANES), lambda bi, i: (bi * per_b + i, 0)),
                   pl.BlockSpec((e_rows, tm), lambda bi, i: (0, bi * per_b + i))),
        compiler_params=_params(("parallel", "parallel")),
        name="out_proj",
    )(y_ret, y_conv, w_ret, w_conv, x, mod3, mod3, mod3, post_mix_norm, pre_ffn_norm, rw_hi_lo, rb_col)


def _route_kernel(lg_ref, idx_ref, gate_ref, rank_ref, cnt_ref, tri_scr, run_scr, *, n_experts):
    e_rows, tr = lg_ref.shape

    @pl.when(pl.program_id(0) == 0)
    def _():
        r = lax.broadcasted_iota(jnp.int32, (tr, tr), 0)
        c = lax.broadcasted_iota(jnp.int32, (tr, tr), 1)
        tri_scr[...] = jnp.where(r <= c, 1.0, 0.0).astype(BF16)
        run_scr[...] = jnp.zeros_like(run_scr)

    e_iota = lax.broadcasted_iota(jnp.int32, (e_rows, tr), 0)
    neg = jnp.float32(-jnp.inf)
    logits = jnp.where(e_iota < n_experts, lg_ref[...], neg)
    vals, hots = [], []
    for k in range(TOP_K):
        m = jnp.max(logits, axis=0, keepdims=True)
        ik = jnp.min(jnp.where(logits == m, e_iota, e_rows), axis=0, keepdims=True)
        hot = e_iota == ik
        logits = jnp.where(hot, neg, logits)
        vals.append(m)
        hots.append(hot)
        idx_ref[k:k + 1, :] = ik
    exps = [jnp.exp(v - vals[0]) for v in vals]
    den = exps[0]
    for e in exps[1:]:
        den = den + e
    for k in range(TOP_K):
        gate_ref[k:k + 1, :] = exps[k] / den

    sel = jnp.zeros((e_rows, tr), F32)
    for hot in hots:
        sel = sel + jnp.where(hot, 1.0, 0.0)
    csum = jnp.dot(sel.astype(BF16), tri_scr[...], preferred_element_type=F32)
    before = run_scr[:, 0:1] + csum - sel
    for k in range(TOP_K):
        rank_ref[k:k + 1, :] = jnp.sum(jnp.where(hots[k], before, 0.0), axis=0, keepdims=True).astype(jnp.int32)
    run_scr[...] = run_scr[...] + jnp.sum(sel, axis=1, keepdims=True)
    cnt_ref[...] = run_scr[...]


def _route(logits_t, n_experts):
    e_rows, n = logits_t.shape
    tr = _tile(n, 512)
    kt = lambda dt: jax.ShapeDtypeStruct((TOP_K, n), dt)
    blk = pl.BlockSpec((TOP_K, tr), lambda i: (0, i))
    return pl.pallas_call(
        functools.partial(_route_kernel, n_experts=n_experts),
        out_shape=(kt(jnp.int32), kt(F32), kt(jnp.int32), jax.ShapeDtypeStruct((e_rows, LANES), F32)),
        grid=(n // tr,),
        in_specs=[pl.BlockSpec((e_rows, tr), lambda i: (0, i))],
        out_specs=(blk, blk, blk, pl.BlockSpec((e_rows, LANES), lambda i: (0, 0))),
        scratch_shapes=[pltpu.VMEM((tr, tr), BF16), pltpu.VMEM((e_rows, LANES), F32)],
        compiler_params=_params(("arbitrary",)),
        name="route",
    )(logits_t)


def _dispatch_kernel(pos_ref, poff_ref, plen_ref, h_ref, xs_ref, zero_scr, sem, zsem, *, n_tok, n_exp, seg, tm):
    td = h_ref.shape[0] // seg
    step = pl.program_id(0)
    base = step * td

    def pad_copies(e, act):
        off, ln = poff_ref[e], plen_ref[e]
        for bit in range(tm.bit_length() - 1):
            size = 1 << bit

            @pl.when(((ln >> bit) & 1) == 1)
            def _():
                row = pl.multiple_of((off + (ln & (size - 1))) * seg, seg)
                act(pltpu.make_async_copy(zero_scr.at[pl.ds(0, size * seg)],
                                          xs_ref.at[pl.ds(row, size * seg)], zsem))

    def for_pads(act):
        def body(e, carry):
            pad_copies(e, act)
            return carry
        lax.fori_loop(0, n_exp, body, 0)

        half = zero_scr.shape[0]
        end = (poff_ref[n_exp - 1] + plen_ref[n_exp - 1]) * seg

        def tail(n, carry):
            row = pl.multiple_of(end + n * half, seg)
            act(pltpu.make_async_copy(zero_scr, xs_ref.at[pl.ds(row, half)], zsem))
            return carry
        lax.fori_loop(0, (xs_ref.shape[0] - end) // half, tail, 0)

    @pl.when(step == 0)
    def _():
        zero_scr[...] = jnp.zeros_like(zero_scr)
        for_pads(lambda cp: cp.start())

    def row_copy(t, k):
        p = pos_ref[k * n_tok + base + t]
        return pltpu.make_async_copy(h_ref.at[pl.ds(pl.multiple_of(t * seg, seg), seg)],
                                     xs_ref.at[pl.ds(pl.multiple_of(p * seg, seg), seg)], sem)

    def start(t, carry):
        for k in range(TOP_K):
            row_copy(t, k).start()
        return carry

    lax.fori_loop(0, td, start, 0)
    for k in range(TOP_K):
        pltpu.make_async_copy(h_ref, h_ref, sem).wait()

    @pl.when(step == 0)
    def _():
        for_pads(lambda cp: cp.wait())


def _dispatch(pos_flat, pad_off, pad_len, h2p, n_slots, seg, tm):
    n = h2p.shape[0] // seg
    n_exp = pad_off.shape[0]
    assert tm & (tm - 1) == 0
    td = _tile(n, 512)
    return pl.pallas_call(
        functools.partial(_dispatch_kernel, n_tok=n, n_exp=n_exp, seg=seg, tm=tm),
        out_shape=jax.ShapeDtypeStruct((n_slots * seg, LANES), U32),
        grid_spec=pltpu.PrefetchScalarGridSpec(
            num_scalar_prefetch=3,
            grid=(n // td,),
            in_specs=[pl.BlockSpec((td * seg, LANES), lambda i, *_: (i, 0))],
            out_specs=pl.BlockSpec(memory_space=pl.ANY),
            scratch_shapes=[pltpu.VMEM((max(tm // 2, 1) * seg, LANES), U32),
                            pltpu.SemaphoreType.DMA(()), pltpu.SemaphoreType.DMA(())]),
        compiler_params=_params(("arbitrary",)),
        name="dispatch",
    )(pos_flat, pad_off, pad_len, h2p)


def _pack_rows(val, ref, tm):
    half = val.shape[1] // 2
    seg = half // LANES
    vb = val.astype(BF16).astype(F32)
    packed = ((lax.bitcast_convert_type(vb[:, :half], U32) >> 16)
              | (lax.bitcast_convert_type(vb[:, half:], U32) & jnp.uint32(HI16)))
    for s in range(seg):
        ref[pl.ds(s, tm, stride=seg), :] = packed[:, s * LANES:(s + 1) * LANES]


def _unpack_seg(words):
    return (lax.bitcast_convert_type(words << 16, F32),
            lax.bitcast_convert_type(words & jnp.uint32(HI16), F32))


def _weight_runs(tile_expert, n_passes):
    n_tiles = tile_expert.shape[0]
    total = n_tiles * n_passes
    e_lin = jnp.tile(tile_expert, n_passes)
    j_lin = jnp.repeat(jnp.arange(n_passes, dtype=jnp.int32), n_tiles)
    i_lin = jnp.tile(jnp.arange(n_tiles, dtype=jnp.int32), n_passes)
    start = (i_lin == 0) | (e_lin != jnp.roll(e_lin, 1))
    idx = jnp.arange(total, dtype=jnp.int32)
    nxt = jnp.min(jnp.where(start[None, :] & (idx[None, :] > idx[:, None]), idx[None, :], total), axis=1)
    has = nxt < total
    nxt = jnp.minimum(nxt, total - 1)
    return start.astype(jnp.int32), jnp.where(has, e_lin[nxt], -1).astype(jnp.int32), j_lin[nxt]


def _stream_weights(step, first_ref, ne_ref, nj_ref, cur, copies, cast):
    @pl.when(step == 0)
    def _():
        for cp in copies(*cur):
            cp.start()

    @pl.when(first_ref[step] == 1)
    def _():
        for cp in copies(*cur):
            cp.wait()
        cast()

        @pl.when(ne_ref[step] >= 0)
        def _():
            for cp in copies(ne_ref[step], nj_ref[step]):
                cp.start()


def _ffn1_kernel(te_ref, nu_ref, first_ref, ne_ref, nj_ref, tv_ref, xs_ref, w_hbm, bg_ref, bl_ref, act_ref,
                 wg_stage, wl_stage, wg_scr, wl_scr, x_scr, sem, *, seg, d_ff):
    del nu_ref
    j, i = pl.program_id(0), pl.program_id(1)
    tm = x_scr.shape[0]
    tn = wg_scr.shape[1]
    half = seg * LANES

    def copies(e, jj):
        col = pl.multiple_of(jj * tn, tn)
        return (pltpu.make_async_copy(w_hbm.at[e, :, pl.ds(col, tn)], wg_stage, sem.at[0]),
                pltpu.make_async_copy(w_hbm.at[e, :, pl.ds(d_ff + col, tn)], wl_stage, sem.at[1]))

    def cast():
        wg_scr[...] = wg_stage[...].astype(BF16)
        wl_scr[...] = wl_stage[...].astype(BF16)

    _stream_weights(j * pl.num_programs(1) + i, first_ref, ne_ref, nj_ref, (te_ref[i], j), copies, cast)

    rows = tm // ROW_GROUPS
    for r0 in range(0, tm, rows):
        @pl.when(tv_ref[i] > r0)
        def _():
            for s in range(seg):
                lo, hi = _unpack_seg(xs_ref[pl.ds(r0 * seg + s, rows, stride=seg), :])
                x_scr[r0:r0 + rows, s * LANES:(s + 1) * LANES] = lo.astype(BF16)
                x_scr[r0:r0 + rows, half + s * LANES:half + (s + 1) * LANES] = hi.astype(BF16)
            x = x_scr[r0:r0 + rows, :]
            glu = jnp.minimum(jnp.dot(x, wg_scr[...], preferred_element_type=F32) + bg_ref[0], SWIGLU_LIMIT)
            lin = jnp.clip(jnp.dot(x, wl_scr[...], preferred_element_type=F32) + bl_ref[0],
                           -SWIGLU_LIMIT, SWIGLU_LIMIT)
            act_ref[r0:r0 + rows, :] = (glu * jax.nn.sigmoid(SWIGLU_ALPHA * glu) * (lin + 1.0)).astype(act_ref.dtype)

        @pl.when(tv_ref[i] <= r0)
        def _():
            act_ref[r0:r0 + rows, :] = jnp.zeros((rows, tn), act_ref.dtype)


def _ffn1(tile_expert, n_used, tile_valid, xs, w1, b1_3, tm, seg):
    d = seg * LANES * 2
    slots = xs.shape[0] // seg
    d_ff = w1.shape[2] // 2
    tn = _tile(d_ff, 1024)
    nj = d_ff // tn
    n_tiles = slots // tm
    row = lambda i, nu: jnp.minimum(i, nu[0] - 1)
    return pl.pallas_call(
        functools.partial(_ffn1_kernel, seg=seg, d_ff=d_ff),
        out_shape=jax.ShapeDtypeStruct((slots, d_ff), BF16),
        grid_spec=pltpu.PrefetchScalarGridSpec(
            num_scalar_prefetch=6,
            grid=(nj, n_tiles),
            in_specs=[pl.BlockSpec((tm * seg, LANES), lambda j, i, te, nu, *_: (row(i, nu), 0)),
                      pl.BlockSpec(memory_space=pl.ANY),
                      pl.BlockSpec((1, 1, tn), lambda j, i, te, *_: (te[i], 0, j)),
                      pl.BlockSpec((1, 1, tn), lambda j, i, te, *_: (te[i], 0, nj + j))],
            out_specs=pl.BlockSpec((tm, tn), lambda j, i, *_: (i, j)),
            scratch_shapes=[pltpu.VMEM((d, tn), F32), pltpu.VMEM((d, tn), F32),
                            pltpu.VMEM((d, tn), BF16), pltpu.VMEM((d, tn), BF16),
                            pltpu.VMEM((tm, d), BF16), pltpu.SemaphoreType.DMA((2,))]),
        compiler_params=_params(("arbitrary", "arbitrary")),
        name="ffn1",
    )(tile_expert, n_used, *_weight_runs(tile_expert, nj), tile_valid, xs, w1, b1_3, b1_3)


def _ffn2_kernel(te_ref, nu_ref, first_ref, ne_ref, nj_ref, tv_ref, act_ref, w_hbm, b_ref, ys_ref,
                 w_stage, w_scr, sem):
    del nu_ref
    i = pl.program_id(0)
    tm = act_ref.shape[0]

    def copies(e, jj):
        del jj
        return (pltpu.make_async_copy(w_hbm.at[e], w_stage, sem),)

    def cast():
        w_scr[...] = w_stage[...].astype(BF16)

    _stream_weights(i, first_ref, ne_ref, nj_ref, (te_ref[i], 0), copies, cast)

    rows = tm // ROW_GROUPS
    seg = ys_ref.shape[0] // tm
    for r0 in range(0, tm, rows):
        ys_rows = ys_ref.at[pl.ds(r0 * seg, rows * seg)]

        @pl.when(tv_ref[i] > r0)
        def _():
            y = jnp.dot(act_ref[r0:r0 + rows, :], w_scr[...], preferred_element_type=F32) + b_ref[0]
            _pack_rows(y, ys_rows, rows)

        @pl.when(tv_ref[i] <= r0)
        def _():
            ys_rows[...] = jnp.zeros(ys_rows.shape, ys_rows.dtype)


def _ffn2(tile_expert, n_used, tile_valid, act, w2, b2_3, tm):
    slots, d_ff = act.shape
    d = w2.shape[2]
    seg = d // 2 // LANES
    n_tiles = slots // tm
    row = lambda i, nu: jnp.minimum(i, nu[0] - 1)
    return pl.pallas_call(
        _ffn2_kernel,
        out_shape=jax.ShapeDtypeStruct((slots * seg, LANES), U32),
        grid_spec=pltpu.PrefetchScalarGridSpec(
            num_scalar_prefetch=6,
            grid=(n_tiles,),
            in_specs=[pl.BlockSpec((tm, d_ff), lambda i, te, nu, *_: (row(i, nu), 0)),
                      pl.BlockSpec(memory_space=pl.ANY),
                      pl.BlockSpec((1, 1, d), lambda i, te, *_: (te[i], 0, 0))],
            out_specs=pl.BlockSpec((tm * seg, LANES), lambda i, *_: (i, 0)),
            scratch_shapes=[pltpu.VMEM((d_ff, d), F32), pltpu.VMEM((d_ff, d), BF16),
                            pltpu.SemaphoreType.DMA(())]),
        compiler_params=_params(("arbitrary",)),
        name="ffn2",
    )(tile_expert, n_used, *_weight_runs(tile_expert, 1), tile_valid, act, w2, b2_3)


def _combine_kernel(pos_ref, ys_ref, gate_ref, x1_ref, g2_ref, pfn_ref, o_ref, buf, moe_scr, sem, *, n_tok):
    tc, d = x1_ref.shape
    seg = buf.shape[2] // tc
    half = seg * LANES
    step = pl.program_id(0)

    def gather(st, slot):
        def body(t, carry):
            for k in range(TOP_K):
                p = pos_ref[k * n_tok + st * tc + t]
                pltpu.make_async_copy(ys_ref.at[pl.ds(pl.multiple_of(p * seg, seg), seg)],
                                      buf.at[slot, k, pl.ds(pl.multiple_of(t * seg, seg), seg)],
                                      sem.at[slot, k]).start()
            return carry
        lax.fori_loop(0, tc, body, 0)

    @pl.when(step == 0)
    def _():
        gather(0, 0)

    @pl.when(step + 1 < pl.num_programs(0))
    def _():
        gather(step + 1, (step + 1) % 2)

    slot = step % 2
    for k in range(TOP_K):
        pltpu.make_async_copy(buf.at[slot, k], buf.at[slot, k], sem.at[slot, k]).wait()
    gates = [jnp.broadcast_to(gate_ref[:, k:k + 1], (tc, LANES)) for k in range(TOP_K)]
    for s in range(seg):
        m_lo = m_hi = None
        for k in range(TOP_K):
            lo, hi = _unpack_seg(buf[slot, k, pl.ds(s, tc, stride=seg), :])
            m_lo = lo * gates[k] if m_lo is None else m_lo + lo * gates[k]
            m_hi = hi * gates[k] if m_hi is None else m_hi + hi * gates[k]
        moe_scr[:, s * LANES:(s + 1) * LANES] = m_lo
        moe_scr[:, half + s * LANES:half + (s + 1) * LANES] = m_hi
    o_ref[...] = x1_ref[...] + g2_ref[0] * _rms(moe_scr[...], pfn_ref[...])


def _combine(pos_flat, ys, gates_t, x1, mod3, post_ffn_norm, seq):
    n, d = x1.shape
    seg = d // 2 // LANES
    tc = _tile(seq, 256)
    per_batch = seq // tc
    return pl.pallas_call(
        functools.partial(_combine_kernel, n_tok=n),
        out_shape=jax.ShapeDtypeStruct((n, d), F32),
        grid_spec=pltpu.PrefetchScalarGridSpec(
            num_scalar_prefetch=1,
            grid=(n // tc,),
            in_specs=[pl.BlockSpec(memory_space=pl.ANY),
                      pl.BlockSpec((tc, TOP_K), lambda i, pos: (i, 0)),
                      pl.BlockSpec((tc, d), lambda i, pos: (i, 0)),
                      pl.BlockSpec((1, 1, d), lambda i, pos: (i // per_batch, 0, 5)),
                      pl.BlockSpec((1, d), lambda i, pos: (0, 0))],
            out_specs=pl.BlockSpec((tc, d), lambda i, pos: (i, 0)),
            scratch_shapes=[pltpu.VMEM((2, TOP_K, tc * seg, LANES), U32), pltpu.VMEM((tc, d), F32),
                            pltpu.SemaphoreType.DMA((2, TOP_K))]),
        compiler_params=_params(("arbitrary",)),
        name="combine",
    )(pos_flat, ys, gates_t, x1, mod3, post_ffn_norm)


def _moe(h2p, logits_t, x1, mod3, post_ffn_norm, w1, b1, w2, b2, seq, n_exp):
    n, d = x1.shape
    seg = d // 2 // LANES
    tm = 1 << (min(n * TOP_K, 512).bit_length() - 1)
    idx, gates, rank, cnt = _route(logits_t, n_exp)

    counts = cnt[:n_exp, 0].astype(jnp.int32)
    padded = (counts + tm - 1) // tm * tm
    ends = jnp.cumsum(padded)
    starts = ends - padded
    hot = idx[:, :, None] == jnp.arange(n_exp, dtype=jnp.int32)
    pos_flat = (jnp.sum(jnp.where(hot, starts, 0), axis=-1) + rank).reshape(-1)
    n_tiles = -(-n * TOP_K // tm) + n_exp
    n_used = (ends[-1] // tm).astype(jnp.int32)
    tile_start = jnp.minimum(jnp.arange(n_tiles, dtype=jnp.int32), n_used - 1) * tm
    tile_expert = jnp.sum(tile_start[:, None] >= ends[None, :], axis=1).astype(jnp.int32)
    tile_hot = tile_expert[:, None] == jnp.arange(n_exp, dtype=jnp.int32)
    tile_rows_left = jnp.sum(jnp.where(tile_hot, starts + counts, 0), axis=1) - tile_start
    tile_valid = jnp.where(jnp.arange(n_tiles) < n_used, jnp.clip(tile_rows_left, 0, tm), 0).astype(jnp.int32)
    n_used = n_used.reshape(1)

    xs = _dispatch(pos_flat, starts + counts, padded - counts, h2p, n_tiles * tm, seg, tm)
    act = _ffn1(tile_expert, n_used, tile_valid, xs, w1, b1[:, None, :], tm, seg)
    ys = _ffn2(tile_expert, n_used, tile_valid, act, w2, b2[:, None, :], tm)
    return _combine(pos_flat, ys, gates.T, x1, mod3, post_ffn_norm, seq)


def kernel(x, c, ctx, c_ctx, ada_w, ada_b, pre_mix_norm, post_mix_norm, pre_ffn_norm, post_ffn_norm,
           w_in, ret_decay_fwd, ret_decay_bwd, ret_gn_w, conv_w, conv_b, conv_ln_w, conv_ln_b, w_out,
           router_w, router_b, w1, b1, w2, b2):
    assert ada_w.shape[0] == 1, "single-layer stack only"
    b, t, d = x.shape
    n_ctx = ctx.shape[1]
    n_heads = ret_decay_fwd.shape[1]
    d_ret = n_heads * HEAD_DIM
    n_exp = router_w.shape[2]
    assert b < MOD_ROWS and ret_gn_w.shape[1] == d_ret and d % (2 * LANES) == 0

    cc = jnp.zeros((MOD_ROWS, d), F32).at[:b].set(c).at[b].set(c_ctx)
    mod3 = _ada(cc, ada_w[0], ada_b)[:, None, :]
    cos, sin = _rope_tables(n_ctx + t)

    w_in_bf = w_in[0].astype(BF16)
    proj = _in_proj(x, pre_mix_norm, mod3, lambda bi: bi, w_in_bf, 0, w_in.shape[2])
    ctx_kv = _in_proj(ctx.reshape(1, b * n_ctx, d), pre_mix_norm, mod3, lambda bi: b, w_in_bf, d_ret, 2 * d_ret)

    lane_bcast = lambda v: jnp.broadcast_to(v.reshape(n_heads, 1, 1), (n_heads, 1, LANES))
    y_ret = _retention(proj, ctx_kv, cos, sin, lane_bcast(ret_decay_fwd[0]), lane_bcast(ret_decay_bwd[0]),
                       ret_gn_w, n_heads, n_ctx)
    y_conv = _conv(proj, 4 * d_ret, conv_w[0], conv_b, conv_ln_w, conv_ln_b)

    e_rows = -(-n_exp // SUBLANES) * SUBLANES
    rw_t = jnp.zeros((e_rows, d), F32).at[:n_exp].set(router_w[0].T)
    rw_hi = rw_t.astype(BF16)
    rw_lo = (rw_t - rw_hi.astype(F32)).astype(BF16)
    rb_col = jnp.zeros((e_rows, 1), F32).at[:n_exp, 0].set(router_b[0])
    x1, h2p, logits_t = _out_proj(y_ret, y_conv, w_out[0].astype(BF16), x, mod3, post_mix_norm, pre_ffn_norm,
                                  jnp.concatenate([rw_hi, rw_lo], axis=0), rb_col)

    out = _moe(h2p, logits_t, x1.reshape(b * t, d), mod3, post_ffn_norm, w1[0], b1[0], w2[0], b2[0], t, n_exp)
    return out.reshape(b, t, d)
```

```python
import functools
import math

import jax
import jax.numpy as jnp
from jax import lax
from jax.experimental import pallas as pl
from jax.experimental.pallas import tpu as pltpu

F32 = jnp.float32
BF16 = jnp.bfloat16
U32 = jnp.uint32

GRID_W = 64
HEAD_DIM = 128
RET_CHUNK = 256
ROPE_BASE = 10000.0
TOP_K = 4
SWIGLU_ALPHA = 1.702
SWIGLU_LIMIT = 7.0
EPS = 1e-6
GN_EPS = 1e-5
LANES = 128
SUBLANES = 8
MOD_ROWS = 16
VMEM_LIMIT = 56 * 1024 * 1024
HI16 = 0xFFFF0000
ROW_GROUPS = 1


def _tile(n, pref):
    t = min(n, pref)
    while n % t:
        t -= 1
    return t


def _params(sem, vmem=VMEM_LIMIT):
    return pltpu.CompilerParams(dimension_semantics=sem, vmem_limit_bytes=vmem)


def _rms(x, w):
    return x * lax.rsqrt(jnp.mean(x * x, axis=-1, keepdims=True) + EPS) * w


def _silu(x):
    return x * jax.nn.sigmoid(x)


def _dot_nt(a, b):
    return lax.dot_general(a, b, (((1,), (1,)), ((), ())), preferred_element_type=F32)


def _dot_tn(a, b):
    return lax.dot_general(a, b, (((0,), (0,)), ((), ())), preferred_element_type=F32)


def _ada_kernel(c_ref, w_ref, b_ref, o_ref):
    s = _silu(c_ref[...])
    w = w_ref[...]
    s_hi, w_hi = s.astype(BF16), w.astype(BF16)
    s_lo = (s - s_hi.astype(F32)).astype(BF16)
    w_lo = (w - w_hi.astype(F32)).astype(BF16)
    o_ref[...] = (jnp.dot(s_hi, w_hi, preferred_element_type=F32) + jnp.dot(s_lo, w_hi, preferred_element_type=F32)
                  + jnp.dot(s_hi, w_lo, preferred_element_type=F32) + b_ref[...])


def _ada(cc, w, b):
    d, n = w.shape
    tn = _tile(n, 1024)
    return pl.pallas_call(
        _ada_kernel,
        out_shape=jax.ShapeDtypeStruct((MOD_ROWS, n), F32),
        grid=(n // tn,),
        in_specs=[pl.BlockSpec((MOD_ROWS, d), lambda j: (0, 0)),
                  pl.BlockSpec((d, tn), lambda j: (0, j)),
                  pl.BlockSpec((1, tn), lambda j: (0, j))],
        out_specs=pl.BlockSpec((MOD_ROWS, tn), lambda j: (0, j)),
        compiler_params=_params(("parallel",)),
        name="ada",
    )(cc, w, b)


def _rope_kernel(cos_ref, sin_ref):
    p, _ = cos_ref.shape
    half = HEAD_DIM // 2
    lane = lax.broadcasted_iota(jnp.int32, (p, HEAD_DIM), 1)
    pos = lax.broadcasted_iota(jnp.int32, (p, HEAD_DIM), 0).astype(F32)
    j = jnp.where(lane < half, lane, lane - half).astype(F32)
    inv = jnp.exp(j * (-jnp.log(ROPE_BASE) / half))
    ang = pos * inv
    cos_ref[...] = jnp.cos(ang)
    sin_ref[...] = jnp.where(lane < half, -1.0, 1.0) * jnp.sin(ang)


def _rope_tables(p):
    return pl.pallas_call(
        _rope_kernel,
        out_shape=(jax.ShapeDtypeStruct((p, HEAD_DIM), F32), jax.ShapeDtypeStruct((p, HEAD_DIM), F32)),
        name="rope",
    )()


def _rot(t, cos, sin_signed):
    return t * cos + pltpu.roll(t, HEAD_DIM // 2, axis=1) * sin_signed


def _inproj_kernel(x_ref, nw_ref, sh_ref, sc_ref, w_ref, o_ref, h_scr):
    @pl.when(pl.program_id(2) == 0)
    def _():
        rows = math.gcd(x_ref.shape[1], LANES)
        gain = nw_ref[...] * (1.0 + sc_ref[0])
        shift = sh_ref[0]

        def body(r, carry):
            sl = pl.ds(pl.multiple_of(r * rows, rows), rows)
            x = x_ref[0, sl, :]
            rs = lax.rsqrt(jnp.mean(x * x, axis=-1, keepdims=True) + EPS)
            h_scr[sl, :] = (x * rs * gain + shift).astype(BF16)
            return carry
        lax.fori_loop(0, x_ref.shape[1] // rows, body, 0)

    res = jnp.dot(h_scr[...], w_ref[...], preferred_element_type=F32)
    for cb in range(o_ref.shape[1]):
        o_ref[0, cb] = res[:, cb * LANES:(cb + 1) * LANES]


def _in_proj(x, norm_w, mod3, mod_row, w_bf, col0, ncols):
    b, t, d = x.shape
    tm = _tile(t, 1024)
    tn = _tile(math.gcd(ncols, col0), 1536)
    joff = col0 // tn
    nblk = tn // LANES
    return pl.pallas_call(
        _inproj_kernel,
        out_shape=jax.ShapeDtypeStruct((b, ncols // LANES, t, LANES), F32),
        grid=(b, t // tm, ncols // tn),
        in_specs=[pl.BlockSpec((1, tm, d), lambda bi, i, j: (bi, i, 0)),
                  pl.BlockSpec((1, d), lambda bi, i, j: (0, 0)),
                  pl.BlockSpec((1, 1, d), lambda bi, i, j: (mod_row(bi), 0, 0)),
                  pl.BlockSpec((1, 1, d), lambda bi, i, j: (mod_row(bi), 0, 1)),
                  pl.BlockSpec((d, tn), lambda bi, i, j: (0, j + joff))],
        out_specs=pl.BlockSpec((1, nblk, tm, LANES), lambda bi, i, j: (bi, j, i, 0)),
        scratch_shapes=[pltpu.VMEM((tm, d), BF16)],
        compiler_params=_params(("parallel", "parallel", "arbitrary")),
        name="in_proj",
    )(x, norm_w, mod3, mod3, w_bf)


def _log_sigmoid(x):
    return jnp.minimum(x, 0.0) - jnp.log(1.0 + jnp.exp(-jnp.abs(x)))


def _ret_kernel(q_ref, k_ref, v_ref, g_ref, kc_ref, vc_ref, cos_ref, sin_ref, decf_ref, decb_ref,
                gnw_ref, o_ref, q_scr, k_scr, kf_scr, kb_scr, v_scr, o_scr, din_scr, kvf_scr, kvb_scr,
                rf_scr, rb_scr, *, n_ctx):
    t = q_ref.shape[2]
    c = din_scr.shape[0]
    nc = t // c
    lgf = _log_sigmoid(decf_ref[0])[:, 0:1]
    lgb = _log_sigmoid(decb_ref[0])[:, 0:1]

    kc = _rot(kc_ref[0, 0], cos_ref[0:n_ctx, :], sin_ref[0:n_ctx, :])
    vc = vc_ref[0, 0].astype(BF16)
    tc = lax.broadcasted_iota(jnp.int32, (n_ctx, 1), 0).astype(F32)
    rf = _dot_tn((kc * jnp.exp(lgf * (n_ctx - 1.0 - tc))).astype(BF16), vc)
    rb = _dot_tn((kc * jnp.exp(lgb * tc)).astype(BF16), vc)

    scale = HEAD_DIM ** -0.5
    cos, sin = cos_ref[n_ctx:n_ctx + t, :], sin_ref[n_ctx:n_ctx + t, :]
    q_scr[...] = (_rot(q_ref[0, 0], cos, sin) * scale).astype(BF16)
    kr = _rot(k_ref[0, 0], cos, sin)
    ri_all = (lax.broadcasted_iota(jnp.int32, (t, HEAD_DIM), 0) % c).astype(F32)
    k_scr[...] = kr.astype(BF16)
    kf_scr[...] = (kr * jnp.exp(lgf * (c - 1.0 - ri_all))).astype(BF16)
    kb_scr[...] = (kr * jnp.exp(lgb * ri_all)).astype(BF16)
    v_scr[...] = v_ref[0, 0].astype(BF16)

    diff = (lax.broadcasted_iota(jnp.int32, (c, c), 0) - lax.broadcasted_iota(jnp.int32, (c, c), 1)).astype(F32)
    din_scr[...] = jnp.where(diff > 0, jnp.exp(lgf * jnp.maximum(diff, 0.0)),
                             jnp.where(diff < 0, jnp.exp(lgb * jnp.maximum(-diff, 0.0)), 2.0))

    for i in range(nc):
        sl = slice(i * c, (i + 1) * c)
        vi = v_scr[sl, :]
        scores = _dot_nt(q_scr[sl, :], k_scr[sl, :]) * din_scr[...]
        o_scr[sl, :] = jnp.dot(scores.astype(BF16), vi, preferred_element_type=F32)
        kvf_scr[i] = _dot_tn(kf_scr[sl, :], vi)
        kvb_scr[i] = _dot_tn(kb_scr[sl, :], vi)

    gc_f = jnp.exp(lgf * float(c))
    gc_b = jnp.exp(lgb * float(c))
    for i in range(nc):
        rf_scr[i] = rf.astype(BF16)
        rf = rf * gc_f + kvf_scr[i]
    for i in reversed(range(nc)):
        rb_scr[i] = rb.astype(BF16)
        rb = rb * gc_b + kvb_scr[i]

    ri = lax.broadcasted_iota(jnp.int32, (c, HEAD_DIM), 0).astype(F32)
    dq_f = jnp.exp(lgf * (ri + 1.0))
    dq_b = jnp.exp(lgb * (c - ri))
    for i in range(nc):
        sl = slice(i * c, (i + 1) * c)
        qi = q_scr[sl, :]
        o = (o_scr[sl, :] + jnp.dot(qi, rf_scr[i], preferred_element_type=F32) * dq_f
             + jnp.dot(qi, rb_scr[i], preferred_element_type=F32) * dq_b)
        mu = jnp.mean(o, axis=-1, keepdims=True)
        var = jnp.mean(jnp.square(o - mu), axis=-1, keepdims=True)
        on = (o - mu) * lax.rsqrt(var + GN_EPS) * gnw_ref[...]
        o_ref[0, sl, :] = (on * _silu(g_ref[0, 0, sl, :])).astype(o_ref.dtype)


def _retention(proj, ctx_kv, cos, sin, decf3, decb3, gn_w, n_heads, n_ctx):
    b, _, t, hd = proj.shape
    c = _tile(t, RET_CHUNK)
    nc = t // c
    lat = lambda g: pl.BlockSpec((1, 1, t, hd), lambda bi, h: (bi, g * n_heads + h, 0, 0))
    seq = lambda dt: pltpu.VMEM((t, hd), dt)
    return pl.pallas_call(
        functools.partial(_ret_kernel, n_ctx=n_ctx),
        out_shape=jax.ShapeDtypeStruct((b, t, n_heads * hd), BF16),
        grid=(b, n_heads),
        in_specs=[lat(0), lat(1), lat(2), lat(3),
                  pl.BlockSpec((1, 1, n_ctx, hd), lambda bi, h: (0, h, bi, 0)),
                  pl.BlockSpec((1, 1, n_ctx, hd), lambda bi, h: (0, n_heads + h, bi, 0)),
                  pl.BlockSpec(cos.shape, lambda bi, h: (0, 0)),
                  pl.BlockSpec(sin.shape, lambda bi, h: (0, 0)),
                  pl.BlockSpec((1, 1, LANES), lambda bi, h: (h, 0, 0)),
                  pl.BlockSpec((1, 1, LANES), lambda bi, h: (h, 0, 0)),
                  pl.BlockSpec((1, hd), lambda bi, h: (0, h))],
        out_specs=pl.BlockSpec((1, t, hd), lambda bi, h: (bi, 0, h)),
        scratch_shapes=[seq(BF16), seq(BF16), seq(BF16), seq(BF16), seq(BF16), seq(F32),
                        pltpu.VMEM((c, c), F32),
                        pltpu.VMEM((nc, hd, hd), F32), pltpu.VMEM((nc, hd, hd), F32),
                        pltpu.VMEM((nc, hd, hd), BF16), pltpu.VMEM((nc, hd, hd), BF16)],
        compiler_params=_params(("parallel", "parallel")),
        name="ret",
    )(proj, proj, proj, proj, ctx_kv, ctx_kv, cos, sin, decf3, decb3, gn_w)


CONV_PAD_ROWS = 16
CONV_LANE_CHUNK = 256


def _conv_kernel(a_ref, b_ref, w_ref, cb_ref, lnw_ref, lnb_ref, o_ref, up_scr, sh_scr, y_scr):
    nblk, tt = a_ref.shape[1], a_ref.shape[2]
    ch = nblk * LANES
    kw = w_ref.shape[0]
    n_seq = tt // GRID_W
    lead = CONV_PAD_ROWS - kw // 2
    rows = GRID_W + 2 * CONV_PAD_ROWS
    zeros = jnp.zeros((CONV_PAD_ROWS, ch), F32)
    for s in range(n_seq):
        up_scr[s, 0:CONV_PAD_ROWS, :] = zeros
        up_scr[s, CONV_PAD_ROWS + GRID_W:, :] = zeros
    for cb in range(nblk):
        u = a_ref[0, cb] * jax.nn.sigmoid(b_ref[0, cb])
        for s in range(n_seq):
            up_scr[s, CONV_PAD_ROWS:CONV_PAD_ROWS + GRID_W, cb * LANES:(cb + 1) * LANES] = (
                u[s * GRID_W:(s + 1) * GRID_W, :])

    cw = sh_scr.shape[2]

    def seq(s, carry):
        row0 = pl.multiple_of(s * GRID_W, GRID_W)
        for c0 in range(0, ch, cw):
            for r in range(SUBLANES):
                sh_scr[r] = up_scr[s, r:r + rows - SUBLANES, c0:c0 + cw]
            acc = jnp.broadcast_to(cb_ref[:, c0:c0 + cw], (GRID_W, cw))
            for k in range(kw):
                a8, r = divmod(lead + k, SUBLANES)
                acc = acc + sh_scr[r, a8 * SUBLANES:a8 * SUBLANES + GRID_W, :] * w_ref[k:k + 1, c0:c0 + cw]
            y_scr[pl.ds(row0, GRID_W), c0:c0 + cw] = acc
        return carry

    lax.fori_loop(0, n_seq, seq, 0)
    y = y_scr[...]
    mu = jnp.mean(y, axis=-1, keepdims=True)
    var = jnp.mean(jnp.square(y - mu), axis=-1, keepdims=True)
    yn = (y - mu) * lax.rsqrt(var + EPS) * lnw_ref[...] + lnb_ref[...]
    o_ref[0] = _silu(yn).astype(o_ref.dtype)


def _conv(proj, col0, conv_w, conv_b, ln_w, ln_b):
    b, _, t, _ = proj.shape
    kw, ch = conv_w.shape
    assert col0 % ch == 0 and ch % LANES == 0 and kw // 2 <= CONV_PAD_ROWS and t % GRID_W == 0
    nblk = ch // LANES
    tt = _tile(t, 8 * GRID_W)
    cw = min(ch, CONV_LANE_CHUNK)
    rows = GRID_W + 2 * CONV_PAD_ROWS
    ca, cb = col0 // ch, col0 // ch + 1
    vec = pl.BlockSpec((1, ch), lambda bi, i: (0, 0))
    return pl.pallas_call(
        _conv_kernel,
        out_shape=jax.ShapeDtypeStruct((b, t, ch), BF16),
        grid=(b, t // tt),
        in_specs=[pl.BlockSpec((1, nblk, tt, LANES), lambda bi, i: (bi, ca, i, 0)),
                  pl.BlockSpec((1, nblk, tt, LANES), lambda bi, i: (bi, cb, i, 0)),
                  pl.BlockSpec((kw, ch), lambda bi, i: (0, 0)),
                  vec, vec, vec],
        out_specs=pl.BlockSpec((1, tt, ch), lambda bi, i: (bi, i, 0)),
        scratch_shapes=[pltpu.VMEM((tt // GRID_W, rows, ch), F32),
                        pltpu.VMEM((SUBLANES, rows - SUBLANES, cw), F32),
                        pltpu.VMEM((tt, ch), F32)],
        compiler_params=_params(("parallel", "parallel")),
        name="conv",
    )(proj, proj, conv_w, conv_b, ln_w, ln_b)


def _outproj_kernel(yr_ref, yc_ref, wr_ref, wc_ref, x_ref, g1_ref, sh2_ref, sc2_ref, pmn_ref, pfn_ref,
                    rw_ref, rb_ref, x1_ref, h2p_ref, lg_ref):
    tm = x_ref.shape[1]
    e_rows = lg_ref.shape[0]
    mix = (jnp.dot(yr_ref[0], wr_ref[...], preferred_element_type=F32)
           + jnp.dot(yc_ref[0], wc_ref[...], preferred_element_type=F32))
    x1 = x_ref[0] + _rms(mix, g1_ref[0] * pmn_ref[...])
    x1_ref[0] = x1
    h2 = _rms(x1, pfn_ref[...] * (1.0 + sc2_ref[0])) + sh2_ref[0]
    h2_hi = h2.astype(BF16)
    h2_lo = (h2 - h2_hi.astype(F32)).astype(BF16)
    by_hi = _dot_nt(rw_ref[...], h2_hi)
    lg_ref[...] = by_hi[:e_rows] + by_hi[e_rows:] + _dot_nt(rw_ref[:e_rows, :], h2_lo) + rb_ref[...]
    _pack_rows(h2, h2p_ref, tm)


def _out_proj(y_ret, y_conv, w_out_bf, x, mod3, post_mix_norm, pre_ffn_norm, rw_hi_lo, rb_col):
    b, t, d = x.shape
    d_ret, d_conv = y_ret.shape[2], y_conv.shape[2]
    e_rows = rb_col.shape[0]
    seg = d // 2 // LANES
    tm = _tile(t, 512)
    per_b = t // tm
    mod = lambda col: pl.BlockSpec((1, 1, d), lambda bi, i: (bi, 0, col))
    vec = pl.BlockSpec((1, d), lambda bi, i: (0, 0))
    row = lambda width: pl.BlockSpec((1, tm, width), lambda bi, i: (bi, i, 0))
    rw = pl.BlockSpec((2 * e_rows, d), lambda bi, i: (0, 0))
    w_ret, w_conv = w_out_bf[:d_ret], w_out_bf[d_ret:]
    return pl.pallas_call(
        _outproj_kernel,
        out_shape=(jax.ShapeDtypeStruct((b, t, d), F32),
                   jax.ShapeDtypeStruct((b * t * seg, LANES), U32),
                   jax.ShapeDtypeStruct((e_rows, b * t), F32)),
        grid=(b, per_b),
        in_specs=[row(d_ret), row(d_conv),
                  pl.BlockSpec((d_ret, d), lambda bi, i: (0, 0)),
                  pl.BlockSpec((d_conv, d), lambda bi, i: (0, 0)),
                  row(d), mod(2), mod(3), mod(4), vec, vec, rw,
                  pl.BlockSpec((e_rows, 1), lambda bi, i: (0, 0))],
        out_specs=(row(d),
                   pl.BlockSpec((tm * seg, LANES), lambda bi, i: (bi * per_b + i, 0)),
                   pl.BlockSpec((e_rows, tm), lambda bi, i: (0, bi * per_b + i))),
        compiler_params=_params(("parallel", "parallel")),
        name="out_proj",
    )(y_ret, y_conv, w_ret, w_conv, x, mod3, mod3, mod3, post_mix_norm, pre_ffn_norm, rw_hi_lo, rb_col)


def _route_kernel(lg_ref, idx_ref, gate_ref, rank_ref, cnt_ref, tri_scr, run_scr, *, n_experts):
    e_rows, tr = lg_ref.shape

    @pl.when(pl.program_id(0) == 0)
    def _():
        r = lax.broadcasted_iota(jnp.int32, (tr, tr), 0)
        c = lax.broadcasted_iota(jnp.int32, (tr, tr), 1)
        tri_scr[...] = jnp.where(r <= c, 1.0, 0.0).astype(BF16)
        run_scr[...] = jnp.zeros_like(run_scr)

    e_iota = lax.broadcasted_iota(jnp.int32, (e_rows, tr), 0)
    neg = jnp.float32(-jnp.inf)
    logits = jnp.where(e_iota < n_experts, lg_ref[...], neg)
    vals, hots = [], []
    for k in range(TOP_K):
        m = jnp.max(logits, axis=0, keepdims=True)
        ik = jnp.min(jnp.where(logits == m, e_iota, e_rows), axis=0, keepdims=True)
        hot = e_iota == ik
        logits = jnp.where(hot, neg, logits)
        vals.append(m)
        hots.append(hot)
        idx_ref[k:k + 1, :] = ik
    exps = [jnp.exp(v - vals[0]) for v in vals]
    den = exps[0]
    for e in exps[1:]:
        den = den + e
    for k in range(TOP_K):
        gate_ref[k:k + 1, :] = exps[k] / den

    sel = jnp.zeros((e_rows, tr), F32)
    for hot in hots:
        sel = sel + jnp.where(hot, 1.0, 0.0)
    csum = jnp.dot(sel.astype(BF16), tri_scr[...], preferred_element_type=F32)
    before = run_scr[:, 0:1] + csum - sel
    for k in range(TOP_K):
        rank_ref[k:k + 1, :] = jnp.sum(jnp.where(hots[k], before, 0.0), axis=0, keepdims=True).astype(jnp.int32)
    run_scr[...] = run_scr[...] + jnp.sum(sel, axis=1, keepdims=True)
    cnt_ref[...] = run_scr[...]


def _route(logits_t, n_experts):
    e_rows, n = logits_t.shape
    tr = _tile(n, 512)
    kt = lambda dt: jax.ShapeDtypeStruct((TOP_K, n), dt)
    blk = pl.BlockSpec((TOP_K, tr), lambda i: (0, i))
    return pl.pallas_call(
        functools.partial(_route_kernel, n_experts=n_experts),
        out_shape=(kt(jnp.int32), kt(F32), kt(jnp.int32), jax.ShapeDtypeStruct((e_rows, LANES), F32)),
        grid=(n // tr,),
        in_specs=[pl.BlockSpec((e_rows, tr), lambda i: (0, i))],
        out_specs=(blk, blk, blk, pl.BlockSpec((e_rows, LANES), lambda i: (0, 0))),
        scratch_shapes=[pltpu.VMEM((tr, tr), BF16), pltpu.VMEM((e_rows, LANES), F32)],
        compiler_params=_params(("arbitrary",)),
        name="route",
    )(logits_t)


def _dispatch_kernel(pos_ref, poff_ref, plen_ref, h_ref, xs_ref, zero_scr, sem, zsem, *, n_tok, n_exp, seg, tm):
    td = h_ref.shape[0] // seg
    step = pl.program_id(0)
    base = step * td

    def pad_copies(e, act):
        off, ln = poff_ref[e], plen_ref[e]
        for bit in range(tm.bit_length() - 1):
            size = 1 << bit

            @pl.when(((ln >> bit) & 1) == 1)
            def _():
                row = pl.multiple_of((off + (ln & (size - 1))) * seg, seg)
                act(pltpu.make_async_copy(zero_scr.at[pl.ds(0, size * seg)],
                                          xs_ref.at[pl.ds(row, size * seg)], zsem))

    def for_pads(act):
        def body(e, carry):
            pad_copies(e, act)
            return carry
        lax.fori_loop(0, n_exp, body, 0)

        half = zero_scr.shape[0]
        end = (poff_ref[n_exp - 1] + plen_ref[n_exp - 1]) * seg

        def tail(n, carry):
            row = pl.multiple_of(end + n * half, seg)
            act(pltpu.make_async_copy(zero_scr, xs_ref.at[pl.ds(row, half)], zsem))
            return carry
        lax.fori_loop(0, (xs_ref.shape[0] - end) // half, tail, 0)

    @pl.when(step == 0)
    def _():
        zero_scr[...] = jnp.zeros_like(zero_scr)
        for_pads(lambda cp: cp.start())

    def row_copy(t, k):
        p = pos_ref[k * n_tok + base + t]
        return pltpu.make_async_copy(h_ref.at[pl.ds(pl.multiple_of(t * seg, seg), seg)],
                                     xs_ref.at[pl.ds(pl.multiple_of(p * seg, seg), seg)], sem)

    def start(t, carry):
        for k in range(TOP_K):
            row_copy(t, k).start(priority=k % 2)
        return carry

    lax.fori_loop(0, td, start, 0)
    for k in range(TOP_K):
        pltpu.make_async_copy(h_ref, h_ref, sem).wait()

    @pl.when(step == 0)
    def _():
        for_pads(lambda cp: cp.wait())


def _dispatch(pos_flat, pad_off, pad_len, h2p, n_slots, seg, tm):
    n = h2p.shape[0] // seg
    n_exp = pad_off.shape[0]
    assert tm & (tm - 1) == 0
    td = _tile(n, 512)
    return pl.pallas_call(
        functools.partial(_dispatch_kernel, n_tok=n, n_exp=n_exp, seg=seg, tm=tm),
        out_shape=jax.ShapeDtypeStruct((n_slots * seg, LANES), U32),
        grid_spec=pltpu.PrefetchScalarGridSpec(
            num_scalar_prefetch=3,
            grid=(n // td,),
            in_specs=[pl.BlockSpec((td * seg, LANES), lambda i, *_: (i, 0))],
            out_specs=pl.BlockSpec(memory_space=pl.ANY),
            scratch_shapes=[pltpu.VMEM((max(tm // 2, 1) * seg, LANES), U32),
                            pltpu.SemaphoreType.DMA(()), pltpu.SemaphoreType.DMA(())]),
        compiler_params=_params(("arbitrary",)),
        name="dispatch",
    )(pos_flat, pad_off, pad_len, h2p)


def _pack_rows(val, ref, tm):
    half = val.shape[1] // 2
    seg = half // LANES
    vb = val.astype(BF16).astype(F32)
    packed = ((lax.bitcast_convert_type(vb[:, :half], U32) >> 16)
              | (lax.bitcast_convert_type(vb[:, half:], U32) & jnp.uint32(HI16)))
    for s in range(seg):
        ref[pl.ds(s, tm, stride=seg), :] = packed[:, s * LANES:(s + 1) * LANES]


def _unpack_seg(words):
    return (lax.bitcast_convert_type(words << 16, F32),
            lax.bitcast_convert_type(words & jnp.uint32(HI16), F32))


def _weight_runs(tile_expert, n_passes):
    n_tiles = tile_expert.shape[0]
    total = n_tiles * n_passes
    e_lin = jnp.tile(tile_expert, n_passes)
    j_lin = jnp.repeat(jnp.arange(n_passes, dtype=jnp.int32), n_tiles)
    i_lin = jnp.tile(jnp.arange(n_tiles, dtype=jnp.int32), n_passes)
    start = (i_lin == 0) | (e_lin != jnp.roll(e_lin, 1))
    idx = jnp.arange(total, dtype=jnp.int32)
    nxt = jnp.min(jnp.where(start[None, :] & (idx[None, :] > idx[:, None]), idx[None, :], total), axis=1)
    has = nxt < total
    nxt = jnp.minimum(nxt, total - 1)
    return start.astype(jnp.int32), jnp.where(has, e_lin[nxt], -1).astype(jnp.int32), j_lin[nxt]


def _stream_weights(step, first_ref, ne_ref, nj_ref, cur, copies, cast):
    @pl.when(step == 0)
    def _():
        for cp in copies(*cur):
            cp.start()

    @pl.when(first_ref[step] == 1)
    def _():
        for cp in copies(*cur):
            cp.wait()
        cast()

        @pl.when(ne_ref[step] >= 0)
        def _():
            for cp in copies(ne_ref[step], nj_ref[step]):
                cp.start()


def _ffn1_kernel(te_ref, nu_ref, first_ref, ne_ref, nj_ref, tv_ref, xs_ref, w_hbm, bg_ref, bl_ref, act_ref,
                 wg_stage, wl_stage, wg_scr, wl_scr, x_scr, sem, *, seg, d_ff):
    del nu_ref
    j, i = pl.program_id(0), pl.program_id(1)
    tm = x_scr.shape[0]
    tn = wg_scr.shape[1]
    half = seg * LANES

    def copies(e, jj):
        col = pl.multiple_of(jj * tn, tn)
        return (pltpu.make_async_copy(w_hbm.at[e, :, pl.ds(col, tn)], wg_stage, sem.at[0]),
                pltpu.make_async_copy(w_hbm.at[e, :, pl.ds(d_ff + col, tn)], wl_stage, sem.at[1]))

    def cast():
        wg_scr[...] = wg_stage[...].astype(BF16)
        wl_scr[...] = wl_stage[...].astype(BF16)

    _stream_weights(j * pl.num_programs(1) + i, first_ref, ne_ref, nj_ref, (te_ref[i], j), copies, cast)

    rows = tm // ROW_GROUPS
    for r0 in range(0, tm, rows):
        @pl.when(tv_ref[i] > r0)
        def _():
            for s in range(seg):
                lo, hi = _unpack_seg(xs_ref[pl.ds(r0 * seg + s, rows, stride=seg), :])
                x_scr[r0:r0 + rows, s * LANES:(s + 1) * LANES] = lo.astype(BF16)
                x_scr[r0:r0 + rows, half + s * LANES:half + (s + 1) * LANES] = hi.astype(BF16)
            x = x_scr[r0:r0 + rows, :]
            glu = jnp.minimum(jnp.dot(x, wg_scr[...], preferred_element_type=F32) + bg_ref[0], SWIGLU_LIMIT)
            lin = jnp.clip(jnp.dot(x, wl_scr[...], preferred_element_type=F32) + bl_ref[0],
                           -SWIGLU_LIMIT, SWIGLU_LIMIT)
            act_ref[r0:r0 + rows, :] = (glu * jax.nn.sigmoid(SWIGLU_ALPHA * glu) * (lin + 1.0)).astype(act_ref.dtype)

        @pl.when(tv_ref[i] <= r0)
        def _():
            act_ref[r0:r0 + rows, :] = jnp.zeros((rows, tn), act_ref.dtype)


def _ffn1(tile_expert, n_used, tile_valid, xs, w1, b1_3, tm, seg):
    d = seg * LANES * 2
    slots = xs.shape[0] // seg
    d_ff = w1.shape[2] // 2
    tn = _tile(d_ff, 1024)
    nj = d_ff // tn
    n_tiles = slots // tm
    row = lambda i, nu: jnp.minimum(i, nu[0] - 1)
    return pl.pallas_call(
        functools.partial(_ffn1_kernel, seg=seg, d_ff=d_ff),
        out_shape=jax.ShapeDtypeStruct((slots, d_ff), BF16),
        grid_spec=pltpu.PrefetchScalarGridSpec(
            num_scalar_prefetch=6,
            grid=(nj, n_tiles),
            in_specs=[pl.BlockSpec((tm * seg, LANES), lambda j, i, te, nu, *_: (row(i, nu), 0)),
                      pl.BlockSpec(memory_space=pl.ANY),
                      pl.BlockSpec((1, 1, tn), lambda j, i, te, *_: (te[i], 0, j)),
                      pl.BlockSpec((1, 1, tn), lambda j, i, te, *_: (te[i], 0, nj + j))],
            out_specs=pl.BlockSpec((tm, tn), lambda j, i, *_: (i, j)),
            scratch_shapes=[pltpu.VMEM((d, tn), F32), pltpu.VMEM((d, tn), F32),
                            pltpu.VMEM((d, tn), BF16), pltpu.VMEM((d, tn), BF16),
                            pltpu.VMEM((tm, d), BF16), pltpu.SemaphoreType.DMA((2,))]),
        compiler_params=_params(("arbitrary", "arbitrary")),
        name="ffn1",
    )(tile_expert, n_used, *_weight_runs(tile_expert, nj), tile_valid, xs, w1, b1_3, b1_3)


def _ffn2_kernel(te_ref, nu_ref, first_ref, ne_ref, nj_ref, tv_ref, act_ref, w_hbm, b_ref, ys_ref,
                 w_stage, w_scr, sem):
    del nu_ref
    i = pl.program_id(0)
    tm = act_ref.shape[0]

    def copies(e, jj):
        del jj
        return (pltpu.make_async_copy(w_hbm.at[e], w_stage, sem),)

    def cast():
        w_scr[...] = w_stage[...].astype(BF16)

    _stream_weights(i, first_ref, ne_ref, nj_ref, (te_ref[i], 0), copies, cast)

    rows = tm // ROW_GROUPS
    seg = ys_ref.shape[0] // tm
    for r0 in range(0, tm, rows):
        ys_rows = ys_ref.at[pl.ds(r0 * seg, rows * seg)]

        @pl.when(tv_ref[i] > r0)
        def _():
            y = jnp.dot(act_ref[r0:r0 + rows, :], w_scr[...], preferred_element_type=F32) + b_ref[0]
            _pack_rows(y, ys_rows, rows)

        @pl.when(tv_ref[i] <= r0)
        def _():
            ys_rows[...] = jnp.zeros(ys_rows.shape, ys_rows.dtype)


def _ffn2(tile_expert, n_used, tile_valid, act, w2, b2_3, tm):
    slots, d_ff = act.shape
    d = w2.shape[2]
    seg = d // 2 // LANES
    n_tiles = slots // tm
    row = lambda i, nu: jnp.minimum(i, nu[0] - 1)
    return pl.pallas_call(
        _ffn2_kernel,
        out_shape=jax.ShapeDtypeStruct((slots * seg, LANES), U32),
        grid_spec=pltpu.PrefetchScalarGridSpec(
            num_scalar_prefetch=6,
            grid=(n_tiles,),
            in_specs=[pl.BlockSpec((tm, d_ff), lambda i, te, nu, *_: (row(i, nu), 0)),
                      pl.BlockSpec(memory_space=pl.ANY),
                      pl.BlockSpec((1, 1, d), lambda i, te, *_: (te[i], 0, 0))],
            out_specs=pl.BlockSpec((tm * seg, LANES), lambda i, *_: (i, 0)),
            scratch_shapes=[pltpu.VMEM((d_ff, d), F32), pltpu.VMEM((d_ff, d), BF16),
                            pltpu.SemaphoreType.DMA(())]),
        compiler_params=_params(("arbitrary",)),
        name="ffn2",
    )(tile_expert, n_used, *_weight_runs(tile_expert, 1), tile_valid, act, w2, b2_3)


def _combine_kernel(pos_ref, ys_ref, gate_ref, x1_ref, g2_ref, pfn_ref, o_ref, buf, moe_scr, sem, *, n_tok):
    tc, d = x1_ref.shape
    seg = buf.shape[2] // tc
    half = seg * LANES
    step = pl.program_id(0)

    def gather(st, slot):
        def body(t, carry):
            for k in range(TOP_K):
                p = pos_ref[k * n_tok + st * tc + t]
                pltpu.make_async_copy(ys_ref.at[pl.ds(pl.multiple_of(p * seg, seg), seg)],
                                      buf.at[slot, k, pl.ds(pl.multiple_of(t * seg, seg), seg)],
                                      sem.at[slot, k]).start(priority=k % 2)
            return carry
        lax.fori_loop(0, tc, body, 0)

    @pl.when(step == 0)
    def _():
        gather(0, 0)

    @pl.when(step + 1 < pl.num_programs(0))
    def _():
        gather(step + 1, (step + 1) % 2)

    slot = step % 2
    for k in range(TOP_K):
        pltpu.make_async_copy(buf.at[slot, k], buf.at[slot, k], sem.at[slot, k]).wait()
    gates = [jnp.broadcast_to(gate_ref[:, k:k + 1], (tc, LANES)) for k in range(TOP_K)]
    for s in range(seg):
        m_lo = m_hi = None
        for k in range(TOP_K):
            lo, hi = _unpack_seg(buf[slot, k, pl.ds(s, tc, stride=seg), :])
            m_lo = lo * gates[k] if m_lo is None else m_lo + lo * gates[k]
            m_hi = hi * gates[k] if m_hi is None else m_hi + hi * gates[k]
        moe_scr[:, s * LANES:(s + 1) * LANES] = m_lo
        moe_scr[:, half + s * LANES:half + (s + 1) * LANES] = m_hi
    o_ref[...] = x1_ref[...] + g2_ref[0] * _rms(moe_scr[...], pfn_ref[...])


def _combine(pos_flat, ys, gates_t, x1, mod3, post_ffn_norm, seq):
    n, d = x1.shape
    seg = d // 2 // LANES
    tc = _tile(seq, 256)
    per_batch = seq // tc
    return pl.pallas_call(
        functools.partial(_combine_kernel, n_tok=n),
        out_shape=jax.ShapeDtypeStruct((n, d), F32),
        grid_spec=pltpu.PrefetchScalarGridSpec(
            num_scalar_prefetch=1,
            grid=(n // tc,),
            in_specs=[pl.BlockSpec(memory_space=pl.ANY),
                      pl.BlockSpec((tc, TOP_K), lambda i, pos: (i, 0)),
                      pl.BlockSpec((tc, d), lambda i, pos: (i, 0)),
                      pl.BlockSpec((1, 1, d), lambda i, pos: (i // per_batch, 0, 5)),
                      pl.BlockSpec((1, d), lambda i, pos: (0, 0))],
            out_specs=pl.BlockSpec((tc, d), lambda i, pos: (i, 0)),
            scratch_shapes=[pltpu.VMEM((2, TOP_K, tc * seg, LANES), U32), pltpu.VMEM((tc, d), F32),
                            pltpu.SemaphoreType.DMA((2, TOP_K))]),
        compiler_params=_params(("arbitrary",)),
        name="combine",
    )(pos_flat, ys, gates_t, x1, mod3, post_ffn_norm)


def _moe(h2p, logits_t, x1, mod3, post_ffn_norm, w1, b1, w2, b2, seq, n_exp):
    n, d = x1.shape
    seg = d // 2 // LANES
    tm = 1 << (min(n * TOP_K, 512).bit_length() - 1)
    idx, gates, rank, cnt = _route(logits_t, n_exp)

    counts = cnt[:n_exp, 0].astype(jnp.int32)
    padded = (counts + tm - 1) // tm * tm
    ends = jnp.cumsum(padded)
    starts = ends - padded
    hot = idx[:, :, None] == jnp.arange(n_exp, dtype=jnp.int32)
    pos_flat = (jnp.sum(jnp.where(hot, starts, 0), axis=-1) + rank).reshape(-1)
    n_tiles = -(-n * TOP_K // tm) + n_exp
    n_used = (ends[-1] // tm).astype(jnp.int32)
    tile_start = jnp.minimum(jnp.arange(n_tiles, dtype=jnp.int32), n_used - 1) * tm
    tile_expert = jnp.sum(tile_start[:, None] >= ends[None, :], axis=1).astype(jnp.int32)
    tile_hot = tile_expert[:, None] == jnp.arange(n_exp, dtype=jnp.int32)
    tile_rows_left = jnp.sum(jnp.where(tile_hot, starts + counts, 0), axis=1) - tile_start
    tile_valid = jnp.where(jnp.arange(n_tiles) < n_used, jnp.clip(tile_rows_left, 0, tm), 0).astype(jnp.int32)
    n_used = n_used.reshape(1)

    xs = _dispatch(pos_flat, starts + counts, padded - counts, h2p, n_tiles * tm, seg, tm)
    act = _ffn1(tile_expert, n_used, tile_valid, xs, w1, b1[:, None, :], tm, seg)
    ys = _ffn2(tile_expert, n_used, tile_valid, act, w2, b2[:, None, :], tm)
    return _combine(pos_flat, ys, gates.T, x1, mod3, post_ffn_norm, seq)


def kernel(x, c, ctx, c_ctx, ada_w, ada_b, pre_mix_norm, post_mix_norm, pre_ffn_norm, post_ffn_norm,
           w_in, ret_decay_fwd, ret_decay_bwd, ret_gn_w, conv_w, conv_b, conv_ln_w, conv_ln_b, w_out,
           router_w, router_b, w1, b1, w2, b2):
    assert ada_w.shape[0] == 1, "single-layer stack only"
    b, t, d = x.shape
    n_ctx = ctx.shape[1]
    n_heads = ret_decay_fwd.shape[1]
    d_ret = n_heads * HEAD_DIM
    n_exp = router_w.shape[2]
    assert b < MOD_ROWS and ret_gn_w.shape[1] == d_ret and d % (2 * LANES) == 0

    cc = jnp.zeros((MOD_ROWS, d), F32).at[:b].set(c).at[b].set(c_ctx)
    mod3 = _ada(cc, ada_w[0], ada_b)[:, None, :]
    cos, sin = _rope_tables(n_ctx + t)

    w_in_bf = w_in[0].astype(BF16)
    proj = _in_proj(x, pre_mix_norm, mod3, lambda bi: bi, w_in_bf, 0, w_in.shape[2])
    ctx_kv = _in_proj(ctx.reshape(1, b * n_ctx, d), pre_mix_norm, mod3, lambda bi: b, w_in_bf, d_ret, 2 * d_ret)

    lane_bcast = lambda v: jnp.broadcast_to(v.reshape(n_heads, 1, 1), (n_heads, 1, LANES))
    y_ret = _retention(proj, ctx_kv, cos, sin, lane_bcast(ret_decay_fwd[0]), lane_bcast(ret_decay_bwd[0]),
                       ret_gn_w, n_heads, n_ctx)
    y_conv = _conv(proj, 4 * d_ret, conv_w[0], conv_b, conv_ln_w, conv_ln_b)

    e_rows = -(-n_exp // SUBLANES) * SUBLANES
    rw_t = jnp.zeros((e_rows, d), F32).at[:n_exp].set(router_w[0].T)
    rw_hi = rw_t.astype(BF16)
    rw_lo = (rw_t - rw_hi.astype(F32)).astype(BF16)
    rb_col = jnp.zeros((e_rows, 1), F32).at[:n_exp, 0].set(router_b[0])
    x1, h2p, logits_t = _out_proj(y_ret, y_conv, w_out[0].astype(BF16), x, mod3, post_mix_norm, pre_ffn_norm,
                                  jnp.concatenate([rw_hi, rw_lo], axis=0), rb_col)

    out = _moe(h2p, logits_t, x1.reshape(b * t, d), mod3, post_ffn_norm, w1[0], b1[0], w2[0], b2[0], t, n_exp)
    return out.reshape(b, t, d)
```

```python
import functools
import math

import jax
import jax.numpy as jnp
from jax import lax
from jax.experimental import pallas as pl
from jax.experimental.pallas import tpu as pltpu

F32 = jnp.float32
BF16 = jnp.bfloat16
U32 = jnp.uint32

GRID_W = 64
HEAD_DIM = 128
RET_CHUNK = 256
ROPE_BASE = 10000.0
TOP_K = 4
SWIGLU_ALPHA = 1.702
SWIGLU_LIMIT = 7.0
EPS = 1e-6
GN_EPS = 1e-5
LANES = 128
SUBLANES = 8
MOD_ROWS = 16
VMEM_LIMIT = 56 * 1024 * 1024
HI16 = 0xFFFF0000
ROW_GROUPS = 1


def _tile(n, pref):
    t = min(n, pref)
    while n % t:
        t -= 1
    return t


def _params(sem, vmem=VMEM_LIMIT):
    return pltpu.CompilerParams(dimension_semantics=sem, vmem_limit_bytes=vmem)


def _rms(x, w):
    return x * lax.rsqrt(jnp.mean(x * x, axis=-1, keepdims=True) + EPS) * w


def _silu(x):
    return x * jax.nn.sigmoid(x)


def _dot_nt(a, b):
    return lax.dot_general(a, b, (((1,), (1,)), ((), ())), preferred_element_type=F32)


def _dot_tn(a, b):
    return lax.dot_general(a, b, (((0,), (0,)), ((), ())), preferred_element_type=F32)


def _ada_kernel(c_ref, w_ref, b_ref, o_ref):
    s = _silu(c_ref[...])
    w = w_ref[...]
    s_hi, w_hi = s.astype(BF16), w.astype(BF16)
    s_lo = (s - s_hi.astype(F32)).astype(BF16)
    w_lo = (w - w_hi.astype(F32)).astype(BF16)
    o_ref[...] = (jnp.dot(s_hi, w_hi, preferred_element_type=F32) + jnp.dot(s_lo, w_hi, preferred_element_type=F32)
                  + jnp.dot(s_hi, w_lo, preferred_element_type=F32) + b_ref[...])


def _ada(cc, w, b):
    d, n = w.shape
    tn = _tile(n, 1024)
    return pl.pallas_call(
        _ada_kernel,
        out_shape=jax.ShapeDtypeStruct((MOD_ROWS, n), F32),
        grid=(n // tn,),
        in_specs=[pl.BlockSpec((MOD_ROWS, d), lambda j: (0, 0)),
                  pl.BlockSpec((d, tn), lambda j: (0, j)),
                  pl.BlockSpec((1, tn), lambda j: (0, j))],
        out_specs=pl.BlockSpec((MOD_ROWS, tn), lambda j: (0, j)),
        compiler_params=_params(("parallel",)),
        name="ada",
    )(cc, w, b)


def _rope_kernel(cos_ref, sin_ref):
    p, _ = cos_ref.shape
    half = HEAD_DIM // 2
    lane = lax.broadcasted_iota(jnp.int32, (p, HEAD_DIM), 1)
    pos = lax.broadcasted_iota(jnp.int32, (p, HEAD_DIM), 0).astype(F32)
    j = jnp.where(lane < half, lane, lane - half).astype(F32)
    inv = jnp.exp(j * (-jnp.log(ROPE_BASE) / half))
    ang = pos * inv
    cos_ref[...] = jnp.cos(ang)
    sin_ref[...] = jnp.where(lane < half, -1.0, 1.0) * jnp.sin(ang)


def _rope_tables(p):
    return pl.pallas_call(
        _rope_kernel,
        out_shape=(jax.ShapeDtypeStruct((p, HEAD_DIM), F32), jax.ShapeDtypeStruct((p, HEAD_DIM), F32)),
        name="rope",
    )()


def _rot(t, cos, sin_signed):
    return t * cos + pltpu.roll(t, HEAD_DIM // 2, axis=1) * sin_signed


def _inproj_kernel(x_ref, nw_ref, sh_ref, sc_ref, w_ref, o_ref, h_scr):
    @pl.when(pl.program_id(2) == 0)
    def _():
        rows = math.gcd(x_ref.shape[1], LANES)
        gain = nw_ref[...] * (1.0 + sc_ref[0])
        shift = sh_ref[0]

        def body(r, carry):
            sl = pl.ds(pl.multiple_of(r * rows, rows), rows)
            x = x_ref[0, sl, :]
            rs = lax.rsqrt(jnp.mean(x * x, axis=-1, keepdims=True) + EPS)
            h_scr[sl, :] = (x * rs * gain + shift).astype(BF16)
            return carry
        lax.fori_loop(0, x_ref.shape[1] // rows, body, 0)

    res = jnp.dot(h_scr[...], w_ref[...], preferred_element_type=F32)
    for cb in range(o_ref.shape[1]):
        o_ref[0, cb] = res[:, cb * LANES:(cb + 1) * LANES]


def _in_proj(x, norm_w, mod3, mod_row, w_bf, col0, ncols):
    b, t, d = x.shape
    tm = _tile(t, 1024)
    tn = _tile(math.gcd(ncols, col0), 1536)
    joff = col0 // tn
    nblk = tn // LANES
    return pl.pallas_call(
        _inproj_kernel,
        out_shape=jax.ShapeDtypeStruct((b, ncols // LANES, t, LANES), F32),
        grid=(b, t // tm, ncols // tn),
        in_specs=[pl.BlockSpec((1, tm, d), lambda bi, i, j: (bi, i, 0)),
                  pl.BlockSpec((1, d), lambda bi, i, j: (0, 0)),
                  pl.BlockSpec((1, 1, d), lambda bi, i, j: (mod_row(bi), 0, 0)),
                  pl.BlockSpec((1, 1, d), lambda bi, i, j: (mod_row(bi), 0, 1)),
                  pl.BlockSpec((d, tn), lambda bi, i, j: (0, j + joff))],
        out_specs=pl.BlockSpec((1, nblk, tm, LANES), lambda bi, i, j: (bi, j, i, 0)),
        scratch_shapes=[pltpu.VMEM((tm, d), BF16)],
        compiler_params=_params(("parallel", "parallel", "arbitrary")),
        name="in_proj",
    )(x, norm_w, mod3, mod3, w_bf)


def _log_sigmoid(x):
    return jnp.minimum(x, 0.0) - jnp.log(1.0 + jnp.exp(-jnp.abs(x)))


def _ret_kernel(q_ref, k_ref, v_ref, g_ref, kc_ref, vc_ref, cos_ref, sin_ref, decf_ref, decb_ref,
                gnw_ref, o_ref, q_scr, k_scr, kf_scr, kb_scr, v_scr, o_scr, din_scr, kvf_scr, kvb_scr,
                rf_scr, rb_scr, *, n_ctx):
    t = q_ref.shape[2]
    c = din_scr.shape[0]
    nc = t // c
    lgf = _log_sigmoid(decf_ref[0])[:, 0:1]
    lgb = _log_sigmoid(decb_ref[0])[:, 0:1]

    kc = _rot(kc_ref[0, 0], cos_ref[0:n_ctx, :], sin_ref[0:n_ctx, :])
    vc = vc_ref[0, 0].astype(BF16)
    tc = lax.broadcasted_iota(jnp.int32, (n_ctx, 1), 0).astype(F32)
    rf = _dot_tn((kc * jnp.exp(lgf * (n_ctx - 1.0 - tc))).astype(BF16), vc)
    rb = _dot_tn((kc * jnp.exp(lgb * tc)).astype(BF16), vc)

    scale = HEAD_DIM ** -0.5
    cos, sin = cos_ref[n_ctx:n_ctx + t, :], sin_ref[n_ctx:n_ctx + t, :]
    q_scr[...] = (_rot(q_ref[0, 0], cos, sin) * scale).astype(BF16)
    kr = _rot(k_ref[0, 0], cos, sin)
    ri_all = (lax.broadcasted_iota(jnp.int32, (t, HEAD_DIM), 0) % c).astype(F32)
    k_scr[...] = kr.astype(BF16)
    kf_scr[...] = (kr * jnp.exp(lgf * (c - 1.0 - ri_all))).astype(BF16)
    kb_scr[...] = (kr * jnp.exp(lgb * ri_all)).astype(BF16)
    v_scr[...] = v_ref[0, 0].astype(BF16)

    diff = (lax.broadcasted_iota(jnp.int32, (c, c), 0) - lax.broadcasted_iota(jnp.int32, (c, c), 1)).astype(F32)
    din_scr[...] = jnp.where(diff > 0, jnp.exp(lgf * jnp.maximum(diff, 0.0)),
                             jnp.where(diff < 0, jnp.exp(lgb * jnp.maximum(-diff, 0.0)), 2.0))

    for i in range(nc):
        sl = slice(i * c, (i + 1) * c)
        vi = v_scr[sl, :]
        scores = _dot_nt(q_scr[sl, :], k_scr[sl, :]) * din_scr[...]
        o_scr[sl, :] = jnp.dot(scores.astype(BF16), vi, preferred_element_type=F32)
        kvf_scr[i] = _dot_tn(kf_scr[sl, :], vi)
        kvb_scr[i] = _dot_tn(kb_scr[sl, :], vi)

    gc_f = jnp.exp(lgf * float(c))
    gc_b = jnp.exp(lgb * float(c))
    for i in range(nc):
        rf_scr[i] = rf.astype(BF16)
        rf = rf * gc_f + kvf_scr[i]
    for i in reversed(range(nc)):
        rb_scr[i] = rb.astype(BF16)
        rb = rb * gc_b + kvb_scr[i]

    ri = lax.broadcasted_iota(jnp.int32, (c, HEAD_DIM), 0).astype(F32)
    dq_f = jnp.exp(lgf * (ri + 1.0))
    dq_b = jnp.exp(lgb * (c - ri))
    for i in range(nc):
        sl = slice(i * c, (i + 1) * c)
        qi = q_scr[sl, :]
        o = (o_scr[sl, :] + jnp.dot(qi, rf_scr[i], preferred_element_type=F32) * dq_f
             + jnp.dot(qi, rb_scr[i], preferred_element_type=F32) * dq_b)
        mu = jnp.mean(o, axis=-1, keepdims=True)
        var = jnp.mean(jnp.square(o - mu), axis=-1, keepdims=True)
        on = (o - mu) * lax.rsqrt(var + GN_EPS) * gnw_ref[...]
        o_ref[0, sl, :] = (on * _silu(g_ref[0, 0, sl, :])).astype(o_ref.dtype)


def _retention(proj, ctx_kv, cos, sin, decf3, decb3, gn_w, n_heads, n_ctx):
    b, _, t, hd = proj.shape
    c = _tile(t, RET_CHUNK)
    nc = t // c
    lat = lambda g: pl.BlockSpec((1, 1, t, hd), lambda bi, h: (bi, g * n_heads + h, 0, 0))
    seq = lambda dt: pltpu.VMEM((t, hd), dt)
    return pl.pallas_call(
        functools.partial(_ret_kernel, n_ctx=n_ctx),
        out_shape=jax.ShapeDtypeStruct((b, t, n_heads * hd), BF16),
        grid=(b, n_heads),
        in_specs=[lat(0), lat(1), lat(2), lat(3),
                  pl.BlockSpec((1, 1, n_ctx, hd), lambda bi, h: (0, h, bi, 0)),
                  pl.BlockSpec((1, 1, n_ctx, hd), lambda bi, h: (0, n_heads + h, bi, 0)),
                  pl.BlockSpec(cos.shape, lambda bi, h: (0, 0)),
                  pl.BlockSpec(sin.shape, lambda bi, h: (0, 0)),
                  pl.BlockSpec((1, 1, LANES), lambda bi, h: (h, 0, 0)),
                  pl.BlockSpec((1, 1, LANES), lambda bi, h: (h, 0, 0)),
                  pl.BlockSpec((1, hd), lambda bi, h: (0, h))],
        out_specs=pl.BlockSpec((1, t, hd), lambda bi, h: (bi, 0, h)),
        scratch_shapes=[seq(BF16), seq(BF16), seq(BF16), seq(BF16), seq(BF16), seq(F32),
                        pltpu.VMEM((c, c), F32),
                        pltpu.VMEM((nc, hd, hd), F32), pltpu.VMEM((nc, hd, hd), F32),
                        pltpu.VMEM((nc, hd, hd), BF16), pltpu.VMEM((nc, hd, hd), BF16)],
        compiler_params=_params(("parallel", "parallel")),
        name="ret",
    )(proj, proj, proj, proj, ctx_kv, ctx_kv, cos, sin, decf3, decb3, gn_w)


CONV_PAD_ROWS = 16
CONV_LANE_CHUNK = 256


def _conv_kernel(a_ref, b_ref, w_ref, cb_ref, lnw_ref, lnb_ref, o_ref, up_scr, sh_scr, y_scr):
    nblk, tt = a_ref.shape[1], a_ref.shape[2]
    ch = nblk * LANES
    kw = w_ref.shape[0]
    n_seq = tt // GRID_W
    lead = CONV_PAD_ROWS - kw // 2
    rows = GRID_W + 2 * CONV_PAD_ROWS
    zeros = jnp.zeros((CONV_PAD_ROWS, ch), F32)
    for s in range(n_seq):
        up_scr[s, 0:CONV_PAD_ROWS, :] = zeros
        up_scr[s, CONV_PAD_ROWS + GRID_W:, :] = zeros
    for cb in range(nblk):
        u = a_ref[0, cb] * jax.nn.sigmoid(b_ref[0, cb])
        for s in range(n_seq):
            up_scr[s, CONV_PAD_ROWS:CONV_PAD_ROWS + GRID_W, cb * LANES:(cb + 1) * LANES] = (
                u[s * GRID_W:(s + 1) * GRID_W, :])

    cw = sh_scr.shape[2]

    def seq(s, carry):
        row0 = pl.multiple_of(s * GRID_W, GRID_W)
        for c0 in range(0, ch, cw):
            for r in range(SUBLANES):
                sh_scr[r] = up_scr[s, r:r + rows - SUBLANES, c0:c0 + cw]
            acc = jnp.broadcast_to(cb_ref[:, c0:c0 + cw], (GRID_W, cw))
            for k in range(kw):
                a8, r = divmod(lead + k, SUBLANES)
                acc = acc + sh_scr[r, a8 * SUBLANES:a8 * SUBLANES + GRID_W, :] * w_ref[k:k + 1, c0:c0 + cw]
            y_scr[pl.ds(row0, GRID_W), c0:c0 + cw] = acc
        return carry

    lax.fori_loop(0, n_seq, seq, 0)
    y = y_scr[...]
    mu = jnp.mean(y, axis=-1, keepdims=True)
    var = jnp.mean(jnp.square(y - mu), axis=-1, keepdims=True)
    yn = (y - mu) * lax.rsqrt(var + EPS) * lnw_ref[...] + lnb_ref[...]
    o_ref[0] = _silu(yn).astype(o_ref.dtype)


def _conv(proj, col0, conv_w, conv_b, ln_w, ln_b):
    b, _, t, _ = proj.shape
    kw, ch = conv_w.shape
    assert col0 % ch == 0 and ch % LANES == 0 and kw // 2 <= CONV_PAD_ROWS and t % GRID_W == 0
    nblk = ch // LANES
    tt = _tile(t, 8 * GRID_W)
    cw = min(ch, CONV_LANE_CHUNK)
    rows = GRID_W + 2 * CONV_PAD_ROWS
    ca, cb = col0 // ch, col0 // ch + 1
    vec = pl.BlockSpec((1, ch), lambda bi, i: (0, 0))
    return pl.pallas_call(
        _conv_kernel,
        out_shape=jax.ShapeDtypeStruct((b, t, ch), BF16),
        grid=(b, t // tt),
        in_specs=[pl.BlockSpec((1, nblk, tt, LANES), lambda bi, i: (bi, ca, i, 0)),
                  pl.BlockSpec((1, nblk, tt, LANES), lambda bi, i: (bi, cb, i, 0)),
                  pl.BlockSpec((kw, ch), lambda bi, i: (0, 0)),
                  vec, vec, vec],
        out_specs=pl.BlockSpec((1, tt, ch), lambda bi, i: (bi, i, 0)),
        scratch_shapes=[pltpu.VMEM((tt // GRID_W, rows, ch), F32),
                        pltpu.VMEM((SUBLANES, rows - SUBLANES, cw), F32),
                        pltpu.VMEM((tt, ch), F32)],
        compiler_params=_params(("parallel", "parallel")),
        name="conv",
    )(proj, proj, conv_w, conv_b, ln_w, ln_b)


def _outproj_kernel(yr_ref, yc_ref, wr_ref, wc_ref, x_ref, g1_ref, sh2_ref, sc2_ref, pmn_ref, pfn_ref,
                    rw_ref, rb_ref, x1_ref, h2p_ref, lg_ref):
    tm = x_ref.shape[1]
    e_rows = lg_ref.shape[0]
    mix = (jnp.dot(yr_ref[0], wr_ref[...], preferred_element_type=F32)
           + jnp.dot(yc_ref[0], wc_ref[...], preferred_element_type=F32))
    x1 = x_ref[0] + _rms(mix, g1_ref[0] * pmn_ref[...])
    x1_ref[0] = x1
    h2 = _rms(x1, pfn_ref[...] * (1.0 + sc2_ref[0])) + sh2_ref[0]
    h2_hi = h2.astype(BF16)
    h2_lo = (h2 - h2_hi.astype(F32)).astype(BF16)
    by_hi = _dot_nt(rw_ref[...], h2_hi)
    lg_ref[...] = by_hi[:e_rows] + by_hi[e_rows:] + _dot_nt(rw_ref[:e_rows, :], h2_lo) + rb_ref[...]
    _pack_rows(h2, h2p_ref, tm)


def _out_proj(y_ret, y_conv, w_out_bf, x, mod3, post_mix_norm, pre_ffn_norm, rw_hi_lo, rb_col):
    b, t, d = x.shape
    d_ret, d_conv = y_ret.shape[2], y_conv.shape[2]
    e_rows = rb_col.shape[0]
    seg = d // 2 // LANES
    tm = _tile(t, 512)
    per_b = t // tm
    mod = lambda col: pl.BlockSpec((1, 1, d), lambda bi, i: (bi, 0, col))
    vec = pl.BlockSpec((1, d), lambda bi, i: (0, 0))
    row = lambda width: pl.BlockSpec((1, tm, width), lambda bi, i: (bi, i, 0))
    rw = pl.BlockSpec((2 * e_rows, d), lambda bi, i: (0, 0))
    w_ret, w_conv = w_out_bf[:d_ret], w_out_bf[d_ret:]
    return pl.pallas_call(
        _outproj_kernel,
        out_shape=(jax.ShapeDtypeStruct((b, t, d), F32),
                   jax.ShapeDtypeStruct((b * t * seg, LANES), U32),
                   jax.ShapeDtypeStruct((e_rows, b * t), F32)),
        grid=(b, per_b),
        in_specs=[row(d_ret), row(d_conv),
                  pl.BlockSpec((d_ret, d), lambda bi, i: (0, 0)),
                  pl.BlockSpec((d_conv, d), lambda bi, i: (0, 0)),
                  row(d), mod(2), mod(3), mod(4), vec, vec, rw,
                  pl.BlockSpec((e_rows, 1), lambda bi, i: (0, 0))],
        out_specs=(row(d),
                   pl.BlockSpec((tm * seg, LANES), lambda bi, i: (bi * per_b + i, 0)),
                   pl.BlockSpec((e_rows, tm), lambda bi, i: (0, bi * per_b + i))),
        compiler_params=_params(("parallel", "parallel")),
        name="out_proj",
    )(y_ret, y_conv, w_ret, w_conv, x, mod3, mod3, mod3, post_mix_norm, pre_ffn_norm, rw_hi_lo, rb_col)


def _route_kernel(lg_ref, idx_ref, gate_ref, rank_ref, cnt_ref, tri_scr, run_scr, *, n_experts):
    e_rows, tr = lg_ref.shape

    @pl.when(pl.program_id(0) == 0)
    def _():
        r = lax.broadcasted_iota(jnp.int32, (tr, tr), 0)
        c = lax.broadcasted_iota(jnp.int32, (tr, tr), 1)
        tri_scr[...] = jnp.where(r <= c, 1.0, 0.0).astype(BF16)
        run_scr[...] = jnp.zeros_like(run_scr)

    e_iota = lax.broadcasted_iota(jnp.int32, (e_rows, tr), 0)
    neg = jnp.float32(-jnp.inf)
    logits = jnp.where(e_iota < n_experts, lg_ref[...], neg)
    vals, hots = [], []
    for k in range(TOP_K):
        m = jnp.max(logits, axis=0, keepdims=True)
        ik = jnp.min(jnp.where(logits == m, e_iota, e_rows), axis=0, keepdims=True)
        hot = e_iota == ik
        logits = jnp.where(hot, neg, logits)
        vals.append(m)
        hots.append(hot)
        idx_ref[k:k + 1, :] = ik
    exps = [jnp.exp(v - vals[0]) for v in vals]
    den = exps[0]
    for e in exps[1:]:
        den = den + e
    for k in range(TOP_K):
        gate_ref[k:k + 1, :] = exps[k] / den

    sel = jnp.zeros((e_rows, tr), F32)
    for hot in hots:
        sel = sel + jnp.where(hot, 1.0, 0.0)
    csum = jnp.dot(sel.astype(BF16), tri_scr[...], preferred_element_type=F32)
    before = run_scr[:, 0:1] + csum - sel
    for k in range(TOP_K):
        rank_ref[k:k + 1, :] = jnp.sum(jnp.where(hots[k], before, 0.0), axis=0, keepdims=True).astype(jnp.int32)
    run_scr[...] = run_scr[...] + jnp.sum(sel, axis=1, keepdims=True)
    cnt_ref[...] = run_scr[...]


def _route(logits_t, n_experts):
    e_rows, n = logits_t.shape
    tr = _tile(n, 512)
    kt = lambda dt: jax.ShapeDtypeStruct((TOP_K, n), dt)
    blk = pl.BlockSpec((TOP_K, tr), lambda i: (0, i))
    return pl.pallas_call(
        functools.partial(_route_kernel, n_experts=n_experts),
        out_shape=(kt(jnp.int32), kt(F32), kt(jnp.int32), jax.ShapeDtypeStruct((e_rows, LANES), F32)),
        grid=(n // tr,),
        in_specs=[pl.BlockSpec((e_rows, tr), lambda i: (0, i))],
        out_specs=(blk, blk, blk, pl.BlockSpec((e_rows, LANES), lambda i: (0, 0))),
        scratch_shapes=[pltpu.VMEM((tr, tr), BF16), pltpu.VMEM((e_rows, LANES), F32)],
        compiler_params=_params(("arbitrary",)),
        name="route",
    )(logits_t)


def _dispatch_kernel(pos_ref, poff_ref, plen_ref, h_ref, xs_ref, zero_scr, sem, zsem, *, n_tok, n_exp, seg, tm):
    td = h_ref.shape[0] // seg
    step = pl.program_id(0)
    base = step * td

    def pad_copies(e, act):
        off, ln = poff_ref[e], plen_ref[e]
        for bit in range(tm.bit_length() - 1):
            size = 1 << bit

            @pl.when(((ln >> bit) & 1) == 1)
            def _():
                row = pl.multiple_of((off + (ln & (size - 1))) * seg, seg)
                act(pltpu.make_async_copy(zero_scr.at[pl.ds(0, size * seg)],
                                          xs_ref.at[pl.ds(row, size * seg)], zsem))

    def for_pads(act):
        def body(e, carry):
            pad_copies(e, act)
            return carry
        lax.fori_loop(0, n_exp, body, 0)

        half = zero_scr.shape[0]
        end = (poff_ref[n_exp - 1] + plen_ref[n_exp - 1]) * seg

        def tail(n, carry):
            row = pl.multiple_of(end + n * half, seg)
            act(pltpu.make_async_copy(zero_scr, xs_ref.at[pl.ds(row, half)], zsem))
            return carry
        lax.fori_loop(0, (xs_ref.shape[0] - end) // half, tail, 0)

    @pl.when(step == 0)
    def _():
        zero_scr[...] = jnp.zeros_like(zero_scr)
        for_pads(lambda cp: cp.start())

    def row_copy(t, k):
        p = pos_ref[k * n_tok + base + t]
        return pltpu.make_async_copy(h_ref.at[pl.ds(pl.multiple_of(t * seg, seg), seg)],
                                     xs_ref.at[pl.ds(pl.multiple_of(p * seg, seg), seg)], sem)

    def start(t, carry):
        for k in range(TOP_K):
            row_copy(t, k).start(priority=k % 2)
        return carry

    lax.fori_loop(0, td, start, 0, unroll=8)
    for k in range(TOP_K):
        pltpu.make_async_copy(h_ref, h_ref, sem).wait()

    @pl.when(step == 0)
    def _():
        for_pads(lambda cp: cp.wait())


def _dispatch(pos_flat, pad_off, pad_len, h2p, n_slots, seg, tm):
    n = h2p.shape[0] // seg
    n_exp = pad_off.shape[0]
    assert tm & (tm - 1) == 0
    td = _tile(n, 512)
    return pl.pallas_call(
        functools.partial(_dispatch_kernel, n_tok=n, n_exp=n_exp, seg=seg, tm=tm),
        out_shape=jax.ShapeDtypeStruct((n_slots * seg, LANES), U32),
        grid_spec=pltpu.PrefetchScalarGridSpec(
            num_scalar_prefetch=3,
            grid=(n // td,),
            in_specs=[pl.BlockSpec((td * seg, LANES), lambda i, *_: (i, 0))],
            out_specs=pl.BlockSpec(memory_space=pl.ANY),
            scratch_shapes=[pltpu.VMEM((max(tm // 2, 1) * seg, LANES), U32),
                            pltpu.SemaphoreType.DMA(()), pltpu.SemaphoreType.DMA(())]),
        compiler_params=_params(("arbitrary",)),
        name="dispatch",
    )(pos_flat, pad_off, pad_len, h2p)


def _pack_rows(val, ref, tm):
    half = val.shape[1] // 2
    seg = half // LANES
    vb = val.astype(BF16).astype(F32)
    packed = ((lax.bitcast_convert_type(vb[:, :half], U32) >> 16)
              | (lax.bitcast_convert_type(vb[:, half:], U32) & jnp.uint32(HI16)))
    for s in range(seg):
        ref[pl.ds(s, tm, stride=seg), :] = packed[:, s * LANES:(s + 1) * LANES]


def _unpack_seg(words):
    return (lax.bitcast_convert_type(words << 16, F32),
            lax.bitcast_convert_type(words & jnp.uint32(HI16), F32))


def _weight_runs(tile_expert, n_passes):
    n_tiles = tile_expert.shape[0]
    total = n_tiles * n_passes
    e_lin = jnp.tile(tile_expert, n_passes)
    j_lin = jnp.repeat(jnp.arange(n_passes, dtype=jnp.int32), n_tiles)
    i_lin = jnp.tile(jnp.arange(n_tiles, dtype=jnp.int32), n_passes)
    start = (i_lin == 0) | (e_lin != jnp.roll(e_lin, 1))
    idx = jnp.arange(total, dtype=jnp.int32)
    nxt = jnp.min(jnp.where(start[None, :] & (idx[None, :] > idx[:, None]), idx[None, :], total), axis=1)
    has = nxt < total
    nxt = jnp.minimum(nxt, total - 1)
    return start.astype(jnp.int32), jnp.where(has, e_lin[nxt], -1).astype(jnp.int32), j_lin[nxt]


def _stream_weights(step, first_ref, ne_ref, nj_ref, cur, copies, cast):
    @pl.when(step == 0)
    def _():
        for cp in copies(*cur):
            cp.start()

    @pl.when(first_ref[step] == 1)
    def _():
        for cp in copies(*cur):
            cp.wait()
        cast()

        @pl.when(ne_ref[step] >= 0)
        def _():
            for cp in copies(ne_ref[step], nj_ref[step]):
                cp.start()


def _ffn1_kernel(te_ref, nu_ref, first_ref, ne_ref, nj_ref, tv_ref, xs_ref, w_hbm, bg_ref, bl_ref, act_ref,
                 wg_stage, wl_stage, wg_scr, wl_scr, x_scr, sem, *, seg, d_ff):
    del nu_ref
    j, i = pl.program_id(0), pl.program_id(1)
    tm = x_scr.shape[0]
    tn = wg_scr.shape[1]
    half = seg * LANES

    def copies(e, jj):
        col = pl.multiple_of(jj * tn, tn)
        return (pltpu.make_async_copy(w_hbm.at[e, :, pl.ds(col, tn)], wg_stage, sem.at[0]),
                pltpu.make_async_copy(w_hbm.at[e, :, pl.ds(d_ff + col, tn)], wl_stage, sem.at[1]))

    def cast():
        wg_scr[...] = wg_stage[...].astype(BF16)
        wl_scr[...] = wl_stage[...].astype(BF16)

    _stream_weights(j * pl.num_programs(1) + i, first_ref, ne_ref, nj_ref, (te_ref[i], j), copies, cast)

    rows = tm // ROW_GROUPS
    for r0 in range(0, tm, rows):
        @pl.when(tv_ref[i] > r0)
        def _():
            for s in range(seg):
                lo, hi = _unpack_seg(xs_ref[pl.ds(r0 * seg + s, rows, stride=seg), :])
                x_scr[r0:r0 + rows, s * LANES:(s + 1) * LANES] = lo.astype(BF16)
                x_scr[r0:r0 + rows, half + s * LANES:half + (s + 1) * LANES] = hi.astype(BF16)
            x = x_scr[r0:r0 + rows, :]
            glu = jnp.minimum(jnp.dot(x, wg_scr[...], preferred_element_type=F32) + bg_ref[0], SWIGLU_LIMIT)
            lin = jnp.clip(jnp.dot(x, wl_scr[...], preferred_element_type=F32) + bl_ref[0],
                           -SWIGLU_LIMIT, SWIGLU_LIMIT)
            act_ref[r0:r0 + rows, :] = (glu * jax.nn.sigmoid(SWIGLU_ALPHA * glu) * (lin + 1.0)).astype(act_ref.dtype)

        @pl.when(tv_ref[i] <= r0)
        def _():
            act_ref[r0:r0 + rows, :] = jnp.zeros((rows, tn), act_ref.dtype)


def _ffn1(tile_expert, n_used, tile_valid, xs, w1, b1_3, tm, seg):
    d = seg * LANES * 2
    slots = xs.shape[0] // seg
    d_ff = w1.shape[2] // 2
    tn = _tile(d_ff, 1024)
    nj = d_ff // tn
    n_tiles = slots // tm
    row = lambda i, nu: jnp.minimum(i, nu[0] - 1)
    return pl.pallas_call(
        functools.partial(_ffn1_kernel, seg=seg, d_ff=d_ff),
        out_shape=jax.ShapeDtypeStruct((slots, d_ff), BF16),
        grid_spec=pltpu.PrefetchScalarGridSpec(
            num_scalar_prefetch=6,
            grid=(nj, n_tiles),
            in_specs=[pl.BlockSpec((tm * seg, LANES), lambda j, i, te, nu, *_: (row(i, nu), 0)),
                      pl.BlockSpec(memory_space=pl.ANY),
                      pl.BlockSpec((1, 1, tn), lambda j, i, te, *_: (te[i], 0, j)),
                      pl.BlockSpec((1, 1, tn), lambda j, i, te, *_: (te[i], 0, nj + j))],
            out_specs=pl.BlockSpec((tm, tn), lambda j, i, *_: (i, j)),
            scratch_shapes=[pltpu.VMEM((d, tn), F32), pltpu.VMEM((d, tn), F32),
                            pltpu.VMEM((d, tn), BF16), pltpu.VMEM((d, tn), BF16),
                            pltpu.VMEM((tm, d), BF16), pltpu.SemaphoreType.DMA((2,))]),
        compiler_params=_params(("arbitrary", "arbitrary")),
        name="ffn1",
    )(tile_expert, n_used, *_weight_runs(tile_expert, nj), tile_valid, xs, w1, b1_3, b1_3)


def _ffn2_kernel(te_ref, nu_ref, first_ref, ne_ref, nj_ref, tv_ref, act_ref, w_hbm, b_ref, ys_ref,
                 w_stage, w_scr, sem):
    del nu_ref
    i = pl.program_id(0)
    tm = act_ref.shape[0]

    def copies(e, jj):
        del jj
        return (pltpu.make_async_copy(w_hbm.at[e], w_stage, sem),)

    def cast():
        w_scr[...] = w_stage[...].astype(BF16)

    _stream_weights(i, first_ref, ne_ref, nj_ref, (te_ref[i], 0), copies, cast)

    rows = tm // ROW_GROUPS
    seg = ys_ref.shape[0] // tm
    for r0 in range(0, tm, rows):
        ys_rows = ys_ref.at[pl.ds(r0 * seg, rows * seg)]

        @pl.when(tv_ref[i] > r0)
        def _():
            y = jnp.dot(act_ref[r0:r0 + rows, :], w_scr[...], preferred_element_type=F32) + b_ref[0]
            _pack_rows(y, ys_rows, rows)

        @pl.when(tv_ref[i] <= r0)
        def _():
            ys_rows[...] = jnp.zeros(ys_rows.shape, ys_rows.dtype)


def _ffn2(tile_expert, n_used, tile_valid, act, w2, b2_3, tm):
    slots, d_ff = act.shape
    d = w2.shape[2]
    seg = d // 2 // LANES
    n_tiles = slots // tm
    row = lambda i, nu: jnp.minimum(i, nu[0] - 1)
    return pl.pallas_call(
        _ffn2_kernel,
        out_shape=jax.ShapeDtypeStruct((slots * seg, LANES), U32),
        grid_spec=pltpu.PrefetchScalarGridSpec(
            num_scalar_prefetch=6,
            grid=(n_tiles,),
            in_specs=[pl.BlockSpec((tm, d_ff), lambda i, te, nu, *_: (row(i, nu), 0)),
                      pl.BlockSpec(memory_space=pl.ANY),
                      pl.BlockSpec((1, 1, d), lambda i, te, *_: (te[i], 0, 0))],
            out_specs=pl.BlockSpec((tm * seg, LANES), lambda i, *_: (i, 0)),
            scratch_shapes=[pltpu.VMEM((d_ff, d), F32), pltpu.VMEM((d_ff, d), BF16),
                            pltpu.SemaphoreType.DMA(())]),
        compiler_params=_params(("arbitrary",)),
        name="ffn2",
    )(tile_expert, n_used, *_weight_runs(tile_expert, 1), tile_valid, act, w2, b2_3)


def _combine_kernel(pos_ref, ys_ref, gate_ref, x1_ref, g2_ref, pfn_ref, o_ref, buf, moe_scr, sem, *, n_tok):
    tc, d = x1_ref.shape
    seg = buf.shape[2] // tc
    half = seg * LANES
    step = pl.program_id(0)

    last = pl.num_programs(0) - 1

    def start_row(st, slot, t):
        for k in range(TOP_K):
            p = pos_ref[k * n_tok + st * tc + t]
            pltpu.make_async_copy(ys_ref.at[pl.ds(pl.multiple_of(p * seg, seg), seg)],
                                  buf.at[slot, k, pl.ds(pl.multiple_of(t * seg, seg), seg)],
                                  sem.at[slot, k]).start(priority=k % 2)

    def wait_rows(slot):
        for k in range(TOP_K):
            pltpu.make_async_copy(buf.at[slot, k], buf.at[slot, k], sem.at[slot, k]).wait()

    @pl.when(step == 0)
    def _():
        def body(t, carry):
            start_row(0, 0, t)
            return carry
        lax.fori_loop(0, tc, body, 0)

    slot = step % 2
    wait_rows(slot)
    for t in range(tc):
        start_row(jnp.minimum(step + 1, last), 1 - slot, t)
    gates = [jnp.broadcast_to(gate_ref[:, k:k + 1], (tc, LANES)) for k in range(TOP_K)]
    for s in range(seg):
        m_lo = m_hi = None
        for k in range(TOP_K):
            lo, hi = _unpack_seg(buf[slot, k, pl.ds(s, tc, stride=seg), :])
            m_lo = lo * gates[k] if m_lo is None else m_lo + lo * gates[k]
            m_hi = hi * gates[k] if m_hi is None else m_hi + hi * gates[k]
        moe_scr[:, s * LANES:(s + 1) * LANES] = m_lo
        moe_scr[:, half + s * LANES:half + (s + 1) * LANES] = m_hi
    o_ref[...] = x1_ref[...] + g2_ref[0] * _rms(moe_scr[...], pfn_ref[...])

    @pl.when(step == last)
    def _():
        wait_rows(1 - slot)


def _combine(pos_flat, ys, gates_t, x1, mod3, post_ffn_norm, seq):
    n, d = x1.shape
    seg = d // 2 // LANES
    tc = _tile(seq, 256)
    per_batch = seq // tc
    return pl.pallas_call(
        functools.partial(_combine_kernel, n_tok=n),
        out_shape=jax.ShapeDtypeStruct((n, d), F32),
        grid_spec=pltpu.PrefetchScalarGridSpec(
            num_scalar_prefetch=1,
            grid=(n // tc,),
            in_specs=[pl.BlockSpec(memory_space=pl.ANY),
                      pl.BlockSpec((tc, TOP_K), lambda i, pos: (i, 0)),
                      pl.BlockSpec((tc, d), lambda i, pos: (i, 0)),
                      pl.BlockSpec((1, 1, d), lambda i, pos: (i // per_batch, 0, 5)),
                      pl.BlockSpec((1, d), lambda i, pos: (0, 0))],
            out_specs=pl.BlockSpec((tc, d), lambda i, pos: (i, 0)),
            scratch_shapes=[pltpu.VMEM((2, TOP_K, tc * seg, LANES), U32), pltpu.VMEM((tc, d), F32),
                            pltpu.SemaphoreType.DMA((2, TOP_K))]),
        compiler_params=_params(("arbitrary",)),
        name="combine",
    )(pos_flat, ys, gates_t, x1, mod3, post_ffn_norm)


def _moe(h2p, logits_t, x1, mod3, post_ffn_norm, w1, b1, w2, b2, seq, n_exp):
    n, d = x1.shape
    seg = d // 2 // LANES
    tm = 1 << (min(n * TOP_K, 512).bit_length() - 1)
    idx, gates, rank, cnt = _route(logits_t, n_exp)

    counts = cnt[:n_exp, 0].astype(jnp.int32)
    padded = (counts + tm - 1) // tm * tm
    ends = jnp.cumsum(padded)
    starts = ends - padded
    hot = idx[:, :, None] == jnp.arange(n_exp, dtype=jnp.int32)
    pos_flat = (jnp.sum(jnp.where(hot, starts, 0), axis=-1) + rank).reshape(-1)
    n_tiles = -(-n * TOP_K // tm) + n_exp
    n_used = (ends[-1] // tm).astype(jnp.int32)
    tile_start = jnp.minimum(jnp.arange(n_tiles, dtype=jnp.int32), n_used - 1) * tm
    tile_expert = jnp.sum(tile_start[:, None] >= ends[None, :], axis=1).astype(jnp.int32)
    tile_hot = tile_expert[:, None] == jnp.arange(n_exp, dtype=jnp.int32)
    tile_rows_left = jnp.sum(jnp.where(tile_hot, starts + counts, 0), axis=1) - tile_start
    tile_valid = jnp.where(jnp.arange(n_tiles) < n_used, jnp.clip(tile_rows_left, 0, tm), 0).astype(jnp.int32)
    n_used = n_used.reshape(1)

    xs = _dispatch(pos_flat, starts + counts, padded - counts, h2p, n_tiles * tm, seg, tm)
    act = _ffn1(tile_expert, n_used, tile_valid, xs, w1, b1[:, None, :], tm, seg)
    ys = _ffn2(tile_expert, n_used, tile_valid, act, w2, b2[:, None, :], tm)
    return _combine(pos_flat, ys, gates.T, x1, mod3, post_ffn_norm, seq)


def kernel(x, c, ctx, c_ctx, ada_w, ada_b, pre_mix_norm, post_mix_norm, pre_ffn_norm, post_ffn_norm,
           w_in, ret_decay_fwd, ret_decay_bwd, ret_gn_w, conv_w, conv_b, conv_ln_w, conv_ln_b, w_out,
           router_w, router_b, w1, b1, w2, b2):
    assert ada_w.shape[0] == 1, "single-layer stack only"
    b, t, d = x.shape
    n_ctx = ctx.shape[1]
    n_heads = ret_decay_fwd.shape[1]
    d_ret = n_heads * HEAD_DIM
    n_exp = router_w.shape[2]
    assert b < MOD_ROWS and ret_gn_w.shape[1] == d_ret and d % (2 * LANES) == 0

    cc = jnp.zeros((MOD_ROWS, d), F32).at[:b].set(c).at[b].set(c_ctx)
    mod3 = _ada(cc, ada_w[0], ada_b)[:, None, :]
    cos, sin = _rope_tables(n_ctx + t)

    w_in_bf = w_in[0].astype(BF16)
    proj = _in_proj(x, pre_mix_norm, mod3, lambda bi: bi, w_in_bf, 0, w_in.shape[2])
    ctx_kv = _in_proj(ctx.reshape(1, b * n_ctx, d), pre_mix_norm, mod3, lambda bi: b, w_in_bf, d_ret, 2 * d_ret)

    lane_bcast = lambda v: jnp.broadcast_to(v.reshape(n_heads, 1, 1), (n_heads, 1, LANES))
    y_ret = _retention(proj, ctx_kv, cos, sin, lane_bcast(ret_decay_fwd[0]), lane_bcast(ret_decay_bwd[0]),
                       ret_gn_w, n_heads, n_ctx)
    y_conv = _conv(proj, 4 * d_ret, conv_w[0], conv_b, conv_ln_w, conv_ln_b)

    e_rows = -(-n_exp // SUBLANES) * SUBLANES
    rw_t = jnp.zeros((e_rows, d), F32).at[:n_exp].set(router_w[0].T)
    rw_hi = rw_t.astype(BF16)
    rw_lo = (rw_t - rw_hi.astype(F32)).astype(BF16)
    rb_col = jnp.zeros((e_rows, 1), F32).at[:n_exp, 0].set(router_b[0])
    x1, h2p, logits_t = _out_proj(y_ret, y_conv, w_out[0].astype(BF16), x, mod3, post_mix_norm, pre_ffn_norm,
                                  jnp.concatenate([rw_hi, rw_lo], axis=0), rb_col)

    out = _moe(h2p, logits_t, x1.reshape(b * t, d), mod3, post_ffn_norm, w1[0], b1[0], w2[0], b2[0], t, n_exp)
    return out.reshape(b, t, d)
```

```python
import functools
import math

import jax
import jax.numpy as jnp
from jax import lax
from jax.experimental import pallas as pl
from jax.experimental.pallas import tpu as pltpu

F32 = jnp.float32
BF16 = jnp.bfloat16
U32 = jnp.uint32

GRID_W = 64
HEAD_DIM = 128
RET_CHUNK = 256
ROPE_BASE = 10000.0
TOP_K = 4
SWIGLU_ALPHA = 1.702
SWIGLU_LIMIT = 7.0
EPS = 1e-6
GN_EPS = 1e-5
LANES = 128
SUBLANES = 8
MOD_ROWS = 16
VMEM_LIMIT = 56 * 1024 * 1024
HI16 = 0xFFFF0000
ROW_GROUPS = 1


def _tile(n, pref):
    t = min(n, pref)
    while n % t:
        t -= 1
    return t


def _params(sem, vmem=VMEM_LIMIT):
    return pltpu.CompilerParams(dimension_semantics=sem, vmem_limit_bytes=vmem)


def _rms(x, w):
    return x * lax.rsqrt(jnp.mean(x * x, axis=-1, keepdims=True) + EPS) * w


def _silu(x):
    return x * jax.nn.sigmoid(x)


def _dot_nt(a, b):
    return lax.dot_general(a, b, (((1,), (1,)), ((), ())), preferred_element_type=F32)


def _dot_tn(a, b):
    return lax.dot_general(a, b, (((0,), (0,)), ((), ())), preferred_element_type=F32)


def _ada_kernel(c_ref, w_ref, b_ref, o_ref):
    s = _silu(c_ref[...])
    w = w_ref[...]
    s_hi, w_hi = s.astype(BF16), w.astype(BF16)
    s_lo = (s - s_hi.astype(F32)).astype(BF16)
    w_lo = (w - w_hi.astype(F32)).astype(BF16)
    o_ref[...] = (jnp.dot(s_hi, w_hi, preferred_element_type=F32) + jnp.dot(s_lo, w_hi, preferred_element_type=F32)
                  + jnp.dot(s_hi, w_lo, preferred_element_type=F32) + b_ref[...])


def _ada(cc, w, b):
    d, n = w.shape
    tn = _tile(n, 1024)
    return pl.pallas_call(
        _ada_kernel,
        out_shape=jax.ShapeDtypeStruct((MOD_ROWS, n), F32),
        grid=(n // tn,),
        in_specs=[pl.BlockSpec((MOD_ROWS, d), lambda j: (0, 0)),
                  pl.BlockSpec((d, tn), lambda j: (0, j)),
                  pl.BlockSpec((1, tn), lambda j: (0, j))],
        out_specs=pl.BlockSpec((MOD_ROWS, tn), lambda j: (0, j)),
        compiler_params=_params(("parallel",)),
        name="ada",
    )(cc, w, b)


def _rope_kernel(cos_ref, sin_ref):
    p, _ = cos_ref.shape
    half = HEAD_DIM // 2
    lane = lax.broadcasted_iota(jnp.int32, (p, HEAD_DIM), 1)
    pos = lax.broadcasted_iota(jnp.int32, (p, HEAD_DIM), 0).astype(F32)
    j = jnp.where(lane < half, lane, lane - half).astype(F32)
    inv = jnp.exp(j * (-jnp.log(ROPE_BASE) / half))
    ang = pos * inv
    cos_ref[...] = jnp.cos(ang)
    sin_ref[...] = jnp.where(lane < half, -1.0, 1.0) * jnp.sin(ang)


def _rope_tables(p):
    return pl.pallas_call(
        _rope_kernel,
        out_shape=(jax.ShapeDtypeStruct((p, HEAD_DIM), F32), jax.ShapeDtypeStruct((p, HEAD_DIM), F32)),
        name="rope",
    )()


def _rot(t, cos, sin_signed):
    return t * cos + pltpu.roll(t, HEAD_DIM // 2, axis=1) * sin_signed


def _inproj_kernel(x_ref, nw_ref, sh_ref, sc_ref, w_ref, o_ref, h_scr):
    @pl.when(pl.program_id(2) == 0)
    def _():
        rows = math.gcd(x_ref.shape[1], LANES)
        gain = nw_ref[...] * (1.0 + sc_ref[0])
        shift = sh_ref[0]

        def body(r, carry):
            sl = pl.ds(pl.multiple_of(r * rows, rows), rows)
            x = x_ref[0, sl, :]
            rs = lax.rsqrt(jnp.mean(x * x, axis=-1, keepdims=True) + EPS)
            h_scr[sl, :] = (x * rs * gain + shift).astype(BF16)
            return carry
        lax.fori_loop(0, x_ref.shape[1] // rows, body, 0)

    res = jnp.dot(h_scr[...], w_ref[...], preferred_element_type=F32)
    for cb in range(o_ref.shape[1]):
        o_ref[0, cb] = res[:, cb * LANES:(cb + 1) * LANES]


def _in_proj(x, norm_w, mod3, mod_row, w_bf, col0, ncols):
    b, t, d = x.shape
    tm = _tile(t, 1024)
    tn = _tile(math.gcd(ncols, col0), 1536)
    joff = col0 // tn
    nblk = tn // LANES
    return pl.pallas_call(
        _inproj_kernel,
        out_shape=jax.ShapeDtypeStruct((b, ncols // LANES, t, LANES), F32),
        grid=(b, t // tm, ncols // tn),
        in_specs=[pl.BlockSpec((1, tm, d), lambda bi, i, j: (bi, i, 0)),
                  pl.BlockSpec((1, d), lambda bi, i, j: (0, 0)),
                  pl.BlockSpec((1, 1, d), lambda bi, i, j: (mod_row(bi), 0, 0)),
                  pl.BlockSpec((1, 1, d), lambda bi, i, j: (mod_row(bi), 0, 1)),
                  pl.BlockSpec((d, tn), lambda bi, i, j: (0, j + joff))],
        out_specs=pl.BlockSpec((1, nblk, tm, LANES), lambda bi, i, j: (bi, j, i, 0)),
        scratch_shapes=[pltpu.VMEM((tm, d), BF16)],
        compiler_params=_params(("parallel", "parallel", "arbitrary")),
        name="in_proj",
    )(x, norm_w, mod3, mod3, w_bf)


def _log_sigmoid(x):
    return jnp.minimum(x, 0.0) - jnp.log(1.0 + jnp.exp(-jnp.abs(x)))


def _ret_kernel(q_ref, k_ref, v_ref, g_ref, kc_ref, vc_ref, cos_ref, sin_ref, decf_ref, decb_ref,
                gnw_ref, o_ref, q_scr, k_scr, kf_scr, kb_scr, v_scr, o_scr, din_scr, kvf_scr, kvb_scr,
                rf_scr, rb_scr, *, n_ctx):
    t = q_ref.shape[2]
    c = din_scr.shape[0]
    nc = t // c
    lgf = _log_sigmoid(decf_ref[0])[:, 0:1]
    lgb = _log_sigmoid(decb_ref[0])[:, 0:1]

    kc = _rot(kc_ref[0, 0], cos_ref[0:n_ctx, :], sin_ref[0:n_ctx, :])
    vc = vc_ref[0, 0].astype(BF16)
    tc = lax.broadcasted_iota(jnp.int32, (n_ctx, 1), 0).astype(F32)
    rf = _dot_tn((kc * jnp.exp(lgf * (n_ctx - 1.0 - tc))).astype(BF16), vc)
    rb = _dot_tn((kc * jnp.exp(lgb * tc)).astype(BF16), vc)

    scale = HEAD_DIM ** -0.5
    cos, sin = cos_ref[n_ctx:n_ctx + t, :], sin_ref[n_ctx:n_ctx + t, :]
    q_scr[...] = (_rot(q_ref[0, 0], cos, sin) * scale).astype(BF16)
    kr = _rot(k_ref[0, 0], cos, sin)
    ri_all = (lax.broadcasted_iota(jnp.int32, (t, HEAD_DIM), 0) % c).astype(F32)
    k_scr[...] = kr.astype(BF16)
    kf_scr[...] = (kr * jnp.exp(lgf * (c - 1.0 - ri_all))).astype(BF16)
    kb_scr[...] = (kr * jnp.exp(lgb * ri_all)).astype(BF16)
    v_scr[...] = v_ref[0, 0].astype(BF16)

    diff = (lax.broadcasted_iota(jnp.int32, (c, c), 0) - lax.broadcasted_iota(jnp.int32, (c, c), 1)).astype(F32)
    din_scr[...] = jnp.where(diff > 0, jnp.exp(lgf * jnp.maximum(diff, 0.0)),
                             jnp.where(diff < 0, jnp.exp(lgb * jnp.maximum(-diff, 0.0)), 2.0))

    for i in range(nc):
        sl = slice(i * c, (i + 1) * c)
        vi = v_scr[sl, :]
        scores = _dot_nt(q_scr[sl, :], k_scr[sl, :]) * din_scr[...]
        o_scr[sl, :] = jnp.dot(scores.astype(BF16), vi, preferred_element_type=F32)
        kvf_scr[i] = _dot_tn(kf_scr[sl, :], vi)
        kvb_scr[i] = _dot_tn(kb_scr[sl, :], vi)

    gc_f = jnp.exp(lgf * float(c))
    gc_b = jnp.exp(lgb * float(c))
    for i in range(nc):
        rf_scr[i] = rf.astype(BF16)
        rf = rf * gc_f + kvf_scr[i]
    for i in reversed(range(nc)):
        rb_scr[i] = rb.astype(BF16)
        rb = rb * gc_b + kvb_scr[i]

    ri = lax.broadcasted_iota(jnp.int32, (c, HEAD_DIM), 0).astype(F32)
    dq_f = jnp.exp(lgf * (ri + 1.0))
    dq_b = jnp.exp(lgb * (c - ri))
    for i in range(nc):
        sl = slice(i * c, (i + 1) * c)
        qi = q_scr[sl, :]
        o = (o_scr[sl, :] + jnp.dot(qi, rf_scr[i], preferred_element_type=F32) * dq_f
             + jnp.dot(qi, rb_scr[i], preferred_element_type=F32) * dq_b)
        mu = jnp.mean(o, axis=-1, keepdims=True)
        var = jnp.mean(jnp.square(o - mu), axis=-1, keepdims=True)
        on = (o - mu) * lax.rsqrt(var + GN_EPS) * gnw_ref[...]
        o_ref[0, sl, :] = (on * _silu(g_ref[0, 0, sl, :])).astype(o_ref.dtype)


def _retention(proj, ctx_kv, cos, sin, decf3, decb3, gn_w, n_heads, n_ctx):
    b, _, t, hd = proj.shape
    c = _tile(t, RET_CHUNK)
    nc = t // c
    lat = lambda g: pl.BlockSpec((1, 1, t, hd), lambda bi, h: (bi, g * n_heads + h, 0, 0))
    seq = lambda dt: pltpu.VMEM((t, hd), dt)
    return pl.pallas_call(
        functools.partial(_ret_kernel, n_ctx=n_ctx),
        out_shape=jax.ShapeDtypeStruct((b, t, n_heads * hd), BF16),
        grid=(b, n_heads),
        in_specs=[lat(0), lat(1), lat(2), lat(3),
                  pl.BlockSpec((1, 1, n_ctx, hd), lambda bi, h: (0, h, bi, 0)),
                  pl.BlockSpec((1, 1, n_ctx, hd), lambda bi, h: (0, n_heads + h, bi, 0)),
                  pl.BlockSpec(cos.shape, lambda bi, h: (0, 0)),
                  pl.BlockSpec(sin.shape, lambda bi, h: (0, 0)),
                  pl.BlockSpec((1, 1, LANES), lambda bi, h: (h, 0, 0)),
                  pl.BlockSpec((1, 1, LANES), lambda bi, h: (h, 0, 0)),
                  pl.BlockSpec((1, hd), lambda bi, h: (0, h))],
        out_specs=pl.BlockSpec((1, t, hd), lambda bi, h: (bi, 0, h)),
        scratch_shapes=[seq(BF16), seq(BF16), seq(BF16), seq(BF16), seq(BF16), seq(F32),
                        pltpu.VMEM((c, c), F32),
                        pltpu.VMEM((nc, hd, hd), F32), pltpu.VMEM((nc, hd, hd), F32),
                        pltpu.VMEM((nc, hd, hd), BF16), pltpu.VMEM((nc, hd, hd), BF16)],
        compiler_params=_params(("parallel", "parallel")),
        name="ret",
    )(proj, proj, proj, proj, ctx_kv, ctx_kv, cos, sin, decf3, decb3, gn_w)


CONV_PAD_ROWS = 16
CONV_LANE_CHUNK = 256


def _conv_kernel(a_ref, b_ref, w_ref, cb_ref, lnw_ref, lnb_ref, o_ref, up_scr, sh_scr, y_scr):
    nblk, tt = a_ref.shape[1], a_ref.shape[2]
    ch = nblk * LANES
    kw = w_ref.shape[0]
    n_seq = tt // GRID_W
    lead = CONV_PAD_ROWS - kw // 2
    rows = GRID_W + 2 * CONV_PAD_ROWS
    zeros = jnp.zeros((CONV_PAD_ROWS, ch), F32)
    for s in range(n_seq):
        up_scr[s, 0:CONV_PAD_ROWS, :] = zeros
        up_scr[s, CONV_PAD_ROWS + GRID_W:, :] = zeros
    for cb in range(nblk):
        u = a_ref[0, cb] * jax.nn.sigmoid(b_ref[0, cb])
        for s in range(n_seq):
            up_scr[s, CONV_PAD_ROWS:CONV_PAD_ROWS + GRID_W, cb * LANES:(cb + 1) * LANES] = (
                u[s * GRID_W:(s + 1) * GRID_W, :])

    cw = sh_scr.shape[2]

    def seq(s, carry):
        row0 = pl.multiple_of(s * GRID_W, GRID_W)
        for c0 in range(0, ch, cw):
            for r in range(SUBLANES):
                sh_scr[r] = up_scr[s, r:r + rows - SUBLANES, c0:c0 + cw]
            acc = jnp.broadcast_to(cb_ref[:, c0:c0 + cw], (GRID_W, cw))
            for k in range(kw):
                a8, r = divmod(lead + k, SUBLANES)
                acc = acc + sh_scr[r, a8 * SUBLANES:a8 * SUBLANES + GRID_W, :] * w_ref[k:k + 1, c0:c0 + cw]
            y_scr[pl.ds(row0, GRID_W), c0:c0 + cw] = acc
        return carry

    lax.fori_loop(0, n_seq, seq, 0)
    y = y_scr[...]
    mu = jnp.mean(y, axis=-1, keepdims=True)
    var = jnp.mean(jnp.square(y - mu), axis=-1, keepdims=True)
    yn = (y - mu) * lax.rsqrt(var + EPS) * lnw_ref[...] + lnb_ref[...]
    o_ref[0] = _silu(yn).astype(o_ref.dtype)


def _conv(proj, col0, conv_w, conv_b, ln_w, ln_b):
    b, _, t, _ = proj.shape
    kw, ch = conv_w.shape
    assert col0 % ch == 0 and ch % LANES == 0 and kw // 2 <= CONV_PAD_ROWS and t % GRID_W == 0
    nblk = ch // LANES
    tt = _tile(t, 8 * GRID_W)
    cw = min(ch, CONV_LANE_CHUNK)
    rows = GRID_W + 2 * CONV_PAD_ROWS
    ca, cb = col0 // ch, col0 // ch + 1
    vec = pl.BlockSpec((1, ch), lambda bi, i: (0, 0))
    return pl.pallas_call(
        _conv_kernel,
        out_shape=jax.ShapeDtypeStruct((b, t, ch), BF16),
        grid=(b, t // tt),
        in_specs=[pl.BlockSpec((1, nblk, tt, LANES), lambda bi, i: (bi, ca, i, 0)),
                  pl.BlockSpec((1, nblk, tt, LANES), lambda bi, i: (bi, cb, i, 0)),
                  pl.BlockSpec((kw, ch), lambda bi, i: (0, 0)),
                  vec, vec, vec],
        out_specs=pl.BlockSpec((1, tt, ch), lambda bi, i: (bi, i, 0)),
        scratch_shapes=[pltpu.VMEM((tt // GRID_W, rows, ch), F32),
                        pltpu.VMEM((SUBLANES, rows - SUBLANES, cw), F32),
                        pltpu.VMEM((tt, ch), F32)],
        compiler_params=_params(("parallel", "parallel")),
        name="conv",
    )(proj, proj, conv_w, conv_b, ln_w, ln_b)


def _outproj_kernel(yr_ref, yc_ref, wr_ref, wc_ref, x_ref, g1_ref, sh2_ref, sc2_ref, pmn_ref, pfn_ref,
                    rw_ref, rb_ref, x1_ref, h2p_ref, lg_ref):
    tm = x_ref.shape[1]
    e_rows = lg_ref.shape[0]
    mix = (jnp.dot(yr_ref[0], wr_ref[...], preferred_element_type=F32)
           + jnp.dot(yc_ref[0], wc_ref[...], preferred_element_type=F32))
    x1 = x_ref[0] + _rms(mix, g1_ref[0] * pmn_ref[...])
    x1_ref[0] = x1
    h2 = _rms(x1, pfn_ref[...] * (1.0 + sc2_ref[0])) + sh2_ref[0]
    h2_hi = h2.astype(BF16)
    h2_lo = (h2 - h2_hi.astype(F32)).astype(BF16)
    by_hi = _dot_nt(rw_ref[...], h2_hi)
    lg_ref[...] = by_hi[:e_rows] + by_hi[e_rows:] + _dot_nt(rw_ref[:e_rows, :], h2_lo) + rb_ref[...]
    _pack_rows(h2, h2p_ref, tm)


def _out_proj(y_ret, y_conv, w_out_bf, x, mod3, post_mix_norm, pre_ffn_norm, rw_hi_lo, rb_col):
    b, t, d = x.shape
    d_ret, d_conv = y_ret.shape[2], y_conv.shape[2]
    e_rows = rb_col.shape[0]
    seg = d // 2 // LANES
    tm = _tile(t, 512)
    per_b = t // tm
    mod = lambda col: pl.BlockSpec((1, 1, d), lambda bi, i: (bi, 0, col))
    vec = pl.BlockSpec((1, d), lambda bi, i: (0, 0))
    row = lambda width: pl.BlockSpec((1, tm, width), lambda bi, i: (bi, i, 0))
    rw = pl.BlockSpec((2 * e_rows, d), lambda bi, i: (0, 0))
    w_ret, w_conv = w_out_bf[:d_ret], w_out_bf[d_ret:]
    return pl.pallas_call(
        _outproj_kernel,
        out_shape=(jax.ShapeDtypeStruct((b, t, d), F32),
                   jax.ShapeDtypeStruct((b * t * seg, LANES), U32),
                   jax.ShapeDtypeStruct((e_rows, b * t), F32)),
        grid=(b, per_b),
        in_specs=[row(d_ret), row(d_conv),
                  pl.BlockSpec((d_ret, d), lambda bi, i: (0, 0)),
                  pl.BlockSpec((d_conv, d), lambda bi, i: (0, 0)),
                  row(d), mod(2), mod(3), mod(4), vec, vec, rw,
                  pl.BlockSpec((e_rows, 1), lambda bi, i: (0, 0))],
        out_specs=(row(d),
                   pl.BlockSpec((tm * seg, LANES), lambda bi, i: (bi * per_b + i, 0)),
                   pl.BlockSpec((e_rows, tm), lambda bi, i: (0, bi * per_b + i))),
        compiler_params=_params(("parallel", "parallel")),
        name="out_proj",
    )(y_ret, y_conv, w_ret, w_conv, x, mod3, mod3, mod3, post_mix_norm, pre_ffn_norm, rw_hi_lo, rb_col)


def _route_kernel(lg_ref, idx_ref, gate_ref, rank_ref, cnt_ref, tri_scr, run_scr, *, n_experts):
    e_rows, tr = lg_ref.shape

    @pl.when(pl.program_id(0) == 0)
    def _():
        r = lax.broadcasted_iota(jnp.int32, (tr, tr), 0)
        c = lax.broadcasted_iota(jnp.int32, (tr, tr), 1)
        tri_scr[...] = jnp.where(r <= c, 1.0, 0.0).astype(BF16)
        run_scr[...] = jnp.zeros_like(run_scr)

    e_iota = lax.broadcasted_iota(jnp.int32, (e_rows, tr), 0)
    neg = jnp.float32(-jnp.inf)
    logits = jnp.where(e_iota < n_experts, lg_ref[...], neg)
    vals, hots = [], []
    for k in range(TOP_K):
        m = jnp.max(logits, axis=0, keepdims=True)
        ik = jnp.min(jnp.where(logits == m, e_iota, e_rows), axis=0, keepdims=True)
        hot = e_iota == ik
        logits = jnp.where(hot, neg, logits)
        vals.append(m)
        hots.append(hot)
        idx_ref[k:k + 1, :] = ik
    exps = [jnp.exp(v - vals[0]) for v in vals]
    den = exps[0]
    for e in exps[1:]:
        den = den + e
    for k in range(TOP_K):
        gate_ref[k:k + 1, :] = exps[k] / den

    sel = jnp.zeros((e_rows, tr), F32)
    for hot in hots:
        sel = sel + jnp.where(hot, 1.0, 0.0)
    csum = jnp.dot(sel.astype(BF16), tri_scr[...], preferred_element_type=F32)
    before = run_scr[:, 0:1] + csum - sel
    for k in range(TOP_K):
        rank_ref[k:k + 1, :] = jnp.sum(jnp.where(hots[k], before, 0.0), axis=0, keepdims=True).astype(jnp.int32)
    run_scr[...] = run_scr[...] + jnp.sum(sel, axis=1, keepdims=True)
    cnt_ref[...] = run_scr[...]


def _route(logits_t, n_experts):
    e_rows, n = logits_t.shape
    tr = _tile(n, 512)
    kt = lambda dt: jax.ShapeDtypeStruct((TOP_K, n), dt)
    blk = pl.BlockSpec((TOP_K, tr), lambda i: (0, i))
    return pl.pallas_call(
        functools.partial(_route_kernel, n_experts=n_experts),
        out_shape=(kt(jnp.int32), kt(F32), kt(jnp.int32), jax.ShapeDtypeStruct((e_rows, LANES), F32)),
        grid=(n // tr,),
        in_specs=[pl.BlockSpec((e_rows, tr), lambda i: (0, i))],
        out_specs=(blk, blk, blk, pl.BlockSpec((e_rows, LANES), lambda i: (0, 0))),
        scratch_shapes=[pltpu.VMEM((tr, tr), BF16), pltpu.VMEM((e_rows, LANES), F32)],
        compiler_params=_params(("arbitrary",)),
        name="route",
    )(logits_t)


def _dispatch_kernel(pos_ref, poff_ref, plen_ref, h_ref, xs_ref, zero_scr, sem, zsem, *, n_tok, n_exp, seg, tm):
    td = h_ref.shape[0] // seg
    step = pl.program_id(0)
    base = step * td

    def pad_copies(e, act):
        off, ln = poff_ref[e], plen_ref[e]
        for bit in range(tm.bit_length() - 1):
            size = 1 << bit

            @pl.when(((ln >> bit) & 1) == 1)
            def _():
                row = pl.multiple_of((off + (ln & (size - 1))) * seg, seg)
                act(pltpu.make_async_copy(zero_scr.at[pl.ds(0, size * seg)],
                                          xs_ref.at[pl.ds(row, size * seg)], zsem))

    def for_pads(act):
        def body(e, carry):
            pad_copies(e, act)
            return carry
        lax.fori_loop(0, n_exp, body, 0)

        half = zero_scr.shape[0]
        end = (poff_ref[n_exp - 1] + plen_ref[n_exp - 1]) * seg

        def tail(n, carry):
            row = pl.multiple_of(end + n * half, seg)
            act(pltpu.make_async_copy(zero_scr, xs_ref.at[pl.ds(row, half)], zsem))
            return carry
        lax.fori_loop(0, (xs_ref.shape[0] - end) // half, tail, 0)

    @pl.when(step == 0)
    def _():
        zero_scr[...] = jnp.zeros_like(zero_scr)
        for_pads(lambda cp: cp.start())

    def row_copy(t, k):
        p = pos_ref[k * n_tok + base + t]
        return pltpu.make_async_copy(h_ref.at[pl.ds(pl.multiple_of(t * seg, seg), seg)],
                                     xs_ref.at[pl.ds(pl.multiple_of(p * seg, seg), seg)], sem)

    def start(t, carry):
        for k in range(TOP_K):
            row_copy(t, k).start(priority=k % 2)
        return carry

    lax.fori_loop(0, td, start, 0, unroll=8)
    for k in range(TOP_K):
        pltpu.make_async_copy(h_ref, h_ref, sem).wait()

    @pl.when(step == 0)
    def _():
        for_pads(lambda cp: cp.wait())


def _dispatch(pos_flat, pad_off, pad_len, h2p, n_slots, seg, tm):
    n = h2p.shape[0] // seg
    n_exp = pad_off.shape[0]
    assert tm & (tm - 1) == 0
    td = _tile(n, 512)
    return pl.pallas_call(
        functools.partial(_dispatch_kernel, n_tok=n, n_exp=n_exp, seg=seg, tm=tm),
        out_shape=jax.ShapeDtypeStruct((n_slots * seg, LANES), U32),
        grid_spec=pltpu.PrefetchScalarGridSpec(
            num_scalar_prefetch=3,
            grid=(n // td,),
            in_specs=[pl.BlockSpec((td * seg, LANES), lambda i, *_: (i, 0))],
            out_specs=pl.BlockSpec(memory_space=pl.ANY),
            scratch_shapes=[pltpu.VMEM((max(tm // 2, 1) * seg, LANES), U32),
                            pltpu.SemaphoreType.DMA(()), pltpu.SemaphoreType.DMA(())]),
        compiler_params=_params(("arbitrary",)),
        name="dispatch",
    )(pos_flat, pad_off, pad_len, h2p)


def _pack_rows(val, ref, tm):
    half = val.shape[1] // 2
    seg = half // LANES
    vb = val.astype(BF16).astype(F32)
    packed = ((lax.bitcast_convert_type(vb[:, :half], U32) >> 16)
              | (lax.bitcast_convert_type(vb[:, half:], U32) & jnp.uint32(HI16)))
    for s in range(seg):
        ref[pl.ds(s, tm, stride=seg), :] = packed[:, s * LANES:(s + 1) * LANES]


def _unpack_seg(words):
    return (lax.bitcast_convert_type(words << 16, F32),
            lax.bitcast_convert_type(words & jnp.uint32(HI16), F32))


def _weight_runs(tile_expert, n_passes):
    n_tiles = tile_expert.shape[0]
    total = n_tiles * n_passes
    e_lin = jnp.tile(tile_expert, n_passes)
    j_lin = jnp.repeat(jnp.arange(n_passes, dtype=jnp.int32), n_tiles)
    i_lin = jnp.tile(jnp.arange(n_tiles, dtype=jnp.int32), n_passes)
    start = (i_lin == 0) | (e_lin != jnp.roll(e_lin, 1))
    idx = jnp.arange(total, dtype=jnp.int32)
    nxt = jnp.min(jnp.where(start[None, :] & (idx[None, :] > idx[:, None]), idx[None, :], total), axis=1)
    has = nxt < total
    nxt = jnp.minimum(nxt, total - 1)
    return start.astype(jnp.int32), jnp.where(has, e_lin[nxt], -1).astype(jnp.int32), j_lin[nxt]


def _stream_weights(step, first_ref, ne_ref, nj_ref, cur, copies, cast):
    @pl.when(step == 0)
    def _():
        for cp in copies(*cur):
            cp.start(priority=1)

    @pl.when(first_ref[step] == 1)
    def _():
        for cp in copies(*cur):
            cp.wait()
        cast()

        @pl.when(ne_ref[step] >= 0)
        def _():
            for cp in copies(ne_ref[step], nj_ref[step]):
                cp.start(priority=1)


def _ffn1_kernel(te_ref, nu_ref, first_ref, ne_ref, nj_ref, tv_ref, xs_ref, w_hbm, bg_ref, bl_ref, act_ref,
                 wg_stage, wl_stage, wg_scr, wl_scr, x_scr, sem, *, seg, d_ff):
    del nu_ref
    j, i = pl.program_id(0), pl.program_id(1)
    tm = x_scr.shape[0]
    tn = wg_scr.shape[1]
    half = seg * LANES

    def copies(e, jj):
        col = pl.multiple_of(jj * tn, tn)
        return (pltpu.make_async_copy(w_hbm.at[e, :, pl.ds(col, tn)], wg_stage, sem.at[0]),
                pltpu.make_async_copy(w_hbm.at[e, :, pl.ds(d_ff + col, tn)], wl_stage, sem.at[1]))

    def cast():
        wg_scr[...] = wg_stage[...].astype(BF16)
        wl_scr[...] = wl_stage[...].astype(BF16)

    _stream_weights(j * pl.num_programs(1) + i, first_ref, ne_ref, nj_ref, (te_ref[i], j), copies, cast)

    rows = tm // ROW_GROUPS
    for r0 in range(0, tm, rows):
        @pl.when(tv_ref[i] > r0)
        def _():
            for s in range(seg):
                lo, hi = _unpack_seg(xs_ref[pl.ds(r0 * seg + s, rows, stride=seg), :])
                x_scr[r0:r0 + rows, s * LANES:(s + 1) * LANES] = lo.astype(BF16)
                x_scr[r0:r0 + rows, half + s * LANES:half + (s + 1) * LANES] = hi.astype(BF16)
            x = x_scr[r0:r0 + rows, :]
            glu = jnp.minimum(jnp.dot(x, wg_scr[...], preferred_element_type=F32) + bg_ref[0], SWIGLU_LIMIT)
            lin = jnp.clip(jnp.dot(x, wl_scr[...], preferred_element_type=F32) + bl_ref[0],
                           -SWIGLU_LIMIT, SWIGLU_LIMIT)
            act_ref[r0:r0 + rows, :] = (glu * jax.nn.sigmoid(SWIGLU_ALPHA * glu) * (lin + 1.0)).astype(act_ref.dtype)

        @pl.when(tv_ref[i] <= r0)
        def _():
            act_ref[r0:r0 + rows, :] = jnp.zeros((rows, tn), act_ref.dtype)


def _ffn1(tile_expert, n_used, tile_valid, xs, w1, b1_3, tm, seg):
    d = seg * LANES * 2
    slots = xs.shape[0] // seg
    d_ff = w1.shape[2] // 2
    tn = _tile(d_ff, 1024)
    nj = d_ff // tn
    n_tiles = slots // tm
    row = lambda i, nu: jnp.minimum(i, nu[0] - 1)
    return pl.pallas_call(
        functools.partial(_ffn1_kernel, seg=seg, d_ff=d_ff),
        out_shape=jax.ShapeDtypeStruct((slots, d_ff), BF16),
        grid_spec=pltpu.PrefetchScalarGridSpec(
            num_scalar_prefetch=6,
            grid=(nj, n_tiles),
            in_specs=[pl.BlockSpec((tm * seg, LANES), lambda j, i, te, nu, *_: (row(i, nu), 0)),
                      pl.BlockSpec(memory_space=pl.ANY),
                      pl.BlockSpec((1, 1, tn), lambda j, i, te, *_: (te[i], 0, j)),
                      pl.BlockSpec((1, 1, tn), lambda j, i, te, *_: (te[i], 0, nj + j))],
            out_specs=pl.BlockSpec((tm, tn), lambda j, i, *_: (i, j)),
            scratch_shapes=[pltpu.VMEM((d, tn), F32), pltpu.VMEM((d, tn), F32),
                            pltpu.VMEM((d, tn), BF16), pltpu.VMEM((d, tn), BF16),
                            pltpu.VMEM((tm, d), BF16), pltpu.SemaphoreType.DMA((2,))]),
        compiler_params=_params(("arbitrary", "arbitrary")),
        name="ffn1",
    )(tile_expert, n_used, *_weight_runs(tile_expert, nj), tile_valid, xs, w1, b1_3, b1_3)


def _ffn2_kernel(te_ref, nu_ref, first_ref, ne_ref, nj_ref, tv_ref, act_ref, w_hbm, b_ref, ys_ref,
                 w_stage, w_scr, sem):
    del nu_ref
    i = pl.program_id(0)
    tm = act_ref.shape[0]

    def copies(e, jj):
        del jj
        return (pltpu.make_async_copy(w_hbm.at[e], w_stage, sem),)

    def cast():
        w_scr[...] = w_stage[...].astype(BF16)

    _stream_weights(i, first_ref, ne_ref, nj_ref, (te_ref[i], 0), copies, cast)

    rows = tm // ROW_GROUPS
    seg = ys_ref.shape[0] // tm
    for r0 in range(0, tm, rows):
        ys_rows = ys_ref.at[pl.ds(r0 * seg, rows * seg)]

        @pl.when(tv_ref[i] > r0)
        def _():
            y = jnp.dot(act_ref[r0:r0 + rows, :], w_scr[...], preferred_element_type=F32) + b_ref[0]
            _pack_rows(y, ys_rows, rows)

        @pl.when(tv_ref[i] <= r0)
        def _():
            ys_rows[...] = jnp.zeros(ys_rows.shape, ys_rows.dtype)


def _ffn2(tile_expert, n_used, tile_valid, act, w2, b2_3, tm):
    slots, d_ff = act.shape
    d = w2.shape[2]
    seg = d // 2 // LANES
    n_tiles = slots // tm
    row = lambda i, nu: jnp.minimum(i, nu[0] - 1)
    return pl.pallas_call(
        _ffn2_kernel,
        out_shape=jax.ShapeDtypeStruct((slots * seg, LANES), U32),
        grid_spec=pltpu.PrefetchScalarGridSpec(
            num_scalar_prefetch=6,
            grid=(n_tiles,),
            in_specs=[pl.BlockSpec((tm, d_ff), lambda i, te, nu, *_: (row(i, nu), 0)),
                      pl.BlockSpec(memory_space=pl.ANY),
                      pl.BlockSpec((1, 1, d), lambda i, te, *_: (te[i], 0, 0))],
            out_specs=pl.BlockSpec((tm * seg, LANES), lambda i, *_: (i, 0)),
            scratch_shapes=[pltpu.VMEM((d_ff, d), F32), pltpu.VMEM((d_ff, d), BF16),
                            pltpu.SemaphoreType.DMA(())]),
        compiler_params=_params(("arbitrary",)),
        name="ffn2",
    )(tile_expert, n_used, *_weight_runs(tile_expert, 1), tile_valid, act, w2, b2_3)


def _combine_kernel(pos_ref, ys_ref, gate_ref, x1_ref, g2_ref, pfn_ref, o_ref, buf, moe_scr, sem, *, n_tok):
    tc, d = x1_ref.shape
    seg = buf.shape[2] // tc
    half = seg * LANES
    step = pl.program_id(0)

    last = pl.num_programs(0) - 1

    def start_row(st, slot, t):
        for k in range(TOP_K):
            p = pos_ref[k * n_tok + st * tc + t]
            pltpu.make_async_copy(ys_ref.at[pl.ds(pl.multiple_of(p * seg, seg), seg)],
                                  buf.at[slot, k, pl.ds(pl.multiple_of(t * seg, seg), seg)],
                                  sem.at[slot, k]).start(priority=k % 2)

    def wait_rows(slot):
        for k in range(TOP_K):
            pltpu.make_async_copy(buf.at[slot, k], buf.at[slot, k], sem.at[slot, k]).wait()

    @pl.when(step == 0)
    def _():
        def body(t, carry):
            start_row(0, 0, t)
            return carry
        lax.fori_loop(0, tc, body, 0)

    slot = step % 2
    wait_rows(slot)
    for t in range(tc):
        start_row(jnp.minimum(step + 1, last), 1 - slot, t)
    gates = [jnp.broadcast_to(gate_ref[:, k:k + 1], (tc, LANES)) for k in range(TOP_K)]
    for s in range(seg):
        m_lo = m_hi = None
        for k in range(TOP_K):
            lo, hi = _unpack_seg(buf[slot, k, pl.ds(s, tc, stride=seg), :])
            m_lo = lo * gates[k] if m_lo is None else m_lo + lo * gates[k]
            m_hi = hi * gates[k] if m_hi is None else m_hi + hi * gates[k]
        moe_scr[:, s * LANES:(s + 1) * LANES] = m_lo
        moe_scr[:, half + s * LANES:half + (s + 1) * LANES] = m_hi
    o_ref[...] = x1_ref[...] + g2_ref[0] * _rms(moe_scr[...], pfn_ref[...])

    @pl.when(step == last)
    def _():
        wait_rows(1 - slot)


def _combine(pos_flat, ys, gates_t, x1, mod3, post_ffn_norm, seq):
    n, d = x1.shape
    seg = d // 2 // LANES
    tc = _tile(seq, 256)
    per_batch = seq // tc
    return pl.pallas_call(
        functools.partial(_combine_kernel, n_tok=n),
        out_shape=jax.ShapeDtypeStruct((n, d), F32),
        grid_spec=pltpu.PrefetchScalarGridSpec(
            num_scalar_prefetch=1,
            grid=(n // tc,),
            in_specs=[pl.BlockSpec(memory_space=pl.ANY),
                      pl.BlockSpec((tc, TOP_K), lambda i, pos: (i, 0)),
                      pl.BlockSpec((tc, d), lambda i, pos: (i, 0)),
                      pl.BlockSpec((1, 1, d), lambda i, pos: (i // per_batch, 0, 5)),
                      pl.BlockSpec((1, d), lambda i, pos: (0, 0))],
            out_specs=pl.BlockSpec((tc, d), lambda i, pos: (i, 0)),
            scratch_shapes=[pltpu.VMEM((2, TOP_K, tc * seg, LANES), U32), pltpu.VMEM((tc, d), F32),
                            pltpu.SemaphoreType.DMA((2, TOP_K))]),
        compiler_params=_params(("arbitrary",)),
        name="combine",
    )(pos_flat, ys, gates_t, x1, mod3, post_ffn_norm)


def _moe(h2p, logits_t, x1, mod3, post_ffn_norm, w1, b1, w2, b2, seq, n_exp):
    n, d = x1.shape
    seg = d // 2 // LANES
    tm = 1 << (min(n * TOP_K, 512).bit_length() - 1)
    idx, gates, rank, cnt = _route(logits_t, n_exp)

    counts = cnt[:n_exp, 0].astype(jnp.int32)
    padded = (counts + tm - 1) // tm * tm
    ends = jnp.cumsum(padded)
    starts = ends - padded
    hot = idx[:, :, None] == jnp.arange(n_exp, dtype=jnp.int32)
    pos_flat = (jnp.sum(jnp.where(hot, starts, 0), axis=-1) + rank).reshape(-1)
    n_tiles = -(-n * TOP_K // tm) + n_exp
    n_used = (ends[-1] // tm).astype(jnp.int32)
    tile_start = jnp.minimum(jnp.arange(n_tiles, dtype=jnp.int32), n_used - 1) * tm
    tile_expert = jnp.sum(tile_start[:, None] >= ends[None, :], axis=1).astype(jnp.int32)
    tile_hot = tile_expert[:, None] == jnp.arange(n_exp, dtype=jnp.int32)
    tile_rows_left = jnp.sum(jnp.where(tile_hot, starts + counts, 0), axis=1) - tile_start
    tile_valid = jnp.where(jnp.arange(n_tiles) < n_used, jnp.clip(tile_rows_left, 0, tm), 0).astype(jnp.int32)
    n_used = n_used.reshape(1)

    xs = _dispatch(pos_flat, starts + counts, padded - counts, h2p, n_tiles * tm, seg, tm)
    act = _ffn1(tile_expert, n_used, tile_valid, xs, w1, b1[:, None, :], tm, seg)
    ys = _ffn2(tile_expert, n_used, tile_valid, act, w2, b2[:, None, :], tm)
    return _combine(pos_flat, ys, gates.T, x1, mod3, post_ffn_norm, seq)


def kernel(x, c, ctx, c_ctx, ada_w, ada_b, pre_mix_norm, post_mix_norm, pre_ffn_norm, post_ffn_norm,
           w_in, ret_decay_fwd, ret_decay_bwd, ret_gn_w, conv_w, conv_b, conv_ln_w, conv_ln_b, w_out,
           router_w, router_b, w1, b1, w2, b2):
    assert ada_w.shape[0] == 1, "single-layer stack only"
    b, t, d = x.shape
    n_ctx = ctx.shape[1]
    n_heads = ret_decay_fwd.shape[1]
    d_ret = n_heads * HEAD_DIM
    n_exp = router_w.shape[2]
    assert b < MOD_ROWS and ret_gn_w.shape[1] == d_ret and d % (2 * LANES) == 0

    cc = jnp.zeros((MOD_ROWS, d), F32).at[:b].set(c).at[b].set(c_ctx)
    mod3 = _ada(cc, ada_w[0], ada_b)[:, None, :]
    cos, sin = _rope_tables(n_ctx + t)

    w_in_bf = w_in[0].astype(BF16)
    proj = _in_proj(x, pre_mix_norm, mod3, lambda bi: bi, w_in_bf, 0, w_in.shape[2])
    ctx_kv = _in_proj(ctx.reshape(1, b * n_ctx, d), pre_mix_norm, mod3, lambda bi: b, w_in_bf, d_ret, 2 * d_ret)

    lane_bcast = lambda v: jnp.broadcast_to(v.reshape(n_heads, 1, 1), (n_heads, 1, LANES))
    y_ret = _retention(proj, ctx_kv, cos, sin, lane_bcast(ret_decay_fwd[0]), lane_bcast(ret_decay_bwd[0]),
                       ret_gn_w, n_heads, n_ctx)
    y_conv = _conv(proj, 4 * d_ret, conv_w[0], conv_b, conv_ln_w, conv_ln_b)

    e_rows = -(-n_exp // SUBLANES) * SUBLANES
    rw_t = jnp.zeros((e_rows, d), F32).at[:n_exp].set(router_w[0].T)
    rw_hi = rw_t.astype(BF16)
    rw_lo = (rw_t - rw_hi.astype(F32)).astype(BF16)
    rb_col = jnp.zeros((e_rows, 1), F32).at[:n_exp, 0].set(router_b[0])
    x1, h2p, logits_t = _out_proj(y_ret, y_conv, w_out[0].astype(BF16), x, mod3, post_mix_norm, pre_ffn_norm,
                                  jnp.concatenate([rw_hi, rw_lo], axis=0), rb_col)

    out = _moe(h2p, logits_t, x1.reshape(b * t, d), mod3, post_ffn_norm, w1[0], b1[0], w2[0], b2[0], t, n_exp)
    return out.reshape(b, t, d)
```

```python
import functools
import math

import jax
import jax.numpy as jnp
from jax import lax
from jax.experimental import pallas as pl
from jax.experimental.pallas import tpu as pltpu

F32 = jnp.float32
BF16 = jnp.bfloat16
U32 = jnp.float32

GRID_W = 64
HEAD_DIM = 128
RET_CHUNK = 256
ROPE_BASE = 10000.0
TOP_K = 4
SWIGLU_ALPHA = 1.702
SWIGLU_LIMIT = 7.0
EPS = 1e-6
GN_EPS = 1e-5
LANES = 128
SUBLANES = 8
MOD_ROWS = 16
VMEM_LIMIT = 56 * 1024 * 1024
HI16 = 0xFFFF0000


def _tile(n, pref):
    t = min(n, pref)
    while n % t:
        t -= 1
    return t


def _params(sem, vmem=VMEM_LIMIT):
    return pltpu.CompilerParams(dimension_semantics=sem, vmem_limit_bytes=vmem)


def _rms(x, w):
    return x * lax.rsqrt(jnp.mean(x * x, axis=-1, keepdims=True) + EPS) * w


def _silu(x):
    return x * jax.nn.sigmoid(x)


def _dot_nt(a, b):
    return lax.dot_general(a, b, (((1,), (1,)), ((), ())), preferred_element_type=F32)


def _dot_tn(a, b):
    return lax.dot_general(a, b, (((0,), (0,)), ((), ())), preferred_element_type=F32)


def _ada_kernel(c_ref, w_ref, b_ref, o_ref):
    s = _silu(c_ref[...])
    w = w_ref[...]
    s_hi, w_hi = s.astype(BF16), w.astype(BF16)
    s_lo = (s - s_hi.astype(F32)).astype(BF16)
    w_lo = (w - w_hi.astype(F32)).astype(BF16)
    o_ref[...] = (jnp.dot(s_hi, w_hi, preferred_element_type=F32) + jnp.dot(s_lo, w_hi, preferred_element_type=F32)
                  + jnp.dot(s_hi, w_lo, preferred_element_type=F32) + b_ref[...])


def _ada(cc, w, b):
    d, n = w.shape
    tn = _tile(n, 1024)
    return pl.pallas_call(
        _ada_kernel,
        out_shape=jax.ShapeDtypeStruct((MOD_ROWS, n), F32),
        grid=(n // tn,),
        in_specs=[pl.BlockSpec((MOD_ROWS, d), lambda j: (0, 0)),
                  pl.BlockSpec((d, tn), lambda j: (0, j)),
                  pl.BlockSpec((1, tn), lambda j: (0, j))],
        out_specs=pl.BlockSpec((MOD_ROWS, tn), lambda j: (0, j)),
        compiler_params=_params(("parallel",)),
        name="ada",
    )(cc, w, b)


def _rope_kernel(cos_ref, sin_ref):
    p, _ = cos_ref.shape
    half = HEAD_DIM // 2
    lane = lax.broadcasted_iota(jnp.int32, (p, HEAD_DIM), 1)
    pos = lax.broadcasted_iota(jnp.int32, (p, HEAD_DIM), 0).astype(F32)
    j = jnp.where(lane < half, lane, lane - half).astype(F32)
    inv = jnp.exp(j * (-jnp.log(ROPE_BASE) / half))
    ang = pos * inv
    cos_ref[...] = jnp.cos(ang)
    sin_ref[...] = jnp.where(lane < half, -1.0, 1.0) * jnp.sin(ang)


def _rope_tables(p):
    return pl.pallas_call(
        _rope_kernel,
        out_shape=(jax.ShapeDtypeStruct((p, HEAD_DIM), F32), jax.ShapeDtypeStruct((p, HEAD_DIM), F32)),
        name="rope",
    )()


def _rot(t, cos, sin_signed):
    return t * cos + pltpu.roll(t, HEAD_DIM // 2, axis=1) * sin_signed


def _inproj_kernel(x_ref, nw_ref, sh_ref, sc_ref, w_ref, o_ref, h_scr):
    @pl.when(pl.program_id(2) == 0)
    def _():
        rows = math.gcd(x_ref.shape[1], LANES)
        gain = nw_ref[...] * (1.0 + sc_ref[0])
        shift = sh_ref[0]

        def body(r, carry):
            sl = pl.ds(pl.multiple_of(r * rows, rows), rows)
            x = x_ref[0, sl, :]
            rs = lax.rsqrt(jnp.mean(x * x, axis=-1, keepdims=True) + EPS)
            h_scr[sl, :] = (x * rs * gain + shift).astype(BF16)
            return carry
        lax.fori_loop(0, x_ref.shape[1] // rows, body, 0)

    res = jnp.dot(h_scr[...], w_ref[...], preferred_element_type=F32)
    for cb in range(o_ref.shape[1]):
        o_ref[0, cb] = res[:, cb * LANES:(cb + 1) * LANES]


def _in_proj(x, norm_w, mod3, mod_row, w_bf, col0, ncols):
    b, t, d = x.shape
    tm = _tile(t, 1024)
    tn = _tile(math.gcd(ncols, col0), 1536)
    joff = col0 // tn
    nblk = tn // LANES
    return pl.pallas_call(
        _inproj_kernel,
        out_shape=jax.ShapeDtypeStruct((b, ncols // LANES, t, LANES), F32),
        grid=(b, t // tm, ncols // tn),
        in_specs=[pl.BlockSpec((1, tm, d), lambda bi, i, j: (bi, i, 0)),
                  pl.BlockSpec((1, d), lambda bi, i, j: (0, 0)),
                  pl.BlockSpec((1, 1, d), lambda bi, i, j: (mod_row(bi), 0, 0)),
                  pl.BlockSpec((1, 1, d), lambda bi, i, j: (mod_row(bi), 0, 1)),
                  pl.BlockSpec((d, tn), lambda bi, i, j: (0, j + joff))],
        out_specs=pl.BlockSpec((1, nblk, tm, LANES), lambda bi, i, j: (bi, j, i, 0)),
        scratch_shapes=[pltpu.VMEM((tm, d), BF16)],
        compiler_params=_params(("parallel", "parallel", "arbitrary")),
        name="in_proj",
    )(x, norm_w, mod3, mod3, w_bf)


def _log_sigmoid(x):
    return jnp.minimum(x, 0.0) - jnp.log(1.0 + jnp.exp(-jnp.abs(x)))


def _ret_kernel(q_ref, k_ref, v_ref, g_ref, kc_ref, vc_ref, cos_ref, sin_ref, decf_ref, decb_ref,
                gnw_ref, o_ref, q_scr, k_scr, kf_scr, kb_scr, v_scr, o_scr, din_scr, kvf_scr, kvb_scr,
                rf_scr, rb_scr, *, n_ctx):
    t = q_ref.shape[2]
    c = din_scr.shape[0]
    nc = t // c
    lgf = _log_sigmoid(decf_ref[0])[:, 0:1]
    lgb = _log_sigmoid(decb_ref[0])[:, 0:1]

    kc = _rot(kc_ref[0, 0], cos_ref[0:n_ctx, :], sin_ref[0:n_ctx, :])
    vc = vc_ref[0, 0].astype(BF16)
    tc = lax.broadcasted_iota(jnp.int32, (n_ctx, 1), 0).astype(F32)
    rf = _dot_tn((kc * jnp.exp(lgf * (n_ctx - 1.0 - tc))).astype(BF16), vc)
    rb = _dot_tn((kc * jnp.exp(lgb * tc)).astype(BF16), vc)

    scale = HEAD_DIM ** -0.5
    cos, sin = cos_ref[n_ctx:n_ctx + t, :], sin_ref[n_ctx:n_ctx + t, :]
    q_scr[...] = (_rot(q_ref[0, 0], cos, sin) * scale).astype(BF16)
    kr = _rot(k_ref[0, 0], cos, sin)
    ri_all = (lax.broadcasted_iota(jnp.int32, (t, HEAD_DIM), 0) % c).astype(F32)
    k_scr[...] = kr.astype(BF16)
    kf_scr[...] = (kr * jnp.exp(lgf * (c - 1.0 - ri_all))).astype(BF16)
    kb_scr[...] = (kr * jnp.exp(lgb * ri_all)).astype(BF16)
    v_scr[...] = v_ref[0, 0].astype(BF16)

    diff = (lax.broadcasted_iota(jnp.int32, (c, c), 0) - lax.broadcasted_iota(jnp.int32, (c, c), 1)).astype(F32)
    din_scr[...] = jnp.where(diff > 0, jnp.exp(lgf * jnp.maximum(diff, 0.0)),
                             jnp.where(diff < 0, jnp.exp(lgb * jnp.maximum(-diff, 0.0)), 2.0))

    for i in range(nc):
        sl = slice(i * c, (i + 1) * c)
        vi = v_scr[sl, :]
        scores = _dot_nt(q_scr[sl, :], k_scr[sl, :]) * din_scr[...]
        o_scr[sl, :] = jnp.dot(scores.astype(BF16), vi, preferred_element_type=F32)
        kvf_scr[i] = _dot_tn(kf_scr[sl, :], vi)
        kvb_scr[i] = _dot_tn(kb_scr[sl, :], vi)

    gc_f = jnp.exp(lgf * float(c))
    gc_b = jnp.exp(lgb * float(c))
    for i in range(nc):
        rf_scr[i] = rf.astype(BF16)
        rf = rf * gc_f + kvf_scr[i]
    for i in reversed(range(nc)):
        rb_scr[i] = rb.astype(BF16)
        rb = rb * gc_b + kvb_scr[i]

    ri = lax.broadcasted_iota(jnp.int32, (c, HEAD_DIM), 0).astype(F32)
    dq_f = jnp.exp(lgf * (ri + 1.0))
    dq_b = jnp.exp(lgb * (c - ri))
    for i in range(nc):
        sl = slice(i * c, (i + 1) * c)
        qi = q_scr[sl, :]
        o = (o_scr[sl, :] + jnp.dot(qi, rf_scr[i], preferred_element_type=F32) * dq_f
             + jnp.dot(qi, rb_scr[i], preferred_element_type=F32) * dq_b)
        mu = jnp.mean(o, axis=-1, keepdims=True)
        var = jnp.mean(jnp.square(o - mu), axis=-1, keepdims=True)
        on = (o - mu) * lax.rsqrt(var + GN_EPS) * gnw_ref[...]
        o_ref[0, sl, :] = (on * _silu(g_ref[0, 0, sl, :])).astype(o_ref.dtype)


def _retention(proj, ctx_kv, cos, sin, decf3, decb3, gn_w, n_heads, n_ctx):
    b, _, t, hd = proj.shape
    c = _tile(t, RET_CHUNK)
    nc = t // c
    lat = lambda g: pl.BlockSpec((1, 1, t, hd), lambda bi, h: (bi, g * n_heads + h, 0, 0))
    seq = lambda dt: pltpu.VMEM((t, hd), dt)
    return pl.pallas_call(
        functools.partial(_ret_kernel, n_ctx=n_ctx),
        out_shape=jax.ShapeDtypeStruct((b, t, n_heads * hd), BF16),
        grid=(b, n_heads),
        in_specs=[lat(0), lat(1), lat(2), lat(3),
                  pl.BlockSpec((1, 1, n_ctx, hd), lambda bi, h: (0, h, bi, 0)),
                  pl.BlockSpec((1, 1, n_ctx, hd), lambda bi, h: (0, n_heads + h, bi, 0)),
                  pl.BlockSpec(cos.shape, lambda bi, h: (0, 0)),
                  pl.BlockSpec(sin.shape, lambda bi, h: (0, 0)),
                  pl.BlockSpec((1, 1, LANES), lambda bi, h: (h, 0, 0)),
                  pl.BlockSpec((1, 1, LANES), lambda bi, h: (h, 0, 0)),
                  pl.BlockSpec((1, hd), lambda bi, h: (0, h))],
        out_specs=pl.BlockSpec((1, t, hd), lambda bi, h: (bi, 0, h)),
        scratch_shapes=[seq(BF16), seq(BF16), seq(BF16), seq(BF16), seq(BF16), seq(F32),
                        pltpu.VMEM((c, c), F32),
                        pltpu.VMEM((nc, hd, hd), F32), pltpu.VMEM((nc, hd, hd), F32),
                        pltpu.VMEM((nc, hd, hd), BF16), pltpu.VMEM((nc, hd, hd), BF16)],
        compiler_params=_params(("parallel", "parallel")),
        name="ret",
    )(proj, proj, proj, proj, ctx_kv, ctx_kv, cos, sin, decf3, decb3, gn_w)


CONV_PAD_ROWS = 16
CONV_LANE_CHUNK = 256


def _conv_kernel(a_ref, b_ref, w_ref, cb_ref, lnw_ref, lnb_ref, o_ref, up_scr, sh_scr, y_scr):
    nblk, tt = a_ref.shape[1], a_ref.shape[2]
    ch = nblk * LANES
    kw = w_ref.shape[0]
    n_seq = tt // GRID_W
    lead = CONV_PAD_ROWS - kw // 2
    rows = GRID_W + 2 * CONV_PAD_ROWS
    zeros = jnp.zeros((CONV_PAD_ROWS, ch), F32)
    for s in range(n_seq):
        up_scr[s, 0:CONV_PAD_ROWS, :] = zeros
        up_scr[s, CONV_PAD_ROWS + GRID_W:, :] = zeros
    for cb in range(nblk):
        u = a_ref[0, cb] * jax.nn.sigmoid(b_ref[0, cb])
        for s in range(n_seq):
            up_scr[s, CONV_PAD_ROWS:CONV_PAD_ROWS + GRID_W, cb * LANES:(cb + 1) * LANES] = (
                u[s * GRID_W:(s + 1) * GRID_W, :])

    cw = sh_scr.shape[2]

    def seq(s, carry):
        row0 = pl.multiple_of(s * GRID_W, GRID_W)
        for c0 in range(0, ch, cw):
            for r in range(SUBLANES):
                sh_scr[r] = up_scr[s, r:r + rows - SUBLANES, c0:c0 + cw]
            acc = jnp.broadcast_to(cb_ref[:, c0:c0 + cw], (GRID_W, cw))
            for k in range(kw):
                a8, r = divmod(lead + k, SUBLANES)
                acc = acc + sh_scr[r, a8 * SUBLANES:a8 * SUBLANES + GRID_W, :] * w_ref[k:k + 1, c0:c0 + cw]
            y_scr[pl.ds(row0, GRID_W), c0:c0 + cw] = acc
        return carry

    lax.fori_loop(0, n_seq, seq, 0)
    y = y_scr[...]
    mu = jnp.mean(y, axis=-1, keepdims=True)
    var = jnp.mean(jnp.square(y - mu), axis=-1, keepdims=True)
    yn = (y - mu) * lax.rsqrt(var + EPS) * lnw_ref[...] + lnb_ref[...]
    o_ref[0] = _silu(yn).astype(o_ref.dtype)


def _conv(proj, col0, conv_w, conv_b, ln_w, ln_b):
    b, _, t, _ = proj.shape
    kw, ch = conv_w.shape
    assert col0 % ch == 0 and ch % LANES == 0 and kw // 2 <= CONV_PAD_ROWS and t % GRID_W == 0
    nblk = ch // LANES
    tt = _tile(t, 8 * GRID_W)
    cw = min(ch, CONV_LANE_CHUNK)
    rows = GRID_W + 2 * CONV_PAD_ROWS
    ca, cb = col0 // ch, col0 // ch + 1
    vec = pl.BlockSpec((1, ch), lambda bi, i: (0, 0))
    return pl.pallas_call(
        _conv_kernel,
        out_shape=jax.ShapeDtypeStruct((b, t, ch), BF16),
        grid=(b, t // tt),
        in_specs=[pl.BlockSpec((1, nblk, tt, LANES), lambda bi, i: (bi, ca, i, 0)),
                  pl.BlockSpec((1, nblk, tt, LANES), lambda bi, i: (bi, cb, i, 0)),
                  pl.BlockSpec((kw, ch), lambda bi, i: (0, 0)),
                  vec, vec, vec],
        out_specs=pl.BlockSpec((1, tt, ch), lambda bi, i: (bi, i, 0)),
        scratch_shapes=[pltpu.VMEM((tt // GRID_W, rows, ch), F32),
                        pltpu.VMEM((SUBLANES, rows - SUBLANES, cw), F32),
                        pltpu.VMEM((tt, ch), F32)],
        compiler_params=_params(("parallel", "parallel")),
        name="conv",
    )(proj, proj, conv_w, conv_b, ln_w, ln_b)


def _outproj_kernel(yr_ref, yc_ref, wr_ref, wc_ref, x_ref, g1_ref, sh2_ref, sc2_ref, pmn_ref, pfn_ref,
                    rw_ref, rb_ref, x1_ref, h2p_ref, lg_ref):
    tm = x_ref.shape[1]
    e_rows = lg_ref.shape[0]
    mix = (jnp.dot(yr_ref[0], wr_ref[...], preferred_element_type=F32)
           + jnp.dot(yc_ref[0], wc_ref[...], preferred_element_type=F32))
    x1 = x_ref[0] + _rms(mix, g1_ref[0] * pmn_ref[...])
    x1_ref[0] = x1
    h2 = _rms(x1, pfn_ref[...] * (1.0 + sc2_ref[0])) + sh2_ref[0]
    h2_hi = h2.astype(BF16)
    h2_lo = (h2 - h2_hi.astype(F32)).astype(BF16)
    by_hi = _dot_nt(rw_ref[...], h2_hi)
    lg_ref[...] = by_hi[:e_rows] + by_hi[e_rows:] + _dot_nt(rw_ref[:e_rows, :], h2_lo) + rb_ref[...]
    _pack_rows(h2, h2p_ref, tm)


def _out_proj(y_ret, y_conv, w_out_bf, x, mod3, post_mix_norm, pre_ffn_norm, rw_hi_lo, rb_col):
    b, t, d = x.shape
    d_ret, d_conv = y_ret.shape[2], y_conv.shape[2]
    e_rows = rb_col.shape[0]
    seg = d // 2 // LANES
    tm = _tile(t, 512)
    per_b = t // tm
    mod = lambda col: pl.BlockSpec((1, 1, d), lambda bi, i: (bi, 0, col))
    vec = pl.BlockSpec((1, d), lambda bi, i: (0, 0))
    row = lambda width: pl.BlockSpec((1, tm, width), lambda bi, i: (bi, i, 0))
    rw = pl.BlockSpec((2 * e_rows, d), lambda bi, i: (0, 0))
    w_ret, w_conv = w_out_bf[:d_ret], w_out_bf[d_ret:]
    return pl.pallas_call(
        _outproj_kernel,
        out_shape=(jax.ShapeDtypeStruct((b, t, d), F32),
                   jax.ShapeDtypeStruct((2, b * t * seg, LANES), U32),
                   jax.ShapeDtypeStruct((e_rows, b * t), F32)),
        grid=(b, per_b),
        in_specs=[row(d_ret), row(d_conv),
                  pl.BlockSpec((d_ret, d), lambda bi, i: (0, 0)),
                  pl.BlockSpec((d_conv, d), lambda bi, i: (0, 0)),
                  row(d), mod(2), mod(3), mod(4), vec, vec, rw,
                  pl.BlockSpec((e_rows, 1), lambda bi, i: (0, 0))],
        out_specs=(row(d),
                   pl.BlockSpec((2, tm * seg, LANES), lambda bi, i: (0, bi * per_b + i, 0)),
                   pl.BlockSpec((e_rows, tm), lambda bi, i: (0, bi * per_b + i))),
        compiler_params=_params(("parallel", "parallel")),
        name="out_proj",
    )(y_ret, y_conv, w_ret, w_conv, x, mod3, mod3, mod3, post_mix_norm, pre_ffn_norm, rw_hi_lo, rb_col)


def _route_kernel(lg_ref, idx_ref, gate_ref, rank_ref, cnt_ref, tri_scr, run_scr, *, n_experts):
    e_rows, tr = lg_ref.shape

    @pl.when(pl.program_id(0) == 0)
    def _():
        r = lax.broadcasted_iota(jnp.int32, (tr, tr), 0)
        c = lax.broadcasted_iota(jnp.int32, (tr, tr), 1)
        tri_scr[...] = jnp.where(r <= c, 1.0, 0.0).astype(BF16)
        run_scr[...] = jnp.zeros_like(run_scr)

    e_iota = lax.broadcasted_iota(jnp.int32, (e_rows, tr), 0)
    neg = jnp.float32(-jnp.inf)
    logits = jnp.where(e_iota < n_experts, lg_ref[...], neg)
    vals, hots = [], []
    for k in range(TOP_K):
        m = jnp.max(logits, axis=0, keepdims=True)
        ik = jnp.min(jnp.where(logits == m, e_iota, e_rows), axis=0, keepdims=True)
        hot = e_iota == ik
        logits = jnp.where(hot, neg, logits)
        vals.append(m)
        hots.append(hot)
        idx_ref[k:k + 1, :] = ik
    exps = [jnp.exp(v - vals[0]) for v in vals]
    den = exps[0]
    for e in exps[1:]:
        den = den + e
    for k in range(TOP_K):
        gate_ref[k:k + 1, :] = exps[k] / den

    sel = jnp.zeros((e_rows, tr), F32)
    for hot in hots:
        sel = sel + jnp.where(hot, 1.0, 0.0)
    csum = jnp.dot(sel.astype(BF16), tri_scr[...], preferred_element_type=F32)
    before = run_scr[:, 0:1] + csum - sel
    for k in range(TOP_K):
        rank_ref[k:k + 1, :] = jnp.sum(jnp.where(hots[k], before, 0.0), axis=0, keepdims=True).astype(jnp.int32)
    run_scr[...] = run_scr[...] + jnp.sum(sel, axis=1, keepdims=True)
    cnt_ref[...] = run_scr[...]


def _route(logits_t, n_experts):
    e_rows, n = logits_t.shape
    tr = _tile(n, 512)
    kt = lambda dt: jax.ShapeDtypeStruct((TOP_K, n), dt)
    blk = pl.BlockSpec((TOP_K, tr), lambda i: (0, i))
    return pl.pallas_call(
        functools.partial(_route_kernel, n_experts=n_experts),
        out_shape=(kt(jnp.int32), kt(F32), kt(jnp.int32), jax.ShapeDtypeStruct((e_rows, LANES), F32)),
        grid=(n // tr,),
        in_specs=[pl.BlockSpec((e_rows, tr), lambda i: (0, i))],
        out_specs=(blk, blk, blk, pl.BlockSpec((e_rows, LANES), lambda i: (0, 0))),
        scratch_shapes=[pltpu.VMEM((tr, tr), BF16), pltpu.VMEM((e_rows, LANES), F32)],
        compiler_params=_params(("arbitrary",)),
        name="route",
    )(logits_t)


def _dispatch_kernel(pos_ref, poff_ref, plen_ref, h_ref, xs_ref, zero_scr, sem, zsem, *, n_tok, n_exp, seg, tm):
    td = h_ref.shape[1] // seg
    step = pl.program_id(0)
    base = step * td

    def pad_copies(e, act):
        off, ln = poff_ref[e], plen_ref[e]
        for bit in range(tm.bit_length() - 1):
            size = 1 << bit

            @pl.when(((ln >> bit) & 1) == 1)
            def _():
                row = pl.multiple_of((off + (ln & (size - 1))) * seg, seg)
                act(pltpu.make_async_copy(zero_scr.at[:, pl.ds(0, size * seg)],
                                          xs_ref.at[:, pl.ds(row, size * seg)], zsem))

    def for_pads(act):
        def body(e, carry):
            pad_copies(e, act)
            return carry
        lax.fori_loop(0, n_exp, body, 0)

        half = zero_scr.shape[1]
        end = (poff_ref[n_exp - 1] + plen_ref[n_exp - 1]) * seg

        def tail(n, carry):
            row = pl.multiple_of(end + n * half, seg)
            act(pltpu.make_async_copy(zero_scr, xs_ref.at[:, pl.ds(row, half)], zsem))
            return carry
        lax.fori_loop(0, (xs_ref.shape[1] - end) // half, tail, 0)

    @pl.when(step == 0)
    def _():
        zero_scr[...] = jnp.zeros_like(zero_scr)
        for_pads(lambda cp: cp.start())

    def row_copy(t, k):
        p = pos_ref[k * n_tok + base + t]
        return pltpu.make_async_copy(h_ref.at[:, pl.ds(pl.multiple_of(t * seg, seg), seg)],
                                     xs_ref.at[:, pl.ds(pl.multiple_of(p * seg, seg), seg)], sem)

    def start(t, carry):
        for k in range(TOP_K):
            row_copy(t, k).start(priority=k % 2)
        return carry

    lax.fori_loop(0, td, start, 0, unroll=8)
    for k in range(TOP_K):
        pltpu.make_async_copy(h_ref, h_ref, sem).wait()

    @pl.when(step == 0)
    def _():
        for_pads(lambda cp: cp.wait())


def _dispatch(pos_flat, pad_off, pad_len, h2p, n_slots, seg, tm):
    n = h2p.shape[1] // seg
    n_exp = pad_off.shape[0]
    assert tm & (tm - 1) == 0
    td = _tile(n, 512)
    return pl.pallas_call(
        functools.partial(_dispatch_kernel, n_tok=n, n_exp=n_exp, seg=seg, tm=tm),
        out_shape=jax.ShapeDtypeStruct((2, n_slots * seg, LANES), U32),
        grid_spec=pltpu.PrefetchScalarGridSpec(
            num_scalar_prefetch=3,
            grid=(n // td,),
            in_specs=[pl.BlockSpec((2, td * seg, LANES), lambda i, *_: (0, i, 0))],
            out_specs=pl.BlockSpec(memory_space=pl.ANY),
            scratch_shapes=[pltpu.VMEM((2, max(tm // 2, 1) * seg, LANES), U32),
                            pltpu.SemaphoreType.DMA(()), pltpu.SemaphoreType.DMA(())]),
        compiler_params=_params(("arbitrary",)),
        name="dispatch",
    )(pos_flat, pad_off, pad_len, h2p)


def _pack_rows(val, ref, tm):
    half = val.shape[1] // 2
    seg = half // LANES
    for p in range(2):
        for s in range(seg):
            c0 = p * half + s * LANES
            ref[p, pl.ds(s, tm, stride=seg), :] = val[:, c0:c0 + LANES]


def _weight_runs(tile_expert, n_passes):
    n_tiles = tile_expert.shape[0]
    total = n_tiles * n_passes
    e_lin = jnp.tile(tile_expert, n_passes)
    j_lin = jnp.repeat(jnp.arange(n_passes, dtype=jnp.int32), n_tiles)
    i_lin = jnp.tile(jnp.arange(n_tiles, dtype=jnp.int32), n_passes)
    start = (i_lin == 0) | (e_lin != jnp.roll(e_lin, 1))
    idx = jnp.arange(total, dtype=jnp.int32)
    nxt = jnp.min(jnp.where(start[None, :] & (idx[None, :] > idx[:, None]), idx[None, :], total), axis=1)
    has = nxt < total
    nxt = jnp.minimum(nxt, total - 1)
    return start.astype(jnp.int32), jnp.where(has, e_lin[nxt], -1).astype(jnp.int32), j_lin[nxt]


def _stream_weights(step, first_ref, ne_ref, nj_ref, cur, copies, cast):
    @pl.when(step == 0)
    def _():
        for cp in copies(*cur):
            cp.start()

    @pl.when(first_ref[step] == 1)
    def _():
        for cp in copies(*cur):
            cp.wait()
        cast()

        @pl.when(ne_ref[step] >= 0)
        def _():
            for cp in copies(ne_ref[step], nj_ref[step]):
                cp.start()


def _ffn1_kernel(te_ref, nu_ref, first_ref, ne_ref, nj_ref, xs_ref, w_hbm, bg_ref, bl_ref, act_ref,
                 wg_stage, wl_stage, wg_scr, wl_scr, x_scr, sem, *, seg, d_ff):
    j, i = pl.program_id(0), pl.program_id(1)
    tm = x_scr.shape[0]
    tn = wg_scr.shape[1]
    half = seg * LANES

    def copies(e, jj):
        col = pl.multiple_of(jj * tn, tn)
        return (pltpu.make_async_copy(w_hbm.at[e, :, pl.ds(col, tn)], wg_stage, sem.at[0]),
                pltpu.make_async_copy(w_hbm.at[e, :, pl.ds(d_ff + col, tn)], wl_stage, sem.at[1]))

    def cast():
        wg_scr[...] = wg_stage[...].astype(BF16)
        wl_scr[...] = wl_stage[...].astype(BF16)

    _stream_weights(j * pl.num_programs(1) + i, first_ref, ne_ref, nj_ref, (te_ref[i], j), copies, cast)

    @pl.when(i < nu_ref[0])
    def _():
        for p in range(2):
            for s in range(seg):
                c0 = p * half + s * LANES
                x_scr[:, c0:c0 + LANES] = xs_ref[p, pl.ds(s, tm, stride=seg), :].astype(BF16)
        x = x_scr[...]
        glu = jnp.minimum(jnp.dot(x, wg_scr[...], preferred_element_type=F32) + bg_ref[0], SWIGLU_LIMIT)
        lin = jnp.clip(jnp.dot(x, wl_scr[...], preferred_element_type=F32) + bl_ref[0],
                       -SWIGLU_LIMIT, SWIGLU_LIMIT)
        act_ref[...] = (glu * jax.nn.sigmoid(SWIGLU_ALPHA * glu) * (lin + 1.0)).astype(act_ref.dtype)

    @pl.when(i >= nu_ref[0])
    def _():
        act_ref[...] = jnp.zeros_like(act_ref)


def _ffn1(tile_expert, n_used, xs, w1, b1_3, tm, seg):
    d = seg * LANES * 2
    slots = xs.shape[1] // seg
    d_ff = w1.shape[2] // 2
    tn = _tile(d_ff, 1024)
    nj = d_ff // tn
    n_tiles = slots // tm
    row = lambda i, nu: jnp.minimum(i, nu[0] - 1)
    return pl.pallas_call(
        functools.partial(_ffn1_kernel, seg=seg, d_ff=d_ff),
        out_shape=jax.ShapeDtypeStruct((slots, d_ff), BF16),
        grid_spec=pltpu.PrefetchScalarGridSpec(
            num_scalar_prefetch=5,
            grid=(nj, n_tiles),
            in_specs=[pl.BlockSpec((2, tm * seg, LANES), lambda j, i, te, nu, *_: (0, row(i, nu), 0)),
                      pl.BlockSpec(memory_space=pl.ANY),
                      pl.BlockSpec((1, 1, tn), lambda j, i, te, *_: (te[i], 0, j)),
                      pl.BlockSpec((1, 1, tn), lambda j, i, te, *_: (te[i], 0, nj + j))],
            out_specs=pl.BlockSpec((tm, tn), lambda j, i, *_: (i, j)),
            scratch_shapes=[pltpu.VMEM((d, tn), F32), pltpu.VMEM((d, tn), F32),
                            pltpu.VMEM((d, tn), BF16), pltpu.VMEM((d, tn), BF16),
                            pltpu.VMEM((tm, d), BF16), pltpu.SemaphoreType.DMA((2,))]),
        compiler_params=_params(("arbitrary", "arbitrary")),
        name="ffn1",
    )(tile_expert, n_used, *_weight_runs(tile_expert, nj), xs, w1, b1_3, b1_3)


def _ffn2_kernel(te_ref, nu_ref, first_ref, ne_ref, nj_ref, act_ref, w_hbm, b_ref, ys_ref, w_stage, w_scr, sem):
    i = pl.program_id(0)
    tm = act_ref.shape[0]

    def copies(e, jj):
        del jj
        return (pltpu.make_async_copy(w_hbm.at[e], w_stage, sem),)

    def cast():
        w_scr[...] = w_stage[...].astype(BF16)

    _stream_weights(i, first_ref, ne_ref, nj_ref, (te_ref[i], 0), copies, cast)

    @pl.when(i < nu_ref[0])
    def _():
        _pack_rows(jnp.dot(act_ref[...], w_scr[...], preferred_element_type=F32) + b_ref[0], ys_ref, tm)

    @pl.when(i >= nu_ref[0])
    def _():
        ys_ref[...] = jnp.zeros_like(ys_ref)


def _ffn2(tile_expert, n_used, act, w2, b2_3, tm):
    slots, d_ff = act.shape
    d = w2.shape[2]
    seg = d // 2 // LANES
    n_tiles = slots // tm
    row = lambda i, nu: jnp.minimum(i, nu[0] - 1)
    return pl.pallas_call(
        _ffn2_kernel,
        out_shape=jax.ShapeDtypeStruct((2, slots * seg, LANES), U32),
        grid_spec=pltpu.PrefetchScalarGridSpec(
            num_scalar_prefetch=5,
            grid=(n_tiles,),
            in_specs=[pl.BlockSpec((tm, d_ff), lambda i, te, nu, *_: (row(i, nu), 0)),
                      pl.BlockSpec(memory_space=pl.ANY),
                      pl.BlockSpec((1, 1, d), lambda i, te, *_: (te[i], 0, 0))],
            out_specs=pl.BlockSpec((2, tm * seg, LANES), lambda i, *_: (0, i, 0)),
            scratch_shapes=[pltpu.VMEM((d_ff, d), F32), pltpu.VMEM((d_ff, d), BF16),
                            pltpu.SemaphoreType.DMA(())]),
        compiler_params=_params(("arbitrary",)),
        name="ffn2",
    )(tile_expert, n_used, *_weight_runs(tile_expert, 1), act, w2, b2_3)


def _combine_kernel(pos_ref, ys_ref, gate_ref, x1_ref, g2_ref, pfn_ref, o_ref, buf, moe_scr, sem, *, n_tok):
    tc, d = x1_ref.shape
    seg = buf.shape[3] // tc
    half = seg * LANES
    step = pl.program_id(0)

    last = pl.num_programs(0) - 1

    def start_row(st, slot, t):
        for k in range(TOP_K):
            p = pos_ref[k * n_tok + st * tc + t]
            pltpu.make_async_copy(ys_ref.at[:, pl.ds(pl.multiple_of(p * seg, seg), seg)],
                                  buf.at[slot, k, :, pl.ds(pl.multiple_of(t * seg, seg), seg)],
                                  sem.at[slot, k]).start(priority=k % 2)

    def wait_rows(slot):
        for k in range(TOP_K):
            pltpu.make_async_copy(buf.at[slot, k], buf.at[slot, k], sem.at[slot, k]).wait()

    @pl.when(step == 0)
    def _():
        def body(t, carry):
            start_row(0, 0, t)
            return carry
        lax.fori_loop(0, tc, body, 0)

    slot = step % 2
    wait_rows(slot)
    for t in range(tc):
        start_row(jnp.minimum(step + 1, last), 1 - slot, t)
    gates = [jnp.broadcast_to(gate_ref[:, k:k + 1], (tc, LANES)) for k in range(TOP_K)]
    for p in range(2):
        for s in range(seg):
            m = None
            for k in range(TOP_K):
                y = buf[slot, k, p, pl.ds(s, tc, stride=seg), :]
                m = y * gates[k] if m is None else m + y * gates[k]
            c0 = p * half + s * LANES
            moe_scr[:, c0:c0 + LANES] = m
    o_ref[...] = x1_ref[...] + g2_ref[0] * _rms(moe_scr[...], pfn_ref[...])

    @pl.when(step == last)
    def _():
        wait_rows(1 - slot)


def _combine(pos_flat, ys, gates_t, x1, mod3, post_ffn_norm, seq):
    n, d = x1.shape
    seg = d // 2 // LANES
    tc = _tile(seq, 256)
    per_batch = seq // tc
    return pl.pallas_call(
        functools.partial(_combine_kernel, n_tok=n),
        out_shape=jax.ShapeDtypeStruct((n, d), F32),
        grid_spec=pltpu.PrefetchScalarGridSpec(
            num_scalar_prefetch=1,
            grid=(n // tc,),
            in_specs=[pl.BlockSpec(memory_space=pl.ANY),
                      pl.BlockSpec((tc, TOP_K), lambda i, pos: (i, 0)),
                      pl.BlockSpec((tc, d), lambda i, pos: (i, 0)),
                      pl.BlockSpec((1, 1, d), lambda i, pos: (i // per_batch, 0, 5)),
                      pl.BlockSpec((1, d), lambda i, pos: (0, 0))],
            out_specs=pl.BlockSpec((tc, d), lambda i, pos: (i, 0)),
            scratch_shapes=[pltpu.VMEM((2, TOP_K, 2, tc * seg, LANES), U32), pltpu.VMEM((tc, d), F32),
                            pltpu.SemaphoreType.DMA((2, TOP_K))]),
        compiler_params=_params(("arbitrary",)),
        name="combine",
    )(pos_flat, ys, gates_t, x1, mod3, post_ffn_norm)


def _moe(h2p, logits_t, x1, mod3, post_ffn_norm, w1, b1, w2, b2, seq, n_exp):
    n, d = x1.shape
    seg = d // 2 // LANES
    tm = 1 << (min(n * TOP_K, 512).bit_length() - 1)
    idx, gates, rank, cnt = _route(logits_t, n_exp)

    counts = cnt[:n_exp, 0].astype(jnp.int32)
    padded = (counts + tm - 1) // tm * tm
    ends = jnp.cumsum(padded)
    starts = ends - padded
    hot = idx[:, :, None] == jnp.arange(n_exp, dtype=jnp.int32)
    pos_flat = (jnp.sum(jnp.where(hot, starts, 0), axis=-1) + rank).reshape(-1)
    n_tiles = -(-n * TOP_K // tm) + n_exp
    n_used = (ends[-1] // tm).astype(jnp.int32)
    tile_start = jnp.minimum(jnp.arange(n_tiles, dtype=jnp.int32), n_used - 1) * tm
    tile_expert = jnp.sum(tile_start[:, None] >= ends[None, :], axis=1).astype(jnp.int32)
    n_used = n_used.reshape(1)

    xs = _dispatch(pos_flat, starts + counts, padded - counts, h2p, n_tiles * tm, seg, tm)
    act = _ffn1(tile_expert, n_used, xs, w1, b1[:, None, :], tm, seg)
    ys = _ffn2(tile_expert, n_used, act, w2, b2[:, None, :], tm)
    return _combine(pos_flat, ys, gates.T, x1, mod3, post_ffn_norm, seq)


def kernel(x, c, ctx, c_ctx, ada_w, ada_b, pre_mix_norm, post_mix_norm, pre_ffn_norm, post_ffn_norm,
           w_in, ret_decay_fwd, ret_decay_bwd, ret_gn_w, conv_w, conv_b, conv_ln_w, conv_ln_b, w_out,
           router_w, router_b, w1, b1, w2, b2):
    assert ada_w.shape[0] == 1, "single-layer stack only"
    b, t, d = x.shape
    n_ctx = ctx.shape[1]
    n_heads = ret_decay_fwd.shape[1]
    d_ret = n_heads * HEAD_DIM
    n_exp = router_w.shape[2]
    assert b < MOD_ROWS and ret_gn_w.shape[1] == d_ret and d % (2 * LANES) == 0

    cc = jnp.zeros((MOD_ROWS, d), F32).at[:b].set(c).at[b].set(c_ctx)
    mod3 = _ada(cc, ada_w[0], ada_b)[:, None, :]
    cos, sin = _rope_tables(n_ctx + t)

    w_in_bf = w_in[0].astype(BF16)
    proj = _in_proj(x, pre_mix_norm, mod3, lambda bi: bi, w_in_bf, 0, w_in.shape[2])
    ctx_kv = _in_proj(ctx.reshape(1, b * n_ctx, d), pre_mix_norm, mod3, lambda bi: b, w_in_bf, d_ret, 2 * d_ret)

    lane_bcast = lambda v: jnp.broadcast_to(v.reshape(n_heads, 1, 1), (n_heads, 1, LANES))
    y_ret = _retention(proj, ctx_kv, cos, sin, lane_bcast(ret_decay_fwd[0]), lane_bcast(ret_decay_bwd[0]),
                       ret_gn_w, n_heads, n_ctx)
    y_conv = _conv(proj, 4 * d_ret, conv_w[0], conv_b, conv_ln_w, conv_ln_b)

    e_rows = -(-n_exp // SUBLANES) * SUBLANES
    rw_t = jnp.zeros((e_rows, d), F32).at[:n_exp].set(router_w[0].T)
    rw_hi = rw_t.astype(BF16)
    rw_lo = (rw_t - rw_hi.astype(F32)).astype(BF16)
    rb_col = jnp.zeros((e_rows, 1), F32).at[:n_exp, 0].set(router_b[0])
    x1, h2p, logits_t = _out_proj(y_ret, y_conv, w_out[0].astype(BF16), x, mod3, post_mix_norm, pre_ffn_norm,
                                  jnp.concatenate([rw_hi, rw_lo], axis=0), rb_col)

    out = _moe(h2p, logits_t, x1.reshape(b * t, d), mod3, post_ffn_norm, w1[0], b1[0], w2[0], b2[0], t, n_exp)
    return out.reshape(b, t, d)
```
